```python
import jax, jax.numpy as jnp
from jax import lax
import numpy as np

D_MODEL = 2048
BATCH = 8
SEQ = 4096
DEPTH = 4

RET_H = 8
RET_DK = D_MODEL // 16
RET_DV = 2 * RET_DK
RET_CHUNK = 128
ROPE_BASE = 10000.0
FOX_H = 16
FOX_DH = D_MODEL // 16
FOX_BLOCK = 128
CONV_C = D_MODEL
CONV_W = 31
N_BRANCH = 3
N_IN = (2 * RET_H * RET_DK + 2 * RET_H * RET_DV + 3 * FOX_H * FOX_DH + FOX_H
        + 2 * CONV_C + N_BRANCH * D_MODEL)
N_EXPERTS = 64
TOP_K = 6
EXPERT_FF = D_MODEL // 8
ROUTED_SCALE = 2.5
MOE_BLOCK = 512
ALPHA = (2 * DEPTH) ** 0.25
BETA = (8 * DEPTH) ** -0.25
LN_EPS = 1e-5

kernel_name = "hybrid_ret_fox_conformer_moe_deepnorm"


def layernorm(x, g, b):
    xf = x.astype(jnp.float32)
    mu = jnp.mean(xf, axis=-1, keepdims=True)
    var = jnp.mean(jnp.square(xf - mu), axis=-1, keepdims=True)
    y = (xf - mu) * lax.rsqrt(var + LN_EPS)
    return (y * g.astype(jnp.float32) + b.astype(jnp.float32)).astype(x.dtype)


def rotary(x):
    s, d = x.shape[1], x.shape[-1]
    half = d // 2
    inv = 1.0 / (ROPE_BASE ** (jnp.arange(half, dtype=jnp.float32) / half))
    ang = jnp.arange(s, dtype=jnp.float32)[:, None] * inv[None, :]
    cos = jnp.cos(ang)[None, :, None, :]
    sin = jnp.sin(ang)[None, :, None, :]
    x1, x2 = x[..., :half], x[..., half:]
    return jnp.concatenate([x1 * cos - x2 * sin, x1 * sin + x2 * cos], axis=-1)


def retention(q, k, v):
    b, s, h, dk = q.shape
    dv = v.shape[-1]
    L = RET_CHUNK
    n = s // L
    log_g = jnp.log1p(-jnp.exp2(-5.0 - jnp.arange(h, dtype=jnp.float32)))
    q = q.reshape(b, n, L, h, dk).transpose(1, 0, 3, 2, 4)
    k = (k * dk ** -0.5).reshape(b, n, L, h, dk).transpose(1, 0, 3, 2, 4)
    v = v.reshape(b, n, L, h, dv).transpose(1, 0, 3, 2, 4)
    i = jnp.arange(L, dtype=jnp.float32)
    diff = i[:, None] - i[None, :]
    decay = jnp.where(diff >= 0, jnp.exp(log_g[:, None, None] * jnp.maximum(diff, 0.0)), 0.0)
    inner = jnp.einsum('nbhld,nbhmd->nbhlm', q, k) * decay
    inner_o = jnp.einsum('nbhlm,nbhme->nbhle', inner, v)
    xi = jnp.exp(log_g[:, None] * (i + 1.0))[:, :, None]
    zeta = jnp.exp(log_g[:, None] * (L - 1.0 - i))[:, :, None]
    chunk_decay = jnp.exp(log_g * L)[:, None, None]

    def step(state, qkv):
        qc, kc, vc = qkv
        cross = jnp.einsum('bhld,bhde->bhle', qc * xi, state)
        state = state * chunk_decay + jnp.einsum('bhld,bhle->bhde', kc * zeta, vc)
        return state, cross

    state0 = jnp.zeros((b, h, dk, dv), jnp.float32)
    _, cross = lax.scan(step, state0, (q, k, v))
    o = inner_o + cross
    return o.transpose(1, 0, 3, 2, 4).reshape(b, s, h, dv)


def forgetting_attention(q, k, v, f_logit):
    s = q.shape[1]
    scale = q.shape[-1] ** -0.5
    c = jnp.cumsum(jax.nn.log_sigmoid(f_logit.astype(jnp.float32)), axis=1)
    c = c.transpose(0, 2, 1)
    outs = []
    for blk in range(s // FOX_BLOCK):
        q0 = blk * FOX_BLOCK
        kend = q0 + FOX_BLOCK
        sc = jnp.einsum('bqhd,bkhd->bhqk', q[:, q0:kend], k[:, :kend]).astype(jnp.float32) * scale
        sc = sc + (c[:, :, q0:kend, None] - c[:, :, None, :kend])
        mask = jnp.arange(kend)[None, :] <= jnp.arange(q0, kend)[:, None]
        p = jax.nn.softmax(jnp.where(mask, sc, -jnp.inf), axis=-1)
        outs.append(jnp.einsum('bhqk,bkhd->bqhd', p.astype(v.dtype), v[:, :kend]))
    return jnp.concatenate(outs, axis=1)


def conformer_conv(u, conv_w, conv_b, ln_g, ln_b):
    a, g = jnp.split(u, 2, axis=-1)
    y = a * jax.nn.sigmoid(g)
    y = lax.conv_general_dilated(
        y, conv_w[:, None, :], window_strides=(1,), padding=[(CONV_W - 1, 0)],
        dimension_numbers=('NWC', 'WIO', 'NWC'), feature_group_count=CONV_C)
    y = layernorm(y + conv_b, ln_g, ln_b)
    return jax.nn.silu(y)


def mixer_sublayer(x, w_in, b_in, ret_gn_g, conv_w, conv_b, conv_ln_g, conv_ln_b,
                   w_ret_o, w_fox_o, w_conv_o, w_out):
    b, s, d = x.shape
    z = jnp.einsum('bsd,dn->bsn', x, w_in) + b_in
    sizes = (RET_H * RET_DK, RET_H * RET_DK, RET_H * RET_DV, RET_H * RET_DV,
             FOX_H * FOX_DH, FOX_H * FOX_DH, FOX_H * FOX_DH, FOX_H,
             2 * CONV_C, N_BRANCH * D_MODEL)
    points = []
    acc = 0
    for sz in sizes[:-1]:
        acc += sz
        points.append(acc)
    rq, rk, rv, rg, fq, fk, fv, ff, cu, gl = jnp.split(z, points, axis=-1)

    rq = rotary(rq.reshape(b, s, RET_H, RET_DK).astype(jnp.float32))
    rk = rotary(rk.reshape(b, s, RET_H, RET_DK).astype(jnp.float32))
    rv = rv.reshape(b, s, RET_H, RET_DV).astype(jnp.float32)
    ro = retention(rq, rk, rv)
    mu = jnp.mean(ro, axis=-1, keepdims=True)
    var = jnp.mean(jnp.square(ro - mu), axis=-1, keepdims=True)
    ro = ((ro - mu) * lax.rsqrt(var + LN_EPS)).reshape(b, s, RET_H * RET_DV)
    ro = (ro * ret_gn_g.astype(jnp.float32)).astype(x.dtype)
    y_ret = jnp.einsum('bse,ed->bsd', jax.nn.silu(rg) * ro, w_ret_o)

    fo = forgetting_attention(fq.reshape(b, s, FOX_H, FOX_DH), fk.reshape(b, s, FOX_H, FOX_DH),
                              fv.reshape(b, s, FOX_H, FOX_DH), ff)
    y_fox = jnp.einsum('bse,ed->bsd', fo.reshape(b, s, FOX_H * FOX_DH), w_fox_o)

    co = conformer_conv(cu, conv_w, conv_b, conv_ln_g, conv_ln_b)
    y_conv = jnp.einsum('bse,ed->bsd', co, w_conv_o)

    gates = jax.nn.sigmoid(gl.reshape(b, s, N_BRANCH, D_MODEL))
    merged = gates[:, :, 0] * y_ret + gates[:, :, 1] * y_fox + gates[:, :, 2] * y_conv
    return jnp.einsum('bsd,de->bse', merged, w_out)


def routed_experts(h, idx, wts, w_gate, w_up, w_down):
    t, d = h.shape
    a = t * TOP_K
    n_blocks = -(-a // MOE_BLOCK) + N_EXPERTS
    r = n_blocks * MOE_BLOCK
    flat_e = idx.reshape(a)
    order = jnp.argsort(flat_e)
    sorted_e = flat_e[order]
    sizes = jnp.bincount(flat_e, length=N_EXPERTS)
    padded = (sizes + MOE_BLOCK - 1) // MOE_BLOCK * MOE_BLOCK
    pad_end = jnp.cumsum(padded)
    pad_start = pad_end - padded
    grp_start = jnp.cumsum(sizes) - sizes
    dest = pad_start[sorted_e] + jnp.arange(a) - grp_start[sorted_e]
    row_tok = jnp.full((r,), t, jnp.int32).at[dest].set((order // TOP_K).astype(jnp.int32))
    row_w = jnp.zeros((r,), h.dtype).at[dest].set(wts.reshape(a)[order].astype(h.dtype))
    block_e = jnp.minimum(jnp.searchsorted(pad_end, jnp.arange(n_blocks) * MOE_BLOCK, side='right'),
                          N_EXPERTS - 1)
    h_pad = jnp.concatenate([h, jnp.zeros((1, d), h.dtype)], axis=0)

    def step(out, blk):
        tok_b, w_b, e = blk
        xb = h_pad[tok_b]
        yb = (jax.nn.silu(xb @ w_gate[e]) * (xb @ w_up[e])) @ w_down[e]
        return out.at[tok_b].add(yb * w_b[:, None]), None

    out, _ = lax.scan(step, jnp.zeros((t + 1, d), h.dtype),
                      (row_tok.reshape(n_blocks, MOE_BLOCK), row_w.reshape(n_blocks, MOE_BLOCK), block_e))
    return out[:t]


def moe_sublayer(h, w_router, b_router, w_exp_gate, w_exp_up, w_exp_down, w_sh_gate, w_sh_up, w_sh_down):
    b, s, d = h.shape
    h2 = h.reshape(b * s, d)
    scores = jax.nn.sigmoid(jnp.einsum('td,de->te', h2, w_router).astype(jnp.float32))
    _, idx = lax.top_k(scores + b_router.astype(jnp.float32), TOP_K)
    w = jnp.take_along_axis(scores, idx, axis=-1)
    w = w / jnp.sum(w, axis=-1, keepdims=True) * ROUTED_SCALE
    routed = routed_experts(h2, idx, w, w_exp_gate, w_exp_up, w_exp_down)
    shared = (jax.nn.silu(h2 @ w_sh_gate) * (h2 @ w_sh_up)) @ w_sh_down
    return (routed + shared).reshape(b, s, d)


def setup_inputs(seed: int = 0) -> dict:
    key = jax.random.key(seed)
    ks = jax.random.split(key, 32)
    L, D, E, F = DEPTH, D_MODEL, N_EXPERTS, EXPERT_FF
    RV = RET_H * RET_DV
    FV = FOX_H * FOX_DH

    def nrm(k, shape, scale):
        return jax.random.normal(k, shape, jnp.float32) * scale

    return {
        "x": nrm(ks[0], (BATCH, SEQ, D), 1.0),
        "w_in": nrm(ks[1], (L, D, N_IN), D ** -0.5),
        "b_in": nrm(ks[2], (L, N_IN), 0.02),
        "ret_gn_g": 1.0 + nrm(ks[3], (L, RV), 0.02),
        "conv_w": nrm(ks[4], (L, CONV_W, CONV_C), CONV_W ** -0.5),
        "conv_b": nrm(ks[5], (L, CONV_C), 0.02),
        "conv_ln_g": 1.0 + nrm(ks[6], (L, CONV_C), 0.02),
        "conv_ln_b": nrm(ks[7], (L, CONV_C), 0.02),
        "w_ret_o": nrm(ks[8], (L, RV, D), RV ** -0.5 * BETA),
        "w_fox_o": nrm(ks[9], (L, FV, D), FV ** -0.5 * BETA),
        "w_conv_o": nrm(ks[10], (L, CONV_C, D), CONV_C ** -0.5 * BETA),
        "w_out": nrm(ks[11], (L, D, D), D ** -0.5 * BETA),
        "ln1_g": 1.0 + nrm(ks[12], (L, D), 0.02),
        "ln1_b": nrm(ks[13], (L, D), 0.02),
        "w_router": nrm(ks[14], (L, D, E), D ** -0.5),
        "b_router": nrm(ks[15], (L, E), 0.01),
        "w_exp_gate": nrm(ks[16], (L, E, D, F), D ** -0.5),
        "w_exp_up": nrm(ks[17], (L, E, D, F), D ** -0.5),
        "w_exp_down": nrm(ks[18], (L, E, F, D), F ** -0.5 * BETA),
        "w_sh_gate": nrm(ks[19], (L, D, F), D ** -0.5),
        "w_sh_up": nrm(ks[20], (L, D, F), D ** -0.5),
        "w_sh_down": nrm(ks[21], (L, F, D), F ** -0.5 * BETA),
        "ln2_g": 1.0 + nrm(ks[22], (L, D), 0.02),
        "ln2_b": nrm(ks[23], (L, D), 0.02),
    }


def reference(x, w_in, b_in, ret_gn_g, conv_w, conv_b, conv_ln_g, conv_ln_b,
              w_ret_o, w_fox_o, w_conv_o, w_out, ln1_g, ln1_b,
              w_router, b_router, w_exp_gate, w_exp_up, w_exp_down,
              w_sh_gate, w_sh_up, w_sh_down, ln2_g, ln2_b):
    h = x
    for l in range(DEPTH):
        mix = mixer_sublayer(h, w_in[l], b_in[l], ret_gn_g[l], conv_w[l], conv_b[l],
                             conv_ln_g[l], conv_ln_b[l], w_ret_o[l], w_fox_o[l],
                             w_conv_o[l], w_out[l])
        h = layernorm(ALPHA * h + mix, ln1_g[l], ln1_b[l])
        ffn = moe_sublayer(h, w_router[l], b_router[l], w_exp_gate[l], w_exp_up[l],
                           w_exp_down[l], w_sh_gate[l], w_sh_up[l], w_sh_down[l])
        h = layernorm(ALPHA * h + ffn, ln2_g[l], ln2_b[l])
    return h
```

```python
import functools

import jax
import jax.numpy as jnp
from jax import lax
from jax.experimental import pallas as pl
from jax.experimental.pallas import tpu as pltpu

F32 = jnp.float32
BF16 = jnp.bfloat16

RET_H = 8
RET_DK = 128
RET_DV = 256
ROPE_BASE = 10000.0
FOX_H = 16
FOX_DH = 128
CONV_W = 31
N_BRANCH = 3
TOP_K = 6
ROUTED_SCALE = 2.5
DEPTH_FOR_NORM = 4
ALPHA = (2 * DEPTH_FOR_NORM) ** 0.25
LN_EPS = 1e-5

LANES = 128
SUBLANES = 8
VMEM_LIMIT = 56 * 1024 * 1024

RET_CHUNK = 256
FOX_TQ = 256
FOX_TK = 256
CUM_BLK = 256
CONV_TS = 256
CONV_HALO = 32
CONV_RB = 64
CONV_CB = 256
MOE_BLOCK = 512
NEG_BIG = -1e30


def _params(*sem):
    return pltpu.CompilerParams(dimension_semantics=sem, vmem_limit_bytes=VMEM_LIMIT)


def _sigmoid(x):
    return 1.0 / (1.0 + jnp.exp(-x))


def _layernorm_rows(x, g, b):
    mu = jnp.mean(x, axis=-1, keepdims=True)
    xc = x - mu
    var = jnp.mean(xc * xc, axis=-1, keepdims=True)
    return xc * lax.rsqrt(var + LN_EPS) * g + b


def _in_proj_kernel(x_ref, w_ref, b_ref, o_ref):
    acc = jnp.dot(x_ref[...], w_ref[...], preferred_element_type=F32)
    o_ref[...] = acc + b_ref[...]


def in_proj(x_bf, w_bf, b, tm=1024, tn=512):
    m, k = x_bf.shape
    n = w_bf.shape[1]
    tm = min(tm, m)
    return pl.pallas_call(
        _in_proj_kernel,
        grid=(m // tm, n // tn),
        in_specs=[pl.BlockSpec((tm, k), lambda i, j: (i, 0)),
                  pl.BlockSpec((k, tn), lambda i, j: (0, j)),
                  pl.BlockSpec((1, tn), lambda i, j: (0, j))],
        out_specs=pl.BlockSpec((tm, tn), lambda i, j: (i, j)),
        out_shape=jax.ShapeDtypeStruct((m, n), F32),
        compiler_params=_params("parallel", "parallel"),
        name="in_proj",
    )(x_bf, w_bf, b)


def _retention_kernel(cd_ref, q_ref, k_ref, v_ref, g_ref, cos_ref, sin_ref, decay_ref,
                      xi_ref, zeta_ref, gn_ref, o_ref, state_ref):
    @pl.when(pl.program_id(1) == 0)
    def _():
        state_ref[...] = jnp.zeros_like(state_ref)

    cos = cos_ref[...]
    sin = sin_ref[...]
    for h in range(RET_H):
        qs = slice(h * RET_DK, (h + 1) * RET_DK)
        vs = slice(h * RET_DV, (h + 1) * RET_DV)
        q = q_ref[0, :, qs]
        k = k_ref[0, :, qs]
        qr = q * cos + pltpu.roll(q, RET_DK // 2, 1) * sin
        kr = (k * cos + pltpu.roll(k, RET_DK // 2, 1) * sin) * (RET_DK ** -0.5)
        vb = v_ref[0, :, vs].astype(BF16)
        s = lax.dot_general(qr.astype(BF16), kr.astype(BF16), (((1,), (1,)), ((), ())),
                            preferred_element_type=F32)
        inner = (s * decay_ref[h]).astype(BF16)
        o = jnp.dot(inner, vb, preferred_element_type=F32)
        st = state_ref[h]
        o = o + jnp.dot((qr * xi_ref[h]).astype(BF16), st.astype(BF16),
                        preferred_element_type=F32)
        kz = (kr * zeta_ref[h]).astype(BF16)
        state_ref[h] = st * cd_ref[h] + lax.dot_general(
            kz, vb, (((0,), (0,)), ((), ())), preferred_element_type=F32)
        mu = jnp.mean(o, axis=-1, keepdims=True)
        oc = o - mu
        var = jnp.mean(oc * oc, axis=-1, keepdims=True)
        on = oc * lax.rsqrt(var + LN_EPS) * gn_ref[:, vs]
        g = g_ref[0, :, vs]
        o_ref[0, :, vs] = (g * _sigmoid(g) * on).astype(BF16)


def _retention_tables(seq, chunk):
    h = jnp.arange(RET_H, dtype=F32)
    log_g = jnp.log1p(-jnp.exp2(-5.0 - h))
    i = jnp.arange(chunk, dtype=F32)
    diff = i[:, None] - i[None, :]
    decay = jnp.where(diff >= 0, jnp.exp(log_g[:, None, None] * jnp.maximum(diff, 0.0)), 0.0)
    xi = jnp.exp(log_g[:, None] * (i + 1.0))
    zeta = jnp.exp(log_g[:, None] * (chunk - 1.0 - i))
    xi = jnp.broadcast_to(xi[:, :, None], (RET_H, chunk, RET_DK))
    zeta = jnp.broadcast_to(zeta[:, :, None], (RET_H, chunk, RET_DK))
    cd = jnp.exp(log_g * chunk)
    half = RET_DK // 2
    inv = 1.0 / (ROPE_BASE ** (jnp.arange(half, dtype=F32) / half))
    ang = jnp.arange(seq, dtype=F32)[:, None] * inv[None, :]
    cos = jnp.cos(ang)
    sin = jnp.sin(ang)
    cos2 = jnp.concatenate([cos, cos], axis=-1)
    sin2 = jnp.concatenate([-sin, sin], axis=-1)
    return cd, cos2, sin2, decay, xi, zeta


def retention(z, gn_g, tables):
    b, s, _ = z.shape
    cd, cos2, sin2, decay, xi, zeta = tables
    L = decay.shape[1]
    qw = RET_H * RET_DK
    vw = RET_H * RET_DV
    return pl.pallas_call(
        _retention_kernel,
        grid=(b, s // L),
        in_specs=[pl.BlockSpec(memory_space=pltpu.SMEM),
                  pl.BlockSpec((1, L, qw), lambda bi, n: (bi, n, 0)),
                  pl.BlockSpec((1, L, qw), lambda bi, n: (bi, n, 1)),
                  pl.BlockSpec((1, L, vw), lambda bi, n: (bi, n, 1)),
                  pl.BlockSpec((1, L, vw), lambda bi, n: (bi, n, 2)),
                  pl.BlockSpec((L, RET_DK), lambda bi, n: (n, 0)),
                  pl.BlockSpec((L, RET_DK), lambda bi, n: (n, 0)),
                  pl.BlockSpec((RET_H, L, L), lambda bi, n: (0, 0, 0)),
                  pl.BlockSpec((RET_H, L, RET_DK), lambda bi, n: (0, 0, 0)),
                  pl.BlockSpec((RET_H, L, RET_DK), lambda bi, n: (0, 0, 0)),
                  pl.BlockSpec((1, vw), lambda bi, n: (0, 0))],
        out_specs=pl.BlockSpec((1, L, vw), lambda bi, n: (bi, n, 0)),
        out_shape=jax.ShapeDtypeStruct((b, s, vw), BF16),
        scratch_shapes=[pltpu.VMEM((RET_H, RET_DK, RET_DV), F32)],
        compiler_params=_params("parallel", "arbitrary"),
        name="retention",
    )(cd, z, z, z, z, cos2, sin2, decay, xi, zeta, gn_g)


def _fox_cumsum_kernel(f_ref, tri_ref, o_ref):
    s = f_ref.shape[1]
    tri = tri_ref[...]
    carry = jnp.zeros((1, LANES), F32)
    for i in range(s // CUM_BLK):
        f = f_ref[0, i * CUM_BLK:(i + 1) * CUM_BLK, :]
        ls = jnp.minimum(f, 0.0) - jnp.log1p(jnp.exp(-jnp.abs(f)))
        hi = ls.astype(BF16)
        r1 = ls - hi.astype(F32)
        mid = r1.astype(BF16)
        lo = (r1 - mid.astype(F32)).astype(BF16)
        c = (jnp.dot(tri, hi, preferred_element_type=F32)
             + jnp.dot(tri, mid, preferred_element_type=F32)
             + jnp.dot(tri, lo, preferred_element_type=F32)) + carry
        carry = c[CUM_BLK - 1:CUM_BLK, :]
        o_ref[0, :, i * CUM_BLK:(i + 1) * CUM_BLK] = (-c).T[:FOX_H, :]


def fox_cumsum(z, ff_block):
    b, s, _ = z.shape
    r = jnp.arange(CUM_BLK)
    tri = (r[:, None] >= r[None, :]).astype(BF16)
    return pl.pallas_call(
        _fox_cumsum_kernel,
        grid=(b,),
        in_specs=[pl.BlockSpec((1, s, LANES), lambda bi: (bi, 0, ff_block)),
                  pl.BlockSpec((CUM_BLK, CUM_BLK), lambda bi: (0, 0))],
        out_specs=pl.BlockSpec((1, FOX_H, s), lambda bi: (bi, 0, 0)),
        out_shape=jax.ShapeDtypeStruct((b, FOX_H, s), F32),
        compiler_params=_params("parallel"),
        name="fox_cumsum",
    )(z, tri)


def _fox_attn_kernel(q_ref, k_ref, v_ref, nc_ref, o_ref, kb_ref, vb_ref, qb_ref, m_ref, l_ref,
                     acc_ref):
    qi = pl.program_id(2)

    @pl.when(qi == 0)
    def _():
        kb_ref[...] = k_ref[0].astype(BF16)
        vb_ref[...] = v_ref[0].astype(BF16)

    qb_ref[...] = (q_ref[0] * (FOX_DH ** -0.5)).astype(BF16)
    m_ref[...] = jnp.full_like(m_ref, NEG_BIG)
    l_ref[...] = jnp.zeros_like(l_ref)
    acc_ref[...] = jnp.zeros_like(acc_ref)

    def tile(j, masked):
        k0 = pl.multiple_of(j * FOX_TK, FOX_TK)
        s = lax.dot_general(qb_ref[...], kb_ref[pl.ds(k0, FOX_TK), :],
                            (((1,), (1,)), ((), ())), preferred_element_type=F32)
        s = s + nc_ref[0, 0, :, pl.ds(k0, FOX_TK)]
        if masked:
            row = lax.broadcasted_iota(jnp.int32, s.shape, 0)
            col = lax.broadcasted_iota(jnp.int32, s.shape, 1)
            s = jnp.where(col <= row, s, NEG_BIG)
        m_prev = m_ref[...]
        m_new = jnp.maximum(m_prev, jnp.max(s, axis=-1, keepdims=True))
        a = jnp.exp(m_prev - m_new)
        p = jnp.exp(s - m_new)
        l_ref[...] = a * l_ref[...] + jnp.sum(p, axis=-1, keepdims=True)
        acc_ref[...] = a * acc_ref[...] + jnp.dot(
            p.astype(BF16), vb_ref[pl.ds(k0, FOX_TK), :], preferred_element_type=F32)
        m_ref[...] = m_new

    def body(j, c):
        tile(j, False)
        return c

    lax.fori_loop(0, qi, body, 0)
    tile(qi, True)
    o_ref[0] = (acc_ref[...] / l_ref[...]).astype(BF16)


def fox_attention(z, negc, q_blk, k_blk, v_blk):
    b, s, _ = z.shape
    assert FOX_TQ == FOX_TK
    negc4 = negc.reshape(b, FOX_H, 1, s)
    return pl.pallas_call(
        _fox_attn_kernel,
        grid=(b, FOX_H, s // FOX_TQ),
        in_specs=[pl.BlockSpec((1, FOX_TQ, FOX_DH), lambda bi, h, qi: (bi, qi, q_blk + h)),
                  pl.BlockSpec((1, s, FOX_DH), lambda bi, h, qi: (bi, 0, k_blk + h)),
                  pl.BlockSpec((1, s, FOX_DH), lambda bi, h, qi: (bi, 0, v_blk + h)),
                  pl.BlockSpec((1, 1, 1, s), lambda bi, h, qi: (bi, h, 0, 0))],
        out_specs=pl.BlockSpec((1, FOX_TQ, FOX_DH), lambda bi, h, qi: (bi, qi, h)),
        out_shape=jax.ShapeDtypeStruct((b, s, FOX_H * FOX_DH), BF16),
        scratch_shapes=[pltpu.VMEM((s, FOX_DH), BF16),
                        pltpu.VMEM((s, FOX_DH), BF16),
                        pltpu.VMEM((FOX_TQ, FOX_DH), BF16),
                        pltpu.VMEM((FOX_TQ, 1), F32),
                        pltpu.VMEM((FOX_TQ, 1), F32),
                        pltpu.VMEM((FOX_TQ, FOX_DH), F32)],
        compiler_params=_params("parallel", "parallel", "arbitrary"),
        name="fox_attn",
    )(z, z, z, negc4)


def _conv_kernel(a_ref, g_ref, ap_ref, gp_ref, w_ref, cb_ref, lg_ref, lb_ref, o_ref, y_ref,
                 c_ref):
    ts = a_ref.shape[1]
    c_all = a_ref.shape[2]
    prev = ap_ref[0] * _sigmoid(gp_ref[0])
    y_ref[0:CONV_HALO, :] = jnp.where(pl.program_id(1) > 0, prev, 0.0)
    y_ref[CONV_HALO:CONV_HALO + ts, :] = a_ref[0] * _sigmoid(g_ref[0])
    lead = CONV_HALO - (CONV_W - 1)

    n_cb = c_all // CONV_CB
    win_rows = CONV_RB + CONV_HALO

    def block(idx, carry):
        r0 = pl.multiple_of((idx // n_cb) * CONV_RB, CONV_RB)
        c0 = pl.multiple_of((idx % n_cb) * CONV_CB, CONV_CB)
        win = y_ref[pl.ds(r0, win_rows), pl.ds(c0, CONV_CB)]
        acc = jnp.zeros((CONV_RB, CONV_CB), F32)
        for ph in range(SUBLANES):
            rot = win if ph == 0 else pltpu.roll(win, win_rows - ph, 0)
            for w in range(CONV_W):
                if (w + lead) % SUBLANES == ph:
                    a0 = (w + lead) - ph
                    acc = acc + w_ref[w:w + 1, pl.ds(c0, CONV_CB)] * rot[a0:a0 + CONV_RB]
        c_ref[pl.ds(r0, CONV_RB), pl.ds(c0, CONV_CB)] = acc + cb_ref[:, pl.ds(c0, CONV_CB)]
        return carry

    lax.fori_loop(0, (ts // CONV_RB) * n_cb, block, 0)
    y = _layernorm_rows(c_ref[...], lg_ref[...], lb_ref[...])
    o_ref[0] = (y * _sigmoid(y)).astype(BF16)


def conformer_conv(z, a_blk, g_blk, conv_w, conv_b, ln_g, ln_b):
    b, s, _ = z.shape
    c = conv_w.shape[1]
    ts = min(CONV_TS, s)
    hb = ts // CONV_HALO

    def halo(col):
        return lambda bi, i: (bi, jnp.maximum(i * hb - 1, 0), col)

    return pl.pallas_call(
        _conv_kernel,
        grid=(b, s // ts),
        in_specs=[pl.BlockSpec((1, ts, c), lambda bi, i: (bi, i, a_blk)),
                  pl.BlockSpec((1, ts, c), lambda bi, i: (bi, i, g_blk)),
                  pl.BlockSpec((1, CONV_HALO, c), halo(a_blk)),
                  pl.BlockSpec((1, CONV_HALO, c), halo(g_blk)),
                  pl.BlockSpec((CONV_W, c), lambda bi, i: (0, 0)),
                  pl.BlockSpec((1, c), lambda bi, i: (0, 0)),
                  pl.BlockSpec((1, c), lambda bi, i: (0, 0)),
                  pl.BlockSpec((1, c), lambda bi, i: (0, 0))],
        out_specs=pl.BlockSpec((1, ts, c), lambda bi, i: (bi, i, 0)),
        out_shape=jax.ShapeDtypeStruct((b, s, c), BF16),
        scratch_shapes=[pltpu.VMEM((CONV_HALO + ts, c), F32),
                        pltpu.VMEM((ts, c), F32)],
        compiler_params=_params("parallel", "parallel"),
        name="conv",
    )(z, z, z, z, conv_w, conv_b, ln_g, ln_b)


def _merge_kernel(a_ref, f_ref, c_ref, wa_ref, wf_ref, wc_ref, g0_ref, g1_ref, g2_ref, o_ref):
    ya = jnp.dot(a_ref[...], wa_ref[...], preferred_element_type=F32)
    m = _sigmoid(g0_ref[...]) * ya
    yf = jnp.dot(f_ref[...], wf_ref[...], preferred_element_type=F32)
    m = m + _sigmoid(g1_ref[...]) * yf
    yc = jnp.dot(c_ref[...], wc_ref[...], preferred_element_type=F32)
    m = m + _sigmoid(g2_ref[...]) * yc
    o_ref[...] = m.astype(BF16)


def merge(ret_o, fox_o, conv_o, w_ret, w_fox, w_conv, z2, gl_blk, tm=512, tn=512):
    m, k = ret_o.shape
    n = w_ret.shape[1]
    tm = min(tm, m)
    nb = n // tn
    act = pl.BlockSpec((tm, k), lambda i, j: (i, 0))
    wsp = pl.BlockSpec((k, tn), lambda i, j: (0, j))

    def gate(br):
        return pl.BlockSpec((tm, tn), lambda i, j: (i, gl_blk * nb + br * nb + j))

    return pl.pallas_call(
        _merge_kernel,
        grid=(m // tm, nb),
        in_specs=[act, act, act, wsp, wsp, wsp, gate(0), gate(1), gate(2)],
        out_specs=pl.BlockSpec((tm, tn), lambda i, j: (i, j)),
        out_shape=jax.ShapeDtypeStruct((m, n), BF16),
        compiler_params=_params("parallel", "parallel"),
        name="merge",
    )(ret_o, fox_o, conv_o, w_ret, w_fox, w_conv, z2, z2, z2)


def _out_ln_kernel(x_ref, w_ref, h_ref, g_ref, b_ref, o_ref, ob_ref):
    y = jnp.dot(x_ref[...], w_ref[...], preferred_element_type=F32)
    out = _layernorm_rows(ALPHA * h_ref[...] + y, g_ref[...], b_ref[...])
    o_ref[...] = out
    ob_ref[...] = out.astype(BF16)


def out_ln(merged, w_out, h, g, b, tm=512):
    m, k = merged.shape
    n = w_out.shape[1]
    tm = min(tm, m)
    return pl.pallas_call(
        _out_ln_kernel,
        grid=(m // tm,),
        in_specs=[pl.BlockSpec((tm, k), lambda i: (i, 0)),
                  pl.BlockSpec((k, n), lambda i: (0, 0)),
                  pl.BlockSpec((tm, n), lambda i: (i, 0)),
                  pl.BlockSpec((1, n), lambda i: (0, 0)),
                  pl.BlockSpec((1, n), lambda i: (0, 0))],
        out_specs=[pl.BlockSpec((tm, n), lambda i: (i, 0)),
                   pl.BlockSpec((tm, n), lambda i: (i, 0))],
        out_shape=[jax.ShapeDtypeStruct((m, n), F32), jax.ShapeDtypeStruct((m, n), BF16)],
        compiler_params=_params("parallel"),
        name="out_ln",
    )(merged, w_out, h, g, b)


ROUTE_COLS = 8


def _router_kernel(h_ref, w_ref, b_ref, tri_ref, idx_ref, wt_ref, rank_ref, cnt_ref, carry_ref):
    @pl.when(pl.program_id(0) == 0)
    def _():
        carry_ref[...] = jnp.zeros_like(carry_ref)

    h = h_ref[...]
    hh = h.astype(BF16)
    hl = (h - hh.astype(F32)).astype(BF16)
    w = w_ref[...]
    wh = w.astype(BF16)
    wl = (w - wh.astype(F32)).astype(BF16)
    logits = (jnp.dot(hh, wh, preferred_element_type=F32)
              + jnp.dot(hl, wh, preferred_element_type=F32)
              + jnp.dot(hh, wl, preferred_element_type=F32))
    scores = _sigmoid(logits)
    tm, ne = scores.shape
    sel = scores + b_ref[...]
    lane = lax.broadcasted_iota(jnp.int32, (tm, ne), 1).astype(F32)
    mask = jnp.zeros((tm, ne), F32)
    onehots, idxs, wts = [], [], []
    for _ in range(TOP_K):
        mx = jnp.max(sel, axis=-1, keepdims=True)
        ik = jnp.min(jnp.where(sel == mx, lane, float(ne)), axis=-1, keepdims=True)
        oh = lane == ik
        wts.append(jnp.sum(jnp.where(oh, scores, 0.0), axis=-1, keepdims=True))
        idxs.append(ik)
        onehots.append(oh)
        sel = jnp.where(oh, -jnp.inf, sel)
        mask = jnp.where(oh, 1.0, mask)
    wsum = wts[0]
    for t in wts[1:]:
        wsum = wsum + t
    cnt = jnp.dot(tri_ref[...], mask.astype(BF16), preferred_element_type=F32) + carry_ref[...]
    col = lax.broadcasted_iota(jnp.int32, (tm, ROUTE_COLS), 1)
    idx_o = jnp.zeros((tm, ROUTE_COLS), F32)
    wt_o = jnp.zeros((tm, ROUTE_COLS), F32)
    rank_o = jnp.zeros((tm, ROUTE_COLS), F32)
    for kk in range(TOP_K):
        rk = jnp.sum(jnp.where(onehots[kk], cnt, 0.0), axis=-1, keepdims=True)
        idx_o = jnp.where(col == kk, idxs[kk], idx_o)
        wt_o = jnp.where(col == kk, wts[kk] / wsum * ROUTED_SCALE, wt_o)
        rank_o = jnp.where(col == kk, rk, rank_o)
    idx_ref[...] = idx_o.astype(jnp.int32)
    wt_ref[...] = wt_o
    rank_ref[...] = rank_o.astype(jnp.int32)
    carry_ref[...] = carry_ref[...] + jnp.sum(mask, axis=0, keepdims=True)
    cnt_ref[...] = carry_ref[...]


def router(h, w_router, b_router, tm=512):
    m, k = h.shape
    ne = w_router.shape[1]
    tm = min(tm, m)
    r = jnp.arange(tm)
    tri = (r[:, None] > r[None, :]).astype(BF16)
    small = pl.BlockSpec((tm, ROUTE_COLS), lambda i: (i, 0))
    return pl.pallas_call(
        _router_kernel,
        grid=(m // tm,),
        in_specs=[pl.BlockSpec((tm, k), lambda i: (i, 0)),
                  pl.BlockSpec((k, ne), lambda i: (0, 0)),
                  pl.BlockSpec((1, ne), lambda i: (0, 0)),
                  pl.BlockSpec((tm, tm), lambda i: (0, 0))],
        out_specs=[small, small, small, pl.BlockSpec((1, ne), lambda i: (0, 0))],
        out_shape=[jax.ShapeDtypeStruct((m, ROUTE_COLS), jnp.int32),
                   jax.ShapeDtypeStruct((m, ROUTE_COLS), F32),
                   jax.ShapeDtypeStruct((m, ROUTE_COLS), jnp.int32),
                   jax.ShapeDtypeStruct((1, ne), F32)],
        scratch_shapes=[pltpu.VMEM((1, ne), F32)],
        compiler_params=_params("arbitrary"),
        name="router",
    )(h, w_router, b_router, tri)


def _experts_kernel(be_ref, bv_ref, x_ref, wgu_ref, wd_ref, o_ref):
    i = pl.program_id(0)
    ff = wd_ref.shape[1]

    @pl.when(bv_ref[i] > 0)
    def _():
        gu = jnp.dot(x_ref[...], wgu_ref[0], preferred_element_type=F32)
        g = gu[:, :ff]
        a = (g * _sigmoid(g) * gu[:, ff:]).astype(BF16)
        o_ref[...] = jnp.dot(a, wd_ref[0], preferred_element_type=F32).astype(BF16)

    @pl.when(bv_ref[i] == 0)
    def _():
        o_ref[...] = jnp.zeros_like(o_ref)


def experts(xs, block_e, block_valid, w_gu, w_down):
    r, d = xs.shape
    ff2 = w_gu.shape[2]
    nblk = r // MOE_BLOCK
    return pl.pallas_call(
        _experts_kernel,
        grid_spec=pltpu.PrefetchScalarGridSpec(
            num_scalar_prefetch=2,
            grid=(nblk,),
            in_specs=[pl.BlockSpec((MOE_BLOCK, d), lambda i, be, bv: (i, 0)),
                      pl.BlockSpec((1, d, ff2), lambda i, be, bv: (be[i], 0, 0)),
                      pl.BlockSpec((1, ff2 // 2, d), lambda i, be, bv: (be[i], 0, 0))],
            out_specs=pl.BlockSpec((MOE_BLOCK, d), lambda i, be, bv: (i, 0))),
        out_shape=jax.ShapeDtypeStruct((r, d), BF16),
        compiler_params=_params("arbitrary"),
        name="experts",
    )(block_e, block_valid, xs, w_gu, w_down)


def _shared_ln_kernel(xb_ref, h_ref, r_ref, wgu_ref, wd_ref, g_ref, b_ref, o_ref, ob_ref):
    ff = wd_ref.shape[0]
    gu = jnp.dot(xb_ref[...], wgu_ref[...], preferred_element_type=F32)
    g = gu[:, :ff]
    a = (g * _sigmoid(g) * gu[:, ff:]).astype(BF16)
    shared = jnp.dot(a, wd_ref[...], preferred_element_type=F32)
    out = _layernorm_rows(ALPHA * h_ref[...] + (r_ref[...] + shared), g_ref[...], b_ref[...])
    o_ref[...] = out
    ob_ref[...] = out.astype(BF16)


def shared_ln(h_bf, h, routed, w_gu, w_down, g, b, tm=512):
    m, d = h.shape
    ff2 = w_gu.shape[1]
    tm = min(tm, m)
    row = pl.BlockSpec((tm, d), lambda i: (i, 0))
    vec = pl.BlockSpec((1, d), lambda i: (0, 0))
    return pl.pallas_call(
        _shared_ln_kernel,
        grid=(m // tm,),
        in_specs=[row, row, row,
                  pl.BlockSpec((d, ff2), lambda i: (0, 0)),
                  pl.BlockSpec((ff2 // 2, d), lambda i: (0, 0)),
                  vec, vec],
        out_specs=[row, row],
        out_shape=[jax.ShapeDtypeStruct((m, d), F32), jax.ShapeDtypeStruct((m, d), BF16)],
        compiler_params=_params("parallel"),
        name="shared_ln",
    )(h_bf, h, routed, w_gu, w_down, g, b)


def _mixer_layout(d):
    rq = RET_H * RET_DK
    rv = RET_H * RET_DV
    fq = FOX_H * FOX_DH
    main = 2 * rq + 2 * rv + 3 * fq
    ff0 = main
    cu0 = ff0 + FOX_H
    gl0 = cu0 + 2 * d
    end = gl0 + N_BRANCH * d
    return dict(rq=rq, rv=rv, fq=fq, main=main, ff0=ff0, cu0=cu0, gl0=gl0, end=end)


def _prep_in_weights(w_in, b_in, d, tn):
    lay = _mixer_layout(d)
    body = lay["main"] + 2 * d + N_BRANCH * d
    pad = tn - FOX_H
    w = jnp.concatenate([w_in[:, :lay["main"]], w_in[:, lay["cu0"]:lay["end"]],
                         w_in[:, lay["ff0"]:lay["cu0"]],
                         jnp.zeros((w_in.shape[0], pad), w_in.dtype)], axis=1).astype(BF16)
    bb = jnp.concatenate([b_in[:lay["main"]], b_in[lay["cu0"]:lay["end"]],
                          b_in[lay["ff0"]:lay["cu0"]], jnp.zeros((pad,), b_in.dtype)])
    return w, bb.reshape(1, -1).astype(F32), body


def _moe_sublayer(h, h_bf, w_router, b_router, w_gu, w_down, w_sh_gu, w_sh_down, g, b):
    t, d = h.shape
    ne = w_router.shape[1]
    idx8, wt8, rank8, cnt = router(h, w_router, b_router.reshape(1, ne))
    idx = idx8[:, :TOP_K]
    wt = wt8[:, :TOP_K]
    rank = rank8[:, :TOP_K]
    sizes = cnt[0].astype(jnp.int32)
    a = t * TOP_K
    n_blocks = -(-a // MOE_BLOCK) + ne
    r = n_blocks * MOE_BLOCK
    padded = (sizes + MOE_BLOCK - 1) // MOE_BLOCK * MOE_BLOCK
    pad_end = jnp.cumsum(padded)
    pad_start = pad_end - padded
    dest = pad_start[idx] + rank
    tok = jnp.broadcast_to(jnp.arange(t, dtype=jnp.int32)[:, None], (t, TOP_K))
    row_tok = jnp.full((r,), t, jnp.int32).at[dest.reshape(a)].set(tok.reshape(a))
    blk0 = jnp.arange(n_blocks, dtype=jnp.int32) * MOE_BLOCK
    block_e = jnp.minimum(jnp.searchsorted(pad_end, blk0, side='right'), ne - 1).astype(jnp.int32)
    block_valid = (blk0 < pad_end[-1]).astype(jnp.int32)
    h_pad = jnp.concatenate([h_bf, jnp.zeros((1, d), BF16)], axis=0)
    xs = h_pad[row_tok]
    ys = experts(xs, block_e, block_valid, w_gu, w_down)
    routed = jnp.sum(ys[dest].astype(F32) * wt[:, :, None], axis=1)
    return shared_ln(h_bf, h, routed, w_sh_gu, w_sh_down, g, b)


def kernel(x, w_in, b_in, ret_gn_g, conv_w, conv_b, conv_ln_g, conv_ln_b, w_ret_o, w_fox_o,
           w_conv_o, w_out, ln1_g, ln1_b, w_router, b_router, w_exp_gate, w_exp_up, w_exp_down,
           w_sh_gate, w_sh_up, w_sh_down, ln2_g, ln2_b):
    bsz, seq, d = x.shape
    t = bsz * seq
    in_tn = 512
    lay = _mixer_layout(d)
    chunk = min(RET_CHUNK, seq)
    tables = _retention_tables(seq, chunk)
    fq0 = 2 * lay["rq"] + 2 * lay["rv"]

    def row(v):
        return v.reshape(1, -1)

    def layer(carry, p):
        h, h_bf = carry
        w_cat, b_cat, body = _prep_in_weights(p["w_in"], p["b_in"], d, in_tn)
        z2 = in_proj(h_bf, w_cat, b_cat, tn=in_tn)
        z = z2.reshape(bsz, seq, -1)
        ret_o = retention(z, row(p["ret_gn_g"]), tables)
        negc = fox_cumsum(z, body // LANES)
        fox_o = fox_attention(z, negc, fq0 // FOX_DH, (fq0 + lay["fq"]) // FOX_DH,
                              (fq0 + 2 * lay["fq"]) // FOX_DH)
        conv_o = conformer_conv(z, lay["main"] // d, lay["main"] // d + 1, p["conv_w"],
                                row(p["conv_b"]), row(p["conv_ln_g"]), row(p["conv_ln_b"]))
        merged = merge(ret_o.reshape(t, -1), fox_o.reshape(t, -1), conv_o.reshape(t, -1),
                       p["w_ret_o"].astype(BF16), p["w_fox_o"].astype(BF16),
                       p["w_conv_o"].astype(BF16), z2, (lay["main"] + 2 * d) // d)
        h, h_bf = out_ln(merged, p["w_out"].astype(BF16), h, row(p["ln1_g"]), row(p["ln1_b"]))
        w_gu = jnp.concatenate([p["w_exp_gate"], p["w_exp_up"]], axis=-1).astype(BF16)
        w_sh_gu = jnp.concatenate([p["w_sh_gate"], p["w_sh_up"]], axis=-1).astype(BF16)
        h, h_bf = _moe_sublayer(h, h_bf, p["w_router"], p["b_router"], w_gu,
                                p["w_exp_down"].astype(BF16), w_sh_gu,
                                p["w_sh_down"].astype(BF16), row(p["ln2_g"]), row(p["ln2_b"]))
        return (h, h_bf), None

    params = dict(w_in=w_in, b_in=b_in, ret_gn_g=ret_gn_g, conv_w=conv_w, conv_b=conv_b,
                  conv_ln_g=conv_ln_g, conv_ln_b=conv_ln_b, w_ret_o=w_ret_o, w_fox_o=w_fox_o,
                  w_conv_o=w_conv_o, w_out=w_out, ln1_g=ln1_g, ln1_b=ln1_b, w_router=w_router,
                  b_router=b_router, w_exp_gate=w_exp_gate, w_exp_up=w_exp_up,
                  w_exp_down=w_exp_down, w_sh_gate=w_sh_gate, w_sh_up=w_sh_up,
                  w_sh_down=w_sh_down, ln2_g=ln2_g, ln2_b=ln2_b)
    h0 = x.reshape(t, d)
    (h, _), _ = lax.scan(layer, (h0, h0.astype(BF16)), params)
    return h.reshape(bsz, seq, d)
```

```python
import functools

import jax
import jax.numpy as jnp
from jax import lax
from jax.experimental import pallas as pl
from jax.experimental.pallas import tpu as pltpu

F32 = jnp.float32
BF16 = jnp.bfloat16

RET_H = 8
RET_DK = 128
RET_DV = 256
ROPE_BASE = 10000.0
FOX_H = 16
FOX_DH = 128
CONV_W = 31
N_BRANCH = 3
TOP_K = 6
ROUTED_SCALE = 2.5
DEPTH_FOR_NORM = 4
ALPHA = (2 * DEPTH_FOR_NORM) ** 0.25
LN_EPS = 1e-5

LANES = 128
SUBLANES = 8
VMEM_LIMIT = 56 * 1024 * 1024

RET_CHUNK = 256
FOX_TQ = 512
FOX_TK = 512
CUM_BLK = 256
CONV_TS = 256
CONV_HALO = 32
CONV_RB = 64
CONV_CB = 256
MOE_BLOCK = 512
NEG_BIG = -1e30
LOG2E = 1.4426950408889634


def _params(*sem):
    return pltpu.CompilerParams(dimension_semantics=sem, vmem_limit_bytes=VMEM_LIMIT)


def _sigmoid(x):
    return 1.0 / (1.0 + jnp.exp(-x))


def _pack_halves(x):
    c = x.shape[1] // 2
    xb = x.astype(BF16).astype(F32)
    lo = lax.bitcast_convert_type(xb[:, :c], jnp.uint32) >> 16
    hi = lax.bitcast_convert_type(xb[:, c:], jnp.uint32) & jnp.uint32(0xFFFF0000)
    return lo | hi


def _unpack_halves(u):
    lo = lax.bitcast_convert_type(u << 16, F32)
    hi = lax.bitcast_convert_type(u & jnp.uint32(0xFFFF0000), F32)
    return lo, hi


def _layernorm_rows(x, g, b):
    mu = jnp.mean(x, axis=-1, keepdims=True)
    xc = x - mu
    var = jnp.mean(xc * xc, axis=-1, keepdims=True)
    return xc * lax.rsqrt(var + LN_EPS) * g + b


def _in_proj_kernel(x_ref, w_ref, b_ref, o_ref):
    acc = jnp.dot(x_ref[...], w_ref[...], preferred_element_type=F32)
    o_ref[...] = acc + b_ref[...]


def in_proj(x_bf, w_bf, b, tm=1024, tn=512):
    m, k = x_bf.shape
    n = w_bf.shape[1]
    tm = min(tm, m)
    return pl.pallas_call(
        _in_proj_kernel,
        grid=(m // tm, n // tn),
        in_specs=[pl.BlockSpec((tm, k), lambda i, j: (i, 0)),
                  pl.BlockSpec((k, tn), lambda i, j: (0, j)),
                  pl.BlockSpec((1, tn), lambda i, j: (0, j))],
        out_specs=pl.BlockSpec((tm, tn), lambda i, j: (i, j)),
        out_shape=jax.ShapeDtypeStruct((m, n), F32),
        compiler_params=_params("parallel", "parallel"),
        name="in_proj",
    )(x_bf, w_bf, b)


def _retention_kernel(cd_ref, q_ref, k_ref, v_ref, g_ref, cos_ref, sin_ref, decay_ref,
                      xi_ref, zeta_ref, gn_ref, o_ref, state_ref):
    @pl.when(pl.program_id(1) == 0)
    def _():
        state_ref[...] = jnp.zeros_like(state_ref)

    cos = cos_ref[...]
    sin = sin_ref[...]
    for h in range(RET_H):
        qs = slice(h * RET_DK, (h + 1) * RET_DK)
        vs = slice(h * RET_DV, (h + 1) * RET_DV)
        q = q_ref[0, :, qs]
        k = k_ref[0, :, qs]
        qr = q * cos + pltpu.roll(q, RET_DK // 2, 1) * sin
        kr = (k * cos + pltpu.roll(k, RET_DK // 2, 1) * sin) * (RET_DK ** -0.5)
        vb = v_ref[0, :, vs].astype(BF16)
        s = lax.dot_general(qr.astype(BF16), kr.astype(BF16), (((1,), (1,)), ((), ())),
                            preferred_element_type=F32)
        inner = (s * decay_ref[h]).astype(BF16)
        o = jnp.dot(inner, vb, preferred_element_type=F32)
        st = state_ref[h]
        o = o + jnp.dot((qr * xi_ref[h]).astype(BF16), st.astype(BF16),
                        preferred_element_type=F32)
        kz = (kr * zeta_ref[h]).astype(BF16)
        state_ref[h] = st * cd_ref[h] + lax.dot_general(
            kz, vb, (((0,), (0,)), ((), ())), preferred_element_type=F32)
        mu = jnp.mean(o, axis=-1, keepdims=True)
        oc = o - mu
        var = jnp.mean(oc * oc, axis=-1, keepdims=True)
        on = oc * lax.rsqrt(var + LN_EPS) * gn_ref[:, vs]
        g = g_ref[0, :, vs]
        o_ref[0, :, vs] = (g * _sigmoid(g) * on).astype(BF16)


def _retention_tables(seq, chunk):
    h = jnp.arange(RET_H, dtype=F32)
    log_g = jnp.log1p(-jnp.exp2(-5.0 - h))
    i = jnp.arange(chunk, dtype=F32)
    diff = i[:, None] - i[None, :]
    decay = jnp.where(diff >= 0, jnp.exp(log_g[:, None, None] * jnp.maximum(diff, 0.0)), 0.0)
    xi = jnp.exp(log_g[:, None] * (i + 1.0))
    zeta = jnp.exp(log_g[:, None] * (chunk - 1.0 - i))
    xi = jnp.broadcast_to(xi[:, :, None], (RET_H, chunk, RET_DK))
    zeta = jnp.broadcast_to(zeta[:, :, None], (RET_H, chunk, RET_DK))
    cd = jnp.exp(log_g * chunk)
    half = RET_DK // 2
    inv = 1.0 / (ROPE_BASE ** (jnp.arange(half, dtype=F32) / half))
    ang = jnp.arange(seq, dtype=F32)[:, None] * inv[None, :]
    cos = jnp.cos(ang)
    sin = jnp.sin(ang)
    cos2 = jnp.concatenate([cos, cos], axis=-1)
    sin2 = jnp.concatenate([-sin, sin], axis=-1)
    return cd, cos2, sin2, decay, xi, zeta


def retention(z, gn_g, tables):
    b, s, _ = z.shape
    cd, cos2, sin2, decay, xi, zeta = tables
    L = decay.shape[1]
    qw = RET_H * RET_DK
    vw = RET_H * RET_DV
    return pl.pallas_call(
        _retention_kernel,
        grid=(b, s // L),
        in_specs=[pl.BlockSpec(memory_space=pltpu.SMEM),
                  pl.BlockSpec((1, L, qw), lambda bi, n: (bi, n, 0)),
                  pl.BlockSpec((1, L, qw), lambda bi, n: (bi, n, 1)),
                  pl.BlockSpec((1, L, vw), lambda bi, n: (bi, n, 1)),
                  pl.BlockSpec((1, L, vw), lambda bi, n: (bi, n, 2)),
                  pl.BlockSpec((L, RET_DK), lambda bi, n: (n, 0)),
                  pl.BlockSpec((L, RET_DK), lambda bi, n: (n, 0)),
                  pl.BlockSpec((RET_H, L, L), lambda bi, n: (0, 0, 0)),
                  pl.BlockSpec((RET_H, L, RET_DK), lambda bi, n: (0, 0, 0)),
                  pl.BlockSpec((RET_H, L, RET_DK), lambda bi, n: (0, 0, 0)),
                  pl.BlockSpec((1, vw), lambda bi, n: (0, 0))],
        out_specs=pl.BlockSpec((1, L, vw), lambda bi, n: (bi, n, 0)),
        out_shape=jax.ShapeDtypeStruct((b, s, vw), BF16),
        scratch_shapes=[pltpu.VMEM((RET_H, RET_DK, RET_DV), F32)],
        compiler_params=_params("parallel", "arbitrary"),
        name="retention",
    )(cd, z, z, z, z, cos2, sin2, decay, xi, zeta, gn_g)


def _fox_cumsum_kernel(f_ref, tri_ref, o_ref):
    s = f_ref.shape[1]
    tri = tri_ref[...]
    carry = jnp.zeros((1, LANES), F32)
    for i in range(s // CUM_BLK):
        f = f_ref[0, i * CUM_BLK:(i + 1) * CUM_BLK, :]
        ls = jnp.minimum(f, 0.0) - jnp.log1p(jnp.exp(-jnp.abs(f)))
        hi = ls.astype(BF16)
        r1 = ls - hi.astype(F32)
        mid = r1.astype(BF16)
        lo = (r1 - mid.astype(F32)).astype(BF16)
        c = (jnp.dot(tri, hi, preferred_element_type=F32)
             + jnp.dot(tri, mid, preferred_element_type=F32)
             + jnp.dot(tri, lo, preferred_element_type=F32)) + carry
        carry = c[CUM_BLK - 1:CUM_BLK, :]
        o_ref[0, :, i * CUM_BLK:(i + 1) * CUM_BLK] = (-c).T[:FOX_H, :]


def fox_cumsum(z, ff_block):
    b, s, _ = z.shape
    r = jnp.arange(CUM_BLK)
    tri = (r[:, None] >= r[None, :]).astype(BF16)
    return pl.pallas_call(
        _fox_cumsum_kernel,
        grid=(b,),
        in_specs=[pl.BlockSpec((1, s, LANES), lambda bi: (bi, 0, ff_block)),
                  pl.BlockSpec((CUM_BLK, CUM_BLK), lambda bi: (0, 0))],
        out_specs=pl.BlockSpec((1, FOX_H, s), lambda bi: (bi, 0, 0)),
        out_shape=jax.ShapeDtypeStruct((b, FOX_H, s), F32),
        compiler_params=_params("parallel"),
        name="fox_cumsum",
    )(z, tri)


def _fox_attn_kernel(q_ref, k_ref, v_ref, nc_ref, o_ref, kt_ref, vx_ref, qb_ref, m_ref, acc_ref):
    qi = pl.program_id(2)
    s_len = k_ref.shape[1]

    @pl.when(qi == 0)
    def _():
        for c in range(s_len // FOX_TK):
            rows = slice(c * FOX_TK, (c + 1) * FOX_TK)
            kt_ref[:, rows] = k_ref[0, rows, :].T.astype(BF16)
            vx_ref[rows, :FOX_DH] = v_ref[0, rows, :].astype(BF16)
        vx_ref[:, FOX_DH:] = jnp.ones((s_len, FOX_DH), BF16)

    qb_ref[...] = (q_ref[0] * (FOX_DH ** -0.5 * LOG2E)).astype(BF16)
    m_ref[...] = jnp.full_like(m_ref, NEG_BIG)
    acc_ref[...] = jnp.zeros_like(acc_ref)

    def tile(j, masked):
        k0 = pl.multiple_of(j * FOX_TK, FOX_TK)
        s = jnp.dot(qb_ref[...], kt_ref[:, pl.ds(k0, FOX_TK)], preferred_element_type=F32)
        s = s + nc_ref[0, 0, :, pl.ds(k0, FOX_TK)] * LOG2E
        if masked:
            row = lax.broadcasted_iota(jnp.int32, s.shape, 0)
            col = lax.broadcasted_iota(jnp.int32, s.shape, 1)
            s = jnp.where(col <= row, s, NEG_BIG)
        m_prev = m_ref[...]
        m_new = jnp.maximum(m_prev, jnp.max(s, axis=-1, keepdims=True))
        alpha = jnp.exp2(m_prev - m_new)
        p = jnp.concatenate(
            [jnp.exp2(s[:, c * LANES:(c + 1) * LANES] - m_new) for c in range(FOX_TK // LANES)],
            axis=1).astype(BF16)
        pv = jnp.dot(p, vx_ref[pl.ds(k0, FOX_TK), :], preferred_element_type=F32)
        acc_ref[...] = jnp.concatenate([alpha, alpha], axis=1) * acc_ref[...] + pv
        m_ref[...] = m_new

    def body(j, c):
        tile(j, False)
        return c

    lax.fori_loop(0, qi, body, 0)
    tile(qi, True)
    o_ref[0] = (acc_ref[:, :FOX_DH] / acc_ref[:, FOX_DH:]).astype(BF16)


def fox_attention(z, negc, q_blk, k_blk, v_blk):
    b, s, _ = z.shape
    assert FOX_TQ == FOX_TK and FOX_DH == LANES
    negc4 = negc.reshape(b, FOX_H, 1, s)
    return pl.pallas_call(
        _fox_attn_kernel,
        grid=(b, FOX_H, s // FOX_TQ),
        in_specs=[pl.BlockSpec((1, FOX_TQ, FOX_DH), lambda bi, h, qi: (bi, qi, q_blk + h)),
                  pl.BlockSpec((1, s, FOX_DH), lambda bi, h, qi: (bi, 0, k_blk + h)),
                  pl.BlockSpec((1, s, FOX_DH), lambda bi, h, qi: (bi, 0, v_blk + h)),
                  pl.BlockSpec((1, 1, 1, s), lambda bi, h, qi: (bi, h, 0, 0))],
        out_specs=pl.BlockSpec((1, FOX_TQ, FOX_DH), lambda bi, h, qi: (bi, qi, h)),
        out_shape=jax.ShapeDtypeStruct((b, s, FOX_H * FOX_DH), BF16),
        scratch_shapes=[pltpu.VMEM((FOX_DH, s), BF16),
                        pltpu.VMEM((s, 2 * FOX_DH), BF16),
                        pltpu.VMEM((FOX_TQ, FOX_DH), BF16),
                        pltpu.VMEM((FOX_TQ, LANES), F32),
                        pltpu.VMEM((FOX_TQ, 2 * FOX_DH), F32)],
        compiler_params=_params("parallel", "parallel", "arbitrary"),
        name="fox_attn",
    )(z, z, z, negc4)


def _conv_kernel(a_ref, g_ref, ap_ref, gp_ref, w_ref, cb_ref, lg_ref, lb_ref, o_ref, y_ref,
                 c_ref):
    ts = a_ref.shape[1]
    c_all = a_ref.shape[2]
    prev = ap_ref[0] * _sigmoid(gp_ref[0])
    y_ref[0:CONV_HALO, :] = jnp.where(pl.program_id(1) > 0, prev, 0.0)
    y_ref[CONV_HALO:CONV_HALO + ts, :] = a_ref[0] * _sigmoid(g_ref[0])
    lead = CONV_HALO - (CONV_W - 1)

    n_cb = c_all // CONV_CB
    win_rows = CONV_RB + CONV_HALO

    def block(idx, carry):
        r0 = pl.multiple_of((idx // n_cb) * CONV_RB, CONV_RB)
        c0 = pl.multiple_of((idx % n_cb) * CONV_CB, CONV_CB)
        win = y_ref[pl.ds(r0, win_rows), pl.ds(c0, CONV_CB)]
        acc = jnp.zeros((CONV_RB, CONV_CB), F32)
        for ph in range(SUBLANES):
            rot = win if ph == 0 else pltpu.roll(win, win_rows - ph, 0)
            for w in range(CONV_W):
                if (w + lead) % SUBLANES == ph:
                    a0 = (w + lead) - ph
                    acc = acc + w_ref[w:w + 1, pl.ds(c0, CONV_CB)] * rot[a0:a0 + CONV_RB]
        c_ref[pl.ds(r0, CONV_RB), pl.ds(c0, CONV_CB)] = acc + cb_ref[:, pl.ds(c0, CONV_CB)]
        return carry

    lax.fori_loop(0, (ts // CONV_RB) * n_cb, block, 0)
    y = _layernorm_rows(c_ref[...], lg_ref[...], lb_ref[...])
    o_ref[0] = (y * _sigmoid(y)).astype(BF16)


def conformer_conv(z, a_blk, g_blk, conv_w, conv_b, ln_g, ln_b):
    b, s, _ = z.shape
    c = conv_w.shape[1]
    ts = min(CONV_TS, s)
    hb = ts // CONV_HALO

    def halo(col):
        return lambda bi, i: (bi, jnp.maximum(i * hb - 1, 0), col)

    return pl.pallas_call(
        _conv_kernel,
        grid=(b, s // ts),
        in_specs=[pl.BlockSpec((1, ts, c), lambda bi, i: (bi, i, a_blk)),
                  pl.BlockSpec((1, ts, c), lambda bi, i: (bi, i, g_blk)),
                  pl.BlockSpec((1, CONV_HALO, c), halo(a_blk)),
                  pl.BlockSpec((1, CONV_HALO, c), halo(g_blk)),
                  pl.BlockSpec((CONV_W, c), lambda bi, i: (0, 0)),
                  pl.BlockSpec((1, c), lambda bi, i: (0, 0)),
                  pl.BlockSpec((1, c), lambda bi, i: (0, 0)),
                  pl.BlockSpec((1, c), lambda bi, i: (0, 0))],
        out_specs=pl.BlockSpec((1, ts, c), lambda bi, i: (bi, i, 0)),
        out_shape=jax.ShapeDtypeStruct((b, s, c), BF16),
        scratch_shapes=[pltpu.VMEM((CONV_HALO + ts, c), F32),
                        pltpu.VMEM((ts, c), F32)],
        compiler_params=_params("parallel", "parallel"),
        name="conv",
    )(z, z, z, z, conv_w, conv_b, ln_g, ln_b)


def _merge_kernel(a_ref, f_ref, c_ref, wa_ref, wf_ref, wc_ref, g0_ref, g1_ref, g2_ref, o_ref):
    ya = jnp.dot(a_ref[...], wa_ref[...], preferred_element_type=F32)
    m = _sigmoid(g0_ref[...]) * ya
    yf = jnp.dot(f_ref[...], wf_ref[...], preferred_element_type=F32)
    m = m + _sigmoid(g1_ref[...]) * yf
    yc = jnp.dot(c_ref[...], wc_ref[...], preferred_element_type=F32)
    m = m + _sigmoid(g2_ref[...]) * yc
    o_ref[...] = m.astype(BF16)


def merge(ret_o, fox_o, conv_o, w_ret, w_fox, w_conv, z2, gl_blk, tm=512, tn=512):
    m, k = ret_o.shape
    n = w_ret.shape[1]
    tm = min(tm, m)
    nb = n // tn
    act = pl.BlockSpec((tm, k), lambda i, j: (i, 0))
    wsp = pl.BlockSpec((k, tn), lambda i, j: (0, j))

    def gate(br):
        return pl.BlockSpec((tm, tn), lambda i, j: (i, gl_blk * nb + br * nb + j))

    return pl.pallas_call(
        _merge_kernel,
        grid=(m // tm, nb),
        in_specs=[act, act, act, wsp, wsp, wsp, gate(0), gate(1), gate(2)],
        out_specs=pl.BlockSpec((tm, tn), lambda i, j: (i, j)),
        out_shape=jax.ShapeDtypeStruct((m, n), BF16),
        compiler_params=_params("parallel", "parallel"),
        name="merge",
    )(ret_o, fox_o, conv_o, w_ret, w_fox, w_conv, z2, z2, z2)


def _out_ln_kernel(x_ref, w_ref, h_ref, g_ref, b_ref, o_ref, ob_ref, op_ref):
    y = jnp.dot(x_ref[...], w_ref[...], preferred_element_type=F32)
    out = _layernorm_rows(ALPHA * h_ref[...] + y, g_ref[...], b_ref[...])
    o_ref[...] = out
    ob_ref[...] = out.astype(BF16)
    op_ref[...] = _pack_halves(out)


def out_ln(merged, w_out, h, g, b, tm=512):
    m, k = merged.shape
    n = w_out.shape[1]
    tm = min(tm, m)
    return pl.pallas_call(
        _out_ln_kernel,
        grid=(m // tm,),
        in_specs=[pl.BlockSpec((tm, k), lambda i: (i, 0)),
                  pl.BlockSpec((k, n), lambda i: (0, 0)),
                  pl.BlockSpec((tm, n), lambda i: (i, 0)),
                  pl.BlockSpec((1, n), lambda i: (0, 0)),
                  pl.BlockSpec((1, n), lambda i: (0, 0))],
        out_specs=[pl.BlockSpec((tm, n), lambda i: (i, 0)),
                   pl.BlockSpec((tm, n), lambda i: (i, 0)),
                   pl.BlockSpec((tm, n // 2), lambda i: (i, 0))],
        out_shape=[jax.ShapeDtypeStruct((m, n), F32), jax.ShapeDtypeStruct((m, n), BF16),
                   jax.ShapeDtypeStruct((m, n // 2), jnp.uint32)],
        compiler_params=_params("parallel"),
        name="out_ln",
    )(merged, w_out, h, g, b)


ROUTE_COLS = 8


def _router_kernel(h_ref, w_ref, b_ref, tri_ref, idx_ref, wt_ref, rank_ref, cnt_ref, carry_ref):
    @pl.when(pl.program_id(0) == 0)
    def _():
        carry_ref[...] = jnp.zeros_like(carry_ref)

    h = h_ref[...]
    hh = h.astype(BF16)
    hl = (h - hh.astype(F32)).astype(BF16)
    w = w_ref[...]
    wh = w.astype(BF16)
    wl = (w - wh.astype(F32)).astype(BF16)
    logits = (jnp.dot(hh, wh, preferred_element_type=F32)
              + jnp.dot(hl, wh, preferred_element_type=F32)
              + jnp.dot(hh, wl, preferred_element_type=F32))
    scores = _sigmoid(logits)
    tm, ne = scores.shape
    sel = scores + b_ref[...]
    lane = lax.broadcasted_iota(jnp.int32, (tm, ne), 1).astype(F32)
    mask = jnp.zeros((tm, ne), F32)
    onehots, idxs, wts = [], [], []
    for _ in range(TOP_K):
        mx = jnp.max(sel, axis=-1, keepdims=True)
        ik = jnp.min(jnp.where(sel == mx, lane, float(ne)), axis=-1, keepdims=True)
        oh = lane == ik
        wts.append(jnp.sum(jnp.where(oh, scores, 0.0), axis=-1, keepdims=True))
        idxs.append(ik)
        onehots.append(oh)
        sel = jnp.where(oh, -jnp.inf, sel)
        mask = jnp.where(oh, 1.0, mask)
    wsum = wts[0]
    for t in wts[1:]:
        wsum = wsum + t
    cnt = jnp.dot(tri_ref[...], mask.astype(BF16), preferred_element_type=F32) + carry_ref[...]
    col = lax.broadcasted_iota(jnp.int32, (tm, ROUTE_COLS), 1)
    idx_o = jnp.zeros((tm, ROUTE_COLS), F32)
    wt_o = jnp.zeros((tm, ROUTE_COLS), F32)
    rank_o = jnp.zeros((tm, ROUTE_COLS), F32)
    for kk in range(TOP_K):
        rk = jnp.sum(jnp.where(onehots[kk], cnt, 0.0), axis=-1, keepdims=True)
        idx_o = jnp.where(col == kk, idxs[kk], idx_o)
        wt_o = jnp.where(col == kk, wts[kk] / wsum * ROUTED_SCALE, wt_o)
        rank_o = jnp.where(col == kk, rk, rank_o)
    idx_ref[...] = idx_o.astype(jnp.int32)
    wt_ref[...] = wt_o
    rank_ref[...] = rank_o.astype(jnp.int32)
    carry_ref[...] = carry_ref[...] + jnp.sum(mask, axis=0, keepdims=True)
    cnt_ref[...] = carry_ref[...]


def router(h, w_router, b_router, tm=512):
    m, k = h.shape
    ne = w_router.shape[1]
    tm = min(tm, m)
    r = jnp.arange(tm)
    tri = (r[:, None] > r[None, :]).astype(BF16)
    small = pl.BlockSpec((tm, ROUTE_COLS), lambda i: (i, 0))
    return pl.pallas_call(
        _router_kernel,
        grid=(m // tm,),
        in_specs=[pl.BlockSpec((tm, k), lambda i: (i, 0)),
                  pl.BlockSpec((k, ne), lambda i: (0, 0)),
                  pl.BlockSpec((1, ne), lambda i: (0, 0)),
                  pl.BlockSpec((tm, tm), lambda i: (0, 0))],
        out_specs=[small, small, small, pl.BlockSpec((1, ne), lambda i: (0, 0))],
        out_shape=[jax.ShapeDtypeStruct((m, ROUTE_COLS), jnp.int32),
                   jax.ShapeDtypeStruct((m, ROUTE_COLS), F32),
                   jax.ShapeDtypeStruct((m, ROUTE_COLS), jnp.int32),
                   jax.ShapeDtypeStruct((1, ne), F32)],
        scratch_shapes=[pltpu.VMEM((1, ne), F32)],
        compiler_params=_params("arbitrary"),
        name="router",
    )(h, w_router, b_router, tri)


def _experts_kernel(be_ref, bv_ref, x_ref, wgu_ref, wd_ref, o_ref):
    i = pl.program_id(0)
    ff = wd_ref.shape[1]

    @pl.when(bv_ref[i] > 0)
    def _():
        half = x_ref.shape[1]
        xa, xb = _unpack_halves(x_ref[...])
        gu = (jnp.dot(xa.astype(BF16), wgu_ref[0, :half, :], preferred_element_type=F32)
              + jnp.dot(xb.astype(BF16), wgu_ref[0, half:, :], preferred_element_type=F32))
        g = gu[:, :ff]
        a = (g * _sigmoid(g) * gu[:, ff:]).astype(BF16)
        o_ref[...] = _pack_halves(jnp.dot(a, wd_ref[0], preferred_element_type=F32))

    @pl.when(bv_ref[i] == 0)
    def _():
        o_ref[...] = jnp.zeros_like(o_ref)


def experts(xs, block_e, block_valid, w_gu, w_down):
    r, half = xs.shape
    d = 2 * half
    ff2 = w_gu.shape[2]
    nblk = r // MOE_BLOCK
    return pl.pallas_call(
        _experts_kernel,
        grid_spec=pltpu.PrefetchScalarGridSpec(
            num_scalar_prefetch=2,
            grid=(nblk,),
            in_specs=[pl.BlockSpec((MOE_BLOCK, half), lambda i, be, bv: (i, 0)),
                      pl.BlockSpec((1, d, ff2), lambda i, be, bv: (be[i], 0, 0)),
                      pl.BlockSpec((1, ff2 // 2, d), lambda i, be, bv: (be[i], 0, 0))],
            out_specs=pl.BlockSpec((MOE_BLOCK, half), lambda i, be, bv: (i, 0))),
        out_shape=jax.ShapeDtypeStruct((r, half), jnp.uint32),
        compiler_params=_params("arbitrary"),
        name="experts",
    )(block_e, block_valid, xs, w_gu, w_down)


def _shared_ln_kernel(xb_ref, h_ref, yg_ref, wt_ref, wgu_ref, wd_ref, g_ref, b_ref, o_ref,
                      ob_ref):
    ff = wd_ref.shape[0]
    gu = jnp.dot(xb_ref[...], wgu_ref[...], preferred_element_type=F32)
    g = gu[:, :ff]
    a = (g * _sigmoid(g) * gu[:, ff:]).astype(BF16)
    shared = jnp.dot(a, wd_ref[...], preferred_element_type=F32)
    wt = wt_ref[...]
    r_lo = r_hi = None
    for kk in range(TOP_K):
        lo, hi = _unpack_halves(yg_ref[kk])
        wk = wt[:, kk:kk + 1]
        r_lo = wk * lo if r_lo is None else r_lo + wk * lo
        r_hi = wk * hi if r_hi is None else r_hi + wk * hi
    routed = jnp.concatenate([r_lo, r_hi], axis=1)
    out = _layernorm_rows(ALPHA * h_ref[...] + (routed + shared), g_ref[...], b_ref[...])
    o_ref[...] = out
    ob_ref[...] = out.astype(BF16)


def shared_ln(h_bf, h, yg, wt, w_gu, w_down, g, b, tm=256):
    m, d = h.shape
    ff2 = w_gu.shape[1]
    tm = min(tm, m)
    row = pl.BlockSpec((tm, d), lambda i: (i, 0))
    vec = pl.BlockSpec((1, d), lambda i: (0, 0))
    return pl.pallas_call(
        _shared_ln_kernel,
        grid=(m // tm,),
        in_specs=[row, row,
                  pl.BlockSpec((TOP_K, tm, d // 2), lambda i: (0, i, 0)),
                  pl.BlockSpec((tm, ROUTE_COLS), lambda i: (i, 0)),
                  pl.BlockSpec((d, ff2), lambda i: (0, 0)),
                  pl.BlockSpec((ff2 // 2, d), lambda i: (0, 0)),
                  vec, vec],
        out_specs=[row, row],
        out_shape=[jax.ShapeDtypeStruct((m, d), F32), jax.ShapeDtypeStruct((m, d), BF16)],
        compiler_params=_params("parallel"),
        name="shared_ln",
    )(h_bf, h, yg, wt, w_gu, w_down, g, b)


def _mixer_layout(d):
    rq = RET_H * RET_DK
    rv = RET_H * RET_DV
    fq = FOX_H * FOX_DH
    main = 2 * rq + 2 * rv + 3 * fq
    ff0 = main
    cu0 = ff0 + FOX_H
    gl0 = cu0 + 2 * d
    end = gl0 + N_BRANCH * d
    return dict(rq=rq, rv=rv, fq=fq, main=main, ff0=ff0, cu0=cu0, gl0=gl0, end=end)


def _prep_in_weights(w_in, b_in, d, tn):
    lay = _mixer_layout(d)
    body = lay["main"] + 2 * d + N_BRANCH * d
    pad = tn - FOX_H
    w = jnp.concatenate([w_in[:, :lay["main"]], w_in[:, lay["cu0"]:lay["end"]],
                         w_in[:, lay["ff0"]:lay["cu0"]],
                         jnp.zeros((w_in.shape[0], pad), w_in.dtype)], axis=1).astype(BF16)
    bb = jnp.concatenate([b_in[:lay["main"]], b_in[lay["cu0"]:lay["end"]],
                          b_in[lay["ff0"]:lay["cu0"]], jnp.zeros((pad,), b_in.dtype)])
    return w, bb.reshape(1, -1).astype(F32), body


def _moe_sublayer(h, h_bf, h_pk, w_router, b_router, w_gu, w_down, w_sh_gu, w_sh_down, g, b):
    t, d = h.shape
    ne = w_router.shape[1]
    idx8, wt8, rank8, cnt = router(h, w_router, b_router.reshape(1, ne))
    idx = idx8[:, :TOP_K]
    rank = rank8[:, :TOP_K]
    sizes = cnt[0].astype(jnp.int32)
    a = t * TOP_K
    n_blocks = -(-a // MOE_BLOCK) + ne
    r = n_blocks * MOE_BLOCK
    padded = (sizes + MOE_BLOCK - 1) // MOE_BLOCK * MOE_BLOCK
    pad_end = jnp.cumsum(padded)
    pad_start = pad_end - padded
    dest = pad_start[idx] + rank
    tok = jnp.broadcast_to(jnp.arange(t, dtype=jnp.int32)[:, None], (t, TOP_K))
    row_tok = jnp.full((r,), t, jnp.int32).at[dest.reshape(a)].set(
        tok.reshape(a), unique_indices=True, mode="promise_in_bounds")
    blk0 = jnp.arange(n_blocks, dtype=jnp.int32) * MOE_BLOCK
    block_e = jnp.minimum(jnp.sum((pad_end[None, :] <= blk0[:, None]).astype(jnp.int32), axis=1),
                          ne - 1)
    block_valid = (blk0 < pad_end[-1]).astype(jnp.int32)
    h_pad = jnp.concatenate([h_pk, jnp.zeros((1, d // 2), jnp.uint32)], axis=0)
    xs = h_pad[row_tok]
    ys = experts(xs, block_e, block_valid, w_gu, w_down)
    yg = ys[dest.T.reshape(a)].reshape(TOP_K, t, d // 2)
    return shared_ln(h_bf, h, yg, wt8, w_sh_gu, w_sh_down, g, b)


def kernel(x, w_in, b_in, ret_gn_g, conv_w, conv_b, conv_ln_g, conv_ln_b, w_ret_o, w_fox_o,
           w_conv_o, w_out, ln1_g, ln1_b, w_router, b_router, w_exp_gate, w_exp_up, w_exp_down,
           w_sh_gate, w_sh_up, w_sh_down, ln2_g, ln2_b):
    bsz, seq, d = x.shape
    t = bsz * seq
    in_tn = 512
    lay = _mixer_layout(d)
    chunk = min(RET_CHUNK, seq)
    tables = _retention_tables(seq, chunk)
    fq0 = 2 * lay["rq"] + 2 * lay["rv"]

    def row(v):
        return v.reshape(1, -1)

    def layer(carry, p):
        h, h_bf = carry
        w_cat, b_cat, body = _prep_in_weights(p["w_in"], p["b_in"], d, in_tn)
        z2 = in_proj(h_bf, w_cat, b_cat, tn=in_tn)
        z = z2.reshape(bsz, seq, -1)
        ret_o = retention(z, row(p["ret_gn_g"]), tables)
        negc = fox_cumsum(z, body // LANES)
        fox_o = fox_attention(z, negc, fq0 // FOX_DH, (fq0 + lay["fq"]) // FOX_DH,
                              (fq0 + 2 * lay["fq"]) // FOX_DH)
        conv_o = conformer_conv(z, lay["main"] // d, lay["main"] // d + 1, p["conv_w"],
                                row(p["conv_b"]), row(p["conv_ln_g"]), row(p["conv_ln_b"]))
        merged = merge(ret_o.reshape(t, -1), fox_o.reshape(t, -1), conv_o.reshape(t, -1),
                       p["w_ret_o"].astype(BF16), p["w_fox_o"].astype(BF16),
                       p["w_conv_o"].astype(BF16), z2, (lay["main"] + 2 * d) // d)
        h, h_bf, h_pk = out_ln(merged, p["w_out"].astype(BF16), h, row(p["ln1_g"]),
                               row(p["ln1_b"]))
        w_gu = jnp.concatenate([p["w_exp_gate"], p["w_exp_up"]], axis=-1).astype(BF16)
        w_sh_gu = jnp.concatenate([p["w_sh_gate"], p["w_sh_up"]], axis=-1).astype(BF16)
        h, h_bf = _moe_sublayer(h, h_bf, h_pk, p["w_router"], p["b_router"], w_gu,
                                p["w_exp_down"].astype(BF16), w_sh_gu,
                                p["w_sh_down"].astype(BF16), row(p["ln2_g"]), row(p["ln2_b"]))
        return (h, h_bf), None

    params = dict(w_in=w_in, b_in=b_in, ret_gn_g=ret_gn_g, conv_w=conv_w, conv_b=conv_b,
                  conv_ln_g=conv_ln_g, conv_ln_b=conv_ln_b, w_ret_o=w_ret_o, w_fox_o=w_fox_o,
                  w_conv_o=w_conv_o, w_out=w_out, ln1_g=ln1_g, ln1_b=ln1_b, w_router=w_router,
                  b_router=b_router, w_exp_gate=w_exp_gate, w_exp_up=w_exp_up,
                  w_exp_down=w_exp_down, w_sh_gate=w_sh_gate, w_sh_up=w_sh_up,
                  w_sh_down=w_sh_down, ln2_g=ln2_g, ln2_b=ln2_b)
    h0 = x.reshape(t, d)
    (h, _), _ = lax.scan(layer, (h0, h0.astype(BF16)), params)
    return h.reshape(bsz, seq, d)
```

```python
import functools

import jax
import jax.numpy as jnp
from jax import lax
from jax.experimental import pallas as pl
from jax.experimental.pallas import tpu as pltpu

F32 = jnp.float32
BF16 = jnp.bfloat16

RET_H = 8
RET_DK = 128
RET_DV = 256
ROPE_BASE = 10000.0
FOX_H = 16
FOX_DH = 128
CONV_W = 31
N_BRANCH = 3
TOP_K = 6
ROUTED_SCALE = 2.5
DEPTH_FOR_NORM = 4
ALPHA = (2 * DEPTH_FOR_NORM) ** 0.25
LN_EPS = 1e-5

LANES = 128
SUBLANES = 8
VMEM_LIMIT = 56 * 1024 * 1024

IN_PAD = 512
IN_TN = 1536
RET_CHUNK = 256
FOX_TQ = 2048
FOX_TK = 512
FOX_HG = 1
CUM_BLK = 256
CONV_TS = 256
CONV_HALO = 32
CONV_RB = 64
CONV_CB = 256
MOE_BLOCK = 512
NEG_BIG = -1e30
LOG2E = 1.4426950408889634


def _params(*sem):
    return pltpu.CompilerParams(dimension_semantics=sem, vmem_limit_bytes=VMEM_LIMIT)


def _sigmoid(x):
    return 1.0 / (1.0 + jnp.exp(-x))


def _pack_halves(x):
    c = x.shape[1] // 2
    xb = x.astype(BF16).astype(F32)
    lo = lax.bitcast_convert_type(xb[:, :c], jnp.uint32) >> 16
    hi = lax.bitcast_convert_type(xb[:, c:], jnp.uint32) & jnp.uint32(0xFFFF0000)
    return lo | hi


def _unpack_halves(u):
    lo = lax.bitcast_convert_type(u << 16, F32)
    hi = lax.bitcast_convert_type(u & jnp.uint32(0xFFFF0000), F32)
    return lo, hi


def _layernorm_rows(x, g, b):
    mu = jnp.mean(x, axis=-1, keepdims=True)
    xc = x - mu
    var = jnp.mean(xc * xc, axis=-1, keepdims=True)
    return xc * lax.rsqrt(var + LN_EPS) * g + b


def _in_proj_kernel(x_ref, w_ref, b_ref, o_ref):
    acc = jnp.dot(x_ref[...], w_ref[...], preferred_element_type=F32)
    o_ref[...] = acc + b_ref[...]


def in_proj(x_bf, w_bf, b, tm=1024, tn=512):
    m, k = x_bf.shape
    n = w_bf.shape[1]
    tm = min(tm, m)
    return pl.pallas_call(
        _in_proj_kernel,
        grid=(m // tm, n // tn),
        in_specs=[pl.BlockSpec((tm, k), lambda i, j: (i, 0)),
                  pl.BlockSpec((k, tn), lambda i, j: (0, j)),
                  pl.BlockSpec((1, tn), lambda i, j: (0, j))],
        out_specs=pl.BlockSpec((tm, tn), lambda i, j: (i, j)),
        out_shape=jax.ShapeDtypeStruct((m, n), F32),
        compiler_params=_params("parallel", "parallel"),
        name="in_proj",
    )(x_bf, w_bf, b)


def _retention_kernel(cd_ref, q_ref, k_ref, v_ref, g_ref, cos_ref, sin_ref, decay_ref,
                      xi_ref, zeta_ref, gn_ref, o_ref, state_ref):
    @pl.when(pl.program_id(1) == 0)
    def _():
        state_ref[...] = jnp.zeros_like(state_ref)

    cos = cos_ref[...]
    sin = sin_ref[...]
    for h in range(RET_H):
        qs = slice(h * RET_DK, (h + 1) * RET_DK)
        vs = slice(h * RET_DV, (h + 1) * RET_DV)
        q = q_ref[0, :, qs]
        k = k_ref[0, :, qs]
        qr = q * cos + pltpu.roll(q, RET_DK // 2, 1) * sin
        kr = (k * cos + pltpu.roll(k, RET_DK // 2, 1) * sin) * (RET_DK ** -0.5)
        vb = v_ref[0, :, vs].astype(BF16)
        s = lax.dot_general(qr.astype(BF16), kr.astype(BF16), (((1,), (1,)), ((), ())),
                            preferred_element_type=F32)
        inner = (s * decay_ref[h]).astype(BF16)
        o = jnp.dot(inner, vb, preferred_element_type=F32)
        st = state_ref[h]
        o = o + jnp.dot((qr * xi_ref[h]).astype(BF16), st.astype(BF16),
                        preferred_element_type=F32)
        kz = (kr * zeta_ref[h]).astype(BF16)
        state_ref[h] = st * cd_ref[h] + lax.dot_general(
            kz, vb, (((0,), (0,)), ((), ())), preferred_element_type=F32)
        mu = jnp.mean(o, axis=-1, keepdims=True)
        oc = o - mu
        var = jnp.mean(oc * oc, axis=-1, keepdims=True)
        on = oc * lax.rsqrt(var + LN_EPS) * gn_ref[:, vs]
        g = g_ref[0, :, vs]
        o_ref[0, :, vs] = (g * _sigmoid(g) * on).astype(BF16)


def _retention_tables(seq, chunk):
    h = jnp.arange(RET_H, dtype=F32)
    log_g = jnp.log1p(-jnp.exp2(-5.0 - h))
    i = jnp.arange(chunk, dtype=F32)
    diff = i[:, None] - i[None, :]
    decay = jnp.where(diff >= 0, jnp.exp(log_g[:, None, None] * jnp.maximum(diff, 0.0)), 0.0)
    xi = jnp.exp(log_g[:, None] * (i + 1.0))
    zeta = jnp.exp(log_g[:, None] * (chunk - 1.0 - i))
    xi = jnp.broadcast_to(xi[:, :, None], (RET_H, chunk, RET_DK))
    zeta = jnp.broadcast_to(zeta[:, :, None], (RET_H, chunk, RET_DK))
    cd = jnp.exp(log_g * chunk)
    half = RET_DK // 2
    inv = 1.0 / (ROPE_BASE ** (jnp.arange(half, dtype=F32) / half))
    ang = jnp.arange(seq, dtype=F32)[:, None] * inv[None, :]
    cos = jnp.cos(ang)
    sin = jnp.sin(ang)
    cos2 = jnp.concatenate([cos, cos], axis=-1)
    sin2 = jnp.concatenate([-sin, sin], axis=-1)
    return cd, cos2, sin2, decay, xi, zeta


def retention(z, gn_g, tables):
    b, s, _ = z.shape
    cd, cos2, sin2, decay, xi, zeta = tables
    L = decay.shape[1]
    qw = RET_H * RET_DK
    vw = RET_H * RET_DV
    return pl.pallas_call(
        _retention_kernel,
        grid=(b, s // L),
        in_specs=[pl.BlockSpec(memory_space=pltpu.SMEM),
                  pl.BlockSpec((1, L, qw), lambda bi, n: (bi, n, 0)),
                  pl.BlockSpec((1, L, qw), lambda bi, n: (bi, n, 1)),
                  pl.BlockSpec((1, L, vw), lambda bi, n: (bi, n, 1)),
                  pl.BlockSpec((1, L, vw), lambda bi, n: (bi, n, 2)),
                  pl.BlockSpec((L, RET_DK), lambda bi, n: (n, 0)),
                  pl.BlockSpec((L, RET_DK), lambda bi, n: (n, 0)),
                  pl.BlockSpec((RET_H, L, L), lambda bi, n: (0, 0, 0)),
                  pl.BlockSpec((RET_H, L, RET_DK), lambda bi, n: (0, 0, 0)),
                  pl.BlockSpec((RET_H, L, RET_DK), lambda bi, n: (0, 0, 0)),
                  pl.BlockSpec((1, vw), lambda bi, n: (0, 0))],
        out_specs=pl.BlockSpec((1, L, vw), lambda bi, n: (bi, n, 0)),
        out_shape=jax.ShapeDtypeStruct((b, s, vw), BF16),
        scratch_shapes=[pltpu.VMEM((RET_H, RET_DK, RET_DV), F32)],
        compiler_params=_params("parallel", "arbitrary"),
        name="retention",
    )(cd, z, z, z, z, cos2, sin2, decay, xi, zeta, gn_g)


def _fox_cumsum_kernel(f_ref, tri_ref, o_ref):
    s = f_ref.shape[1]
    tri = tri_ref[...]
    carry = jnp.zeros((1, LANES), F32)
    for i in range(s // CUM_BLK):
        f = f_ref[0, i * CUM_BLK:(i + 1) * CUM_BLK, :]
        ls = jnp.minimum(f, 0.0) - jnp.log1p(jnp.exp(-jnp.abs(f)))
        hi = ls.astype(BF16)
        r1 = ls - hi.astype(F32)
        mid = r1.astype(BF16)
        lo = (r1 - mid.astype(F32)).astype(BF16)
        c = (jnp.dot(tri, hi, preferred_element_type=F32)
             + jnp.dot(tri, mid, preferred_element_type=F32)
             + jnp.dot(tri, lo, preferred_element_type=F32)) + carry
        carry = c[CUM_BLK - 1:CUM_BLK, :]
        o_ref[0, :, i * CUM_BLK:(i + 1) * CUM_BLK] = (-c).T[:FOX_H, :]


def fox_cumsum(z, ff_block):
    b, s, _ = z.shape
    r = jnp.arange(CUM_BLK)
    tri = (r[:, None] >= r[None, :]).astype(BF16)
    return pl.pallas_call(
        _fox_cumsum_kernel,
        grid=(b,),
        in_specs=[pl.BlockSpec((1, s, LANES), lambda bi: (bi, 0, ff_block)),
                  pl.BlockSpec((CUM_BLK, CUM_BLK), lambda bi: (0, 0))],
        out_specs=pl.BlockSpec((1, FOX_H, s), lambda bi: (bi, 0, 0)),
        out_shape=jax.ShapeDtypeStruct((b, FOX_H, s), F32),
        compiler_params=_params("parallel"),
        name="fox_cumsum",
    )(z, tri)


def _fox_attn_kernel(q_ref, k_ref, v_ref, nc_ref, o_ref, kt_ref, vx_ref, qb_ref, m_ref, acc_ref):
    qi = pl.program_id(2)
    s_len = k_ref.shape[1]

    def head(g):
        return slice(g * FOX_DH, (g + 1) * FOX_DH)

    @pl.when(qi == 0)
    def _():
        for g in range(FOX_HG):
            for c in range(s_len // FOX_TK):
                rows = slice(c * FOX_TK, (c + 1) * FOX_TK)
                kt_ref[g, :, rows] = k_ref[0, rows, head(g)].T.astype(BF16)
                vx_ref[g, rows, :FOX_DH] = v_ref[0, rows, head(g)].astype(BF16)
            vx_ref[g, :, FOX_DH:] = jnp.ones((s_len, FOX_DH), BF16)

    for g in range(FOX_HG):
        qb_ref[g] = (q_ref[0, :, head(g)] * (FOX_DH ** -0.5 * LOG2E)).astype(BF16)
    m_ref[...] = jnp.full_like(m_ref, NEG_BIG)
    acc_ref[...] = jnp.zeros_like(acc_ref)

    def tile(j, diag):
        k0 = pl.multiple_of(j * FOX_TK, FOX_TK)
        rows = slice(0 if diag is None else diag * FOX_TK, FOX_TQ)
        for g in range(FOX_HG):
            s = jnp.dot(qb_ref[g, rows, :], kt_ref[g, :, pl.ds(k0, FOX_TK)],
                        preferred_element_type=F32)
            s = s + nc_ref[0, g, :, pl.ds(k0, FOX_TK)] * LOG2E
            if diag is not None:
                row = lax.broadcasted_iota(jnp.int32, s.shape, 0)
                col = lax.broadcasted_iota(jnp.int32, s.shape, 1)
                s = jnp.where(col <= row, s, NEG_BIG)
            m_prev = m_ref[g, rows, :]
            m_new = jnp.maximum(m_prev, jnp.max(s, axis=-1, keepdims=True))
            alpha = jnp.exp2(m_prev - m_new)
            p = jnp.concatenate(
                [jnp.exp2(s[:, c * LANES:(c + 1) * LANES] - m_new)
                 for c in range(FOX_TK // LANES)], axis=1).astype(BF16)
            pv = jnp.dot(p, vx_ref[g, pl.ds(k0, FOX_TK), :], preferred_element_type=F32)
            acc_ref[g, rows, :] = jnp.concatenate([alpha, alpha], axis=1) * acc_ref[g, rows, :] + pv
            m_ref[g, rows, :] = m_new

    def body(j, c):
        tile(j, None)
        return c

    per_q = FOX_TQ // FOX_TK
    n_full = qi * per_q
    lax.fori_loop(0, n_full, body, 0)
    for r in range(per_q):
        tile(n_full + r, r)
    for g in range(FOX_HG):
        o_ref[0, :, head(g)] = (acc_ref[g, :, :FOX_DH] / acc_ref[g, :, FOX_DH:]).astype(BF16)


def fox_attention(z, negc, q_blk, k_blk, v_blk):
    b, s, _ = z.shape
    assert FOX_TQ % FOX_TK == 0 and FOX_DH == LANES and FOX_H % FOX_HG == 0
    gw = FOX_HG * FOX_DH
    negc4 = negc.reshape(b, FOX_H, 1, s)
    qg, kg, vg = q_blk // FOX_HG, k_blk // FOX_HG, v_blk // FOX_HG
    return pl.pallas_call(
        _fox_attn_kernel,
        grid=(b, FOX_H // FOX_HG, s // FOX_TQ),
        in_specs=[pl.BlockSpec((1, FOX_TQ, gw), lambda bi, h, qi: (bi, qi, qg + h)),
                  pl.BlockSpec((1, s, gw), lambda bi, h, qi: (bi, 0, kg + h)),
                  pl.BlockSpec((1, s, gw), lambda bi, h, qi: (bi, 0, vg + h)),
                  pl.BlockSpec((1, FOX_HG, 1, s), lambda bi, h, qi: (bi, h, 0, 0))],
        out_specs=pl.BlockSpec((1, FOX_TQ, gw), lambda bi, h, qi: (bi, qi, h)),
        out_shape=jax.ShapeDtypeStruct((b, s, FOX_H * FOX_DH), BF16),
        scratch_shapes=[pltpu.VMEM((FOX_HG, FOX_DH, s), BF16),
                        pltpu.VMEM((FOX_HG, s, 2 * FOX_DH), BF16),
                        pltpu.VMEM((FOX_HG, FOX_TQ, FOX_DH), BF16),
                        pltpu.VMEM((FOX_HG, FOX_TQ, LANES), F32),
                        pltpu.VMEM((FOX_HG, FOX_TQ, 2 * FOX_DH), F32)],
        compiler_params=_params("parallel", "parallel", "arbitrary"),
        name="fox_attn",
    )(z, z, z, negc4)


def _conv_kernel(a_ref, g_ref, ap_ref, gp_ref, w_ref, cb_ref, lg_ref, lb_ref, o_ref, y_ref,
                 c_ref):
    ts = a_ref.shape[1]
    c_all = a_ref.shape[2]
    prev = ap_ref[0] * _sigmoid(gp_ref[0])
    y_ref[0:CONV_HALO, :] = jnp.where(pl.program_id(1) > 0, prev, 0.0)
    y_ref[CONV_HALO:CONV_HALO + ts, :] = a_ref[0] * _sigmoid(g_ref[0])
    lead = CONV_HALO - (CONV_W - 1)

    n_cb = c_all // CONV_CB
    win_rows = CONV_RB + CONV_HALO

    def block(idx, carry):
        r0 = pl.multiple_of((idx // n_cb) * CONV_RB, CONV_RB)
        c0 = pl.multiple_of((idx % n_cb) * CONV_CB, CONV_CB)
        win = y_ref[pl.ds(r0, win_rows), pl.ds(c0, CONV_CB)]
        acc = jnp.zeros((CONV_RB, CONV_CB), F32)
        for ph in range(SUBLANES):
            rot = win if ph == 0 else pltpu.roll(win, win_rows - ph, 0)
            for w in range(CONV_W):
                if (w + lead) % SUBLANES == ph:
                    a0 = (w + lead) - ph
                    acc = acc + w_ref[w:w + 1, pl.ds(c0, CONV_CB)] * rot[a0:a0 + CONV_RB]
        c_ref[pl.ds(r0, CONV_RB), pl.ds(c0, CONV_CB)] = acc + cb_ref[:, pl.ds(c0, CONV_CB)]
        return carry

    lax.fori_loop(0, (ts // CONV_RB) * n_cb, block, 0)
    y = _layernorm_rows(c_ref[...], lg_ref[...], lb_ref[...])
    o_ref[0] = (y * _sigmoid(y)).astype(BF16)


def conformer_conv(z, a_blk, g_blk, conv_w, conv_b, ln_g, ln_b):
    b, s, _ = z.shape
    c = conv_w.shape[1]
    ts = min(CONV_TS, s)
    hb = ts // CONV_HALO

    def halo(col):
        return lambda bi, i: (bi, jnp.maximum(i * hb - 1, 0), col)

    return pl.pallas_call(
        _conv_kernel,
        grid=(b, s // ts),
        in_specs=[pl.BlockSpec((1, ts, c), lambda bi, i: (bi, i, a_blk)),
                  pl.BlockSpec((1, ts, c), lambda bi, i: (bi, i, g_blk)),
                  pl.BlockSpec((1, CONV_HALO, c), halo(a_blk)),
                  pl.BlockSpec((1, CONV_HALO, c), halo(g_blk)),
                  pl.BlockSpec((CONV_W, c), lambda bi, i: (0, 0)),
                  pl.BlockSpec((1, c), lambda bi, i: (0, 0)),
                  pl.BlockSpec((1, c), lambda bi, i: (0, 0)),
                  pl.BlockSpec((1, c), lambda bi, i: (0, 0))],
        out_specs=pl.BlockSpec((1, ts, c), lambda bi, i: (bi, i, 0)),
        out_shape=jax.ShapeDtypeStruct((b, s, c), BF16),
        scratch_shapes=[pltpu.VMEM((CONV_HALO + ts, c), F32),
                        pltpu.VMEM((ts, c), F32)],
        compiler_params=_params("parallel", "parallel"),
        name="conv",
    )(z, z, z, z, conv_w, conv_b, ln_g, ln_b)


def _merge_kernel(a_ref, f_ref, c_ref, wa_ref, wf_ref, wc_ref, g0_ref, g1_ref, g2_ref, o_ref):
    ya = jnp.dot(a_ref[...], wa_ref[...], preferred_element_type=F32)
    m = _sigmoid(g0_ref[...]) * ya
    yf = jnp.dot(f_ref[...], wf_ref[...], preferred_element_type=F32)
    m = m + _sigmoid(g1_ref[...]) * yf
    yc = jnp.dot(c_ref[...], wc_ref[...], preferred_element_type=F32)
    m = m + _sigmoid(g2_ref[...]) * yc
    o_ref[...] = m.astype(BF16)


def merge(ret_o, fox_o, conv_o, w_ret, w_fox, w_conv, z2, gl_blk, tm=512, tn=512):
    m, k = ret_o.shape
    n = w_ret.shape[1]
    tm = min(tm, m)
    nb = n // tn
    act = pl.BlockSpec((tm, k), lambda i, j: (i, 0))
    wsp = pl.BlockSpec((k, tn), lambda i, j: (0, j))

    def gate(br):
        return pl.BlockSpec((tm, tn), lambda i, j: (i, gl_blk * nb + br * nb + j))

    return pl.pallas_call(
        _merge_kernel,
        grid=(m // tm, nb),
        in_specs=[act, act, act, wsp, wsp, wsp, gate(0), gate(1), gate(2)],
        out_specs=pl.BlockSpec((tm, tn), lambda i, j: (i, j)),
        out_shape=jax.ShapeDtypeStruct((m, n), BF16),
        compiler_params=_params("parallel", "parallel"),
        name="merge",
    )(ret_o, fox_o, conv_o, w_ret, w_fox, w_conv, z2, z2, z2)


def _out_ln_kernel(x_ref, w_ref, h_ref, g_ref, b_ref, o_ref, ob_ref, op_ref):
    y = jnp.dot(x_ref[...], w_ref[...], preferred_element_type=F32)
    out = _layernorm_rows(ALPHA * h_ref[...] + y, g_ref[...], b_ref[...])
    o_ref[...] = out
    ob_ref[...] = out.astype(BF16)
    op_ref[...] = _pack_halves(out)


def out_ln(merged, w_out, h, g, b, tm=512):
    m, k = merged.shape
    n = w_out.shape[1]
    tm = min(tm, m)
    return pl.pallas_call(
        _out_ln_kernel,
        grid=(m // tm,),
        in_specs=[pl.BlockSpec((tm, k), lambda i: (i, 0)),
                  pl.BlockSpec((k, n), lambda i: (0, 0)),
                  pl.BlockSpec((tm, n), lambda i: (i, 0)),
                  pl.BlockSpec((1, n), lambda i: (0, 0)),
                  pl.BlockSpec((1, n), lambda i: (0, 0))],
        out_specs=[pl.BlockSpec((tm, n), lambda i: (i, 0)),
                   pl.BlockSpec((tm, n), lambda i: (i, 0)),
                   pl.BlockSpec((tm, n // 2), lambda i: (i, 0))],
        out_shape=[jax.ShapeDtypeStruct((m, n), F32), jax.ShapeDtypeStruct((m, n), BF16),
                   jax.ShapeDtypeStruct((m, n // 2), jnp.uint32)],
        compiler_params=_params("parallel"),
        name="out_ln",
    )(merged, w_out, h, g, b)


ROUTE_COLS = 8


def _router_kernel(h_ref, w_ref, b_ref, tri_ref, idx_ref, wt_ref, rank_ref, cnt_ref, carry_ref):
    @pl.when(pl.program_id(0) == 0)
    def _():
        carry_ref[...] = jnp.zeros_like(carry_ref)

    h = h_ref[...]
    hh = h.astype(BF16)
    hl = (h - hh.astype(F32)).astype(BF16)
    w = w_ref[...]
    wh = w.astype(BF16)
    wl = (w - wh.astype(F32)).astype(BF16)
    logits = (jnp.dot(hh, wh, preferred_element_type=F32)
              + jnp.dot(hl, wh, preferred_element_type=F32)
              + jnp.dot(hh, wl, preferred_element_type=F32))
    scores = _sigmoid(logits)
    tm, ne = scores.shape
    sel = scores + b_ref[...]
    lane = lax.broadcasted_iota(jnp.int32, (tm, ne), 1).astype(F32)
    mask = jnp.zeros((tm, ne), F32)
    onehots, idxs, wts = [], [], []
    for _ in range(TOP_K):
        mx = jnp.max(sel, axis=-1, keepdims=True)
        ik = jnp.min(jnp.where(sel == mx, lane, float(ne)), axis=-1, keepdims=True)
        oh = lane == ik
        wts.append(jnp.sum(jnp.where(oh, scores, 0.0), axis=-1, keepdims=True))
        idxs.append(ik)
        onehots.append(oh)
        sel = jnp.where(oh, -jnp.inf, sel)
        mask = jnp.where(oh, 1.0, mask)
    wsum = wts[0]
    for t in wts[1:]:
        wsum = wsum + t
    cnt = jnp.dot(tri_ref[...], mask.astype(BF16), preferred_element_type=F32) + carry_ref[...]
    col = lax.broadcasted_iota(jnp.int32, (tm, ROUTE_COLS), 1)
    idx_o = jnp.zeros((tm, ROUTE_COLS), F32)
    wt_o = jnp.zeros((tm, ROUTE_COLS), F32)
    rank_o = jnp.zeros((tm, ROUTE_COLS), F32)
    for kk in range(TOP_K):
        rk = jnp.sum(jnp.where(onehots[kk], cnt, 0.0), axis=-1, keepdims=True)
        idx_o = jnp.where(col == kk, idxs[kk], idx_o)
        wt_o = jnp.where(col == kk, wts[kk] / wsum * ROUTED_SCALE, wt_o)
        rank_o = jnp.where(col == kk, rk, rank_o)
    idx_ref[...] = idx_o.astype(jnp.int32)
    wt_ref[...] = wt_o
    rank_ref[...] = rank_o.astype(jnp.int32)
    carry_ref[...] = carry_ref[...] + jnp.sum(mask, axis=0, keepdims=True)
    cnt_ref[...] = carry_ref[...]


def router(h, w_router, b_router, tm=512):
    m, k = h.shape
    ne = w_router.shape[1]
    tm = min(tm, m)
    r = jnp.arange(tm)
    tri = (r[:, None] > r[None, :]).astype(BF16)
    small = pl.BlockSpec((tm, ROUTE_COLS), lambda i: (i, 0))
    return pl.pallas_call(
        _router_kernel,
        grid=(m // tm,),
        in_specs=[pl.BlockSpec((tm, k), lambda i: (i, 0)),
                  pl.BlockSpec((k, ne), lambda i: (0, 0)),
                  pl.BlockSpec((1, ne), lambda i: (0, 0)),
                  pl.BlockSpec((tm, tm), lambda i: (0, 0))],
        out_specs=[small, small, small, pl.BlockSpec((1, ne), lambda i: (0, 0))],
        out_shape=[jax.ShapeDtypeStruct((m, ROUTE_COLS), jnp.int32),
                   jax.ShapeDtypeStruct((m, ROUTE_COLS), F32),
                   jax.ShapeDtypeStruct((m, ROUTE_COLS), jnp.int32),
                   jax.ShapeDtypeStruct((1, ne), F32)],
        scratch_shapes=[pltpu.VMEM((1, ne), F32)],
        compiler_params=_params("arbitrary"),
        name="router",
    )(h, w_router, b_router, tri)


def _experts_kernel(be_ref, bv_ref, x_ref, wg_ref, wu_ref, wd_ref, o_ref, wgb_ref, wub_ref,
                    wdb_ref):
    i = pl.program_id(0)

    @pl.when(jnp.logical_or(i == 0, be_ref[i] != be_ref[jnp.maximum(i - 1, 0)]))
    def _():
        wgb_ref[...] = wg_ref[0].astype(BF16)
        wub_ref[...] = wu_ref[0].astype(BF16)
        wdb_ref[...] = wd_ref[0].astype(BF16)

    @pl.when(bv_ref[i] > 0)
    def _():
        half = x_ref.shape[1]
        xa, xb = _unpack_halves(x_ref[...])
        xa = xa.astype(BF16)
        xb = xb.astype(BF16)
        g = (jnp.dot(xa, wgb_ref[:half, :], preferred_element_type=F32)
             + jnp.dot(xb, wgb_ref[half:, :], preferred_element_type=F32))
        u = (jnp.dot(xa, wub_ref[:half, :], preferred_element_type=F32)
             + jnp.dot(xb, wub_ref[half:, :], preferred_element_type=F32))
        a = (g * _sigmoid(g) * u).astype(BF16)
        o_ref[...] = _pack_halves(jnp.dot(a, wdb_ref[...], preferred_element_type=F32))

    @pl.when(bv_ref[i] == 0)
    def _():
        o_ref[...] = jnp.zeros_like(o_ref)


def experts(xs, block_e, block_valid, w_gate, w_up, w_down):
    r, half = xs.shape
    d = 2 * half
    ff = w_gate.shape[2]
    nblk = r // MOE_BLOCK
    return pl.pallas_call(
        _experts_kernel,
        grid_spec=pltpu.PrefetchScalarGridSpec(
            num_scalar_prefetch=2,
            grid=(nblk,),
            in_specs=[pl.BlockSpec((MOE_BLOCK, half), lambda i, be, bv: (i, 0)),
                      pl.BlockSpec((1, d, ff), lambda i, be, bv: (be[i], 0, 0)),
                      pl.BlockSpec((1, d, ff), lambda i, be, bv: (be[i], 0, 0)),
                      pl.BlockSpec((1, ff, d), lambda i, be, bv: (be[i], 0, 0))],
            out_specs=pl.BlockSpec((MOE_BLOCK, half), lambda i, be, bv: (i, 0)),
            scratch_shapes=[pltpu.VMEM((d, ff), BF16), pltpu.VMEM((d, ff), BF16),
                            pltpu.VMEM((ff, d), BF16)]),
        out_shape=jax.ShapeDtypeStruct((r, half), jnp.uint32),
        compiler_params=_params("arbitrary"),
        name="experts",
    )(block_e, block_valid, xs, w_gate, w_up, w_down)


def _shared_ln_kernel(xb_ref, h_ref, yg_ref, wt_ref, wgu_ref, wd_ref, g_ref, b_ref, o_ref,
                      ob_ref):
    ff = wd_ref.shape[0]
    gu = jnp.dot(xb_ref[...], wgu_ref[...], preferred_element_type=F32)
    g = gu[:, :ff]
    a = (g * _sigmoid(g) * gu[:, ff:]).astype(BF16)
    shared = jnp.dot(a, wd_ref[...], preferred_element_type=F32)
    wt = wt_ref[...]
    r_lo = r_hi = None
    for kk in range(TOP_K):
        lo, hi = _unpack_halves(yg_ref[kk])
        wk = wt[:, kk:kk + 1]
        r_lo = wk * lo if r_lo is None else r_lo + wk * lo
        r_hi = wk * hi if r_hi is None else r_hi + wk * hi
    routed = jnp.concatenate([r_lo, r_hi], axis=1)
    out = _layernorm_rows(ALPHA * h_ref[...] + (routed + shared), g_ref[...], b_ref[...])
    o_ref[...] = out
    ob_ref[...] = out.astype(BF16)


def shared_ln(h_bf, h, yg, wt, w_gu, w_down, g, b, tm=256):
    m, d = h.shape
    ff2 = w_gu.shape[1]
    tm = min(tm, m)
    row = pl.BlockSpec((tm, d), lambda i: (i, 0))
    vec = pl.BlockSpec((1, d), lambda i: (0, 0))
    return pl.pallas_call(
        _shared_ln_kernel,
        grid=(m // tm,),
        in_specs=[row, row,
                  pl.BlockSpec((TOP_K, tm, d // 2), lambda i: (0, i, 0)),
                  pl.BlockSpec((tm, ROUTE_COLS), lambda i: (i, 0)),
                  pl.BlockSpec((d, ff2), lambda i: (0, 0)),
                  pl.BlockSpec((ff2 // 2, d), lambda i: (0, 0)),
                  vec, vec],
        out_specs=[row, row],
        out_shape=[jax.ShapeDtypeStruct((m, d), F32), jax.ShapeDtypeStruct((m, d), BF16)],
        compiler_params=_params("parallel"),
        name="shared_ln",
    )(h_bf, h, yg, wt, w_gu, w_down, g, b)


def _mixer_layout(d):
    rq = RET_H * RET_DK
    rv = RET_H * RET_DV
    fq = FOX_H * FOX_DH
    main = 2 * rq + 2 * rv + 3 * fq
    ff0 = main
    cu0 = ff0 + FOX_H
    gl0 = cu0 + 2 * d
    end = gl0 + N_BRANCH * d
    return dict(rq=rq, rv=rv, fq=fq, main=main, ff0=ff0, cu0=cu0, gl0=gl0, end=end)


def _prep_in_weights(w_in, b_in, d, tn):
    lay = _mixer_layout(d)
    body = lay["main"] + 2 * d + N_BRANCH * d
    pad = tn - FOX_H
    w = jnp.concatenate([w_in[:, :lay["main"]], w_in[:, lay["cu0"]:lay["end"]],
                         w_in[:, lay["ff0"]:lay["cu0"]],
                         jnp.zeros((w_in.shape[0], pad), w_in.dtype)], axis=1).astype(BF16)
    bb = jnp.concatenate([b_in[:lay["main"]], b_in[lay["cu0"]:lay["end"]],
                          b_in[lay["ff0"]:lay["cu0"]], jnp.zeros((pad,), b_in.dtype)])
    return w, bb.reshape(1, -1).astype(F32), body


def _moe_sublayer(h, h_bf, h_pk, w_router, b_router, w_gate, w_up, w_down, w_sh_gu, w_sh_down,
                  g, b):
    t, d = h.shape
    ne = w_router.shape[1]
    idx8, wt8, rank8, cnt = router(h, w_router, b_router.reshape(1, ne))
    idx = idx8[:, :TOP_K]
    rank = rank8[:, :TOP_K]
    sizes = cnt[0].astype(jnp.int32)
    a = t * TOP_K
    n_blocks = -(-a // MOE_BLOCK) + ne
    r = n_blocks * MOE_BLOCK
    padded = (sizes + MOE_BLOCK - 1) // MOE_BLOCK * MOE_BLOCK
    pad_end = jnp.cumsum(padded)
    pad_start = pad_end - padded
    dest = pad_start[idx] + rank
    tok = jnp.broadcast_to(jnp.arange(t, dtype=jnp.int32)[:, None], (t, TOP_K))
    row_tok = (jnp.arange(r, dtype=jnp.int32) % t).at[dest.reshape(a)].set(
        tok.reshape(a), unique_indices=True, mode="promise_in_bounds")
    blk0 = jnp.arange(n_blocks, dtype=jnp.int32) * MOE_BLOCK
    block_e = jnp.minimum(jnp.sum((pad_end[None, :] <= blk0[:, None]).astype(jnp.int32), axis=1),
                          ne - 1)
    block_valid = (blk0 < pad_end[-1]).astype(jnp.int32)
    xs = h_pk[row_tok]
    ys = experts(xs, block_e, block_valid, w_gate, w_up, w_down)
    yg = ys[dest.T.reshape(a)].reshape(TOP_K, t, d // 2)
    return shared_ln(h_bf, h, yg, wt8, w_sh_gu, w_sh_down, g, b)


def kernel(x, w_in, b_in, ret_gn_g, conv_w, conv_b, conv_ln_g, conv_ln_b, w_ret_o, w_fox_o,
           w_conv_o, w_out, ln1_g, ln1_b, w_router, b_router, w_exp_gate, w_exp_up, w_exp_down,
           w_sh_gate, w_sh_up, w_sh_down, ln2_g, ln2_b):
    bsz, seq, d = x.shape
    t = bsz * seq
    in_tn = IN_PAD
    lay = _mixer_layout(d)
    chunk = min(RET_CHUNK, seq)
    tables = _retention_tables(seq, chunk)
    fq0 = 2 * lay["rq"] + 2 * lay["rv"]

    def row(v):
        return v.reshape(1, -1)

    def layer(carry, p):
        h, h_bf = carry
        w_cat, b_cat, body = _prep_in_weights(p["w_in"], p["b_in"], d, in_tn)
        z2 = in_proj(h_bf, w_cat, b_cat, tn=IN_TN)
        z = z2.reshape(bsz, seq, -1)
        ret_o = retention(z, row(p["ret_gn_g"]), tables)
        negc = fox_cumsum(z, body // LANES)
        fox_o = fox_attention(z, negc, fq0 // FOX_DH, (fq0 + lay["fq"]) // FOX_DH,
                              (fq0 + 2 * lay["fq"]) // FOX_DH)
        conv_o = conformer_conv(z, lay["main"] // d, lay["main"] // d + 1, p["conv_w"],
                                row(p["conv_b"]), row(p["conv_ln_g"]), row(p["conv_ln_b"]))
        merged = merge(ret_o.reshape(t, -1), fox_o.reshape(t, -1), conv_o.reshape(t, -1),
                       p["w_ret_o"].astype(BF16), p["w_fox_o"].astype(BF16),
                       p["w_conv_o"].astype(BF16), z2, (lay["main"] + 2 * d) // d)
        h, h_bf, h_pk = out_ln(merged, p["w_out"].astype(BF16), h, row(p["ln1_g"]),
                               row(p["ln1_b"]))
        w_sh_gu = jnp.concatenate([p["w_sh_gate"], p["w_sh_up"]], axis=-1).astype(BF16)
        h, h_bf = _moe_sublayer(h, h_bf, h_pk, p["w_router"], p["b_router"], p["w_exp_gate"],
                                p["w_exp_up"], p["w_exp_down"], w_sh_gu,
                                p["w_sh_down"].astype(BF16), row(p["ln2_g"]), row(p["ln2_b"]))
        return (h, h_bf), None

    params = dict(w_in=w_in, b_in=b_in, ret_gn_g=ret_gn_g, conv_w=conv_w, conv_b=conv_b,
                  conv_ln_g=conv_ln_g, conv_ln_b=conv_ln_b, w_ret_o=w_ret_o, w_fox_o=w_fox_o,
                  w_conv_o=w_conv_o, w_out=w_out, ln1_g=ln1_g, ln1_b=ln1_b, w_router=w_router,
                  b_router=b_router, w_exp_gate=w_exp_gate, w_exp_up=w_exp_up,
                  w_exp_down=w_exp_down, w_sh_gate=w_sh_gate, w_sh_up=w_sh_up,
                  w_sh_down=w_sh_down, ln2_g=ln2_g, ln2_b=ln2_b)
    h0 = x.reshape(t, d)
    (h, _), _ = lax.scan(layer, (h0, h0.astype(BF16)), params)
    return h.reshape(bsz, seq, d)
```

```python
import functools

import jax
import jax.numpy as jnp
from jax import lax
from jax.experimental import pallas as pl
from jax.experimental.pallas import tpu as pltpu

F32 = jnp.float32
BF16 = jnp.bfloat16

RET_H = 8
RET_DK = 128
RET_DV = 256
ROPE_BASE = 10000.0
FOX_H = 16
FOX_DH = 128
CONV_W = 31
N_BRANCH = 3
TOP_K = 6
ROUTED_SCALE = 2.5
DEPTH_FOR_NORM = 4
ALPHA = (2 * DEPTH_FOR_NORM) ** 0.25
LN_EPS = 1e-5

LANES = 128
SUBLANES = 8
VMEM_LIMIT = 56 * 1024 * 1024

IN_PAD = 512
IN_TN = 1536
RET_CHUNK = 256
FOX_TQ = 2048
FOX_TK = 512
FOX_HG = 1
CUM_BLK = 256
CONV_TS = 256
CONV_HALO = 32
CONV_RB = 64
CONV_CB = 256
MOE_BLOCK = 512
NEG_BIG = -1e30
LOG2E = 1.4426950408889634


def _params(*sem):
    return pltpu.CompilerParams(dimension_semantics=sem, vmem_limit_bytes=VMEM_LIMIT)


def _sigmoid(x):
    return 1.0 / (1.0 + jnp.exp(-x))


def _pack_halves(x):
    c = x.shape[1] // 2
    xb = x.astype(BF16).astype(F32)
    lo = lax.bitcast_convert_type(xb[:, :c], jnp.uint32) >> 16
    hi = lax.bitcast_convert_type(xb[:, c:], jnp.uint32) & jnp.uint32(0xFFFF0000)
    return lo | hi


def _unpack_halves(u):
    lo = lax.bitcast_convert_type(u << 16, F32)
    hi = lax.bitcast_convert_type(u & jnp.uint32(0xFFFF0000), F32)
    return lo, hi


def _layernorm_rows(x, g, b):
    mu = jnp.mean(x, axis=-1, keepdims=True)
    xc = x - mu
    var = jnp.mean(xc * xc, axis=-1, keepdims=True)
    return xc * lax.rsqrt(var + LN_EPS) * g + b


def _in_proj_kernel(x_ref, w_ref, b_ref, o_ref):
    acc = jnp.dot(x_ref[...], w_ref[...], preferred_element_type=F32)
    o_ref[...] = acc + b_ref[...]


def in_proj(x_bf, w_bf, b, tm=1024, tn=512):
    m, k = x_bf.shape
    n = w_bf.shape[1]
    tm = min(tm, m)
    return pl.pallas_call(
        _in_proj_kernel,
        grid=(m // tm, n // tn),
        in_specs=[pl.BlockSpec((tm, k), lambda i, j: (i, 0)),
                  pl.BlockSpec((k, tn), lambda i, j: (0, j)),
                  pl.BlockSpec((1, tn), lambda i, j: (0, j))],
        out_specs=pl.BlockSpec((tm, tn), lambda i, j: (i, j)),
        out_shape=jax.ShapeDtypeStruct((m, n), F32),
        compiler_params=_params("parallel", "parallel"),
        name="in_proj",
    )(x_bf, w_bf, b)


def _retention_kernel(cd_ref, q_ref, k_ref, v_ref, g_ref, cos_ref, sin_ref, decay_ref,
                      xi_ref, zeta_ref, gn_ref, o_ref, state_ref):
    @pl.when(pl.program_id(1) == 0)
    def _():
        state_ref[...] = jnp.zeros_like(state_ref)

    cos = cos_ref[...]
    sin = sin_ref[...]
    for h in range(RET_H):
        qs = slice(h * RET_DK, (h + 1) * RET_DK)
        vs = slice(h * RET_DV, (h + 1) * RET_DV)
        q = q_ref[0, :, qs]
        k = k_ref[0, :, qs]
        qr = q * cos + pltpu.roll(q, RET_DK // 2, 1) * sin
        kr = (k * cos + pltpu.roll(k, RET_DK // 2, 1) * sin) * (RET_DK ** -0.5)
        vb = v_ref[0, :, vs].astype(BF16)
        s = lax.dot_general(qr.astype(BF16), kr.astype(BF16), (((1,), (1,)), ((), ())),
                            preferred_element_type=F32)
        inner = (s * decay_ref[h]).astype(BF16)
        o = jnp.dot(inner, vb, preferred_element_type=F32)
        st = state_ref[h]
        o = o + jnp.dot((qr * xi_ref[h]).astype(BF16), st.astype(BF16),
                        preferred_element_type=F32)
        kz = (kr * zeta_ref[h]).astype(BF16)
        state_ref[h] = st * cd_ref[h] + lax.dot_general(
            kz, vb, (((0,), (0,)), ((), ())), preferred_element_type=F32)
        mu = jnp.mean(o, axis=-1, keepdims=True)
        oc = o - mu
        var = jnp.mean(oc * oc, axis=-1, keepdims=True)
        on = oc * lax.rsqrt(var + LN_EPS) * gn_ref[:, vs]
        g = g_ref[0, :, vs]
        o_ref[0, :, vs] = (g * _sigmoid(g) * on).astype(BF16)


def _retention_tables(seq, chunk):
    h = jnp.arange(RET_H, dtype=F32)
    log_g = jnp.log1p(-jnp.exp2(-5.0 - h))
    i = jnp.arange(chunk, dtype=F32)
    diff = i[:, None] - i[None, :]
    decay = jnp.where(diff >= 0, jnp.exp(log_g[:, None, None] * jnp.maximum(diff, 0.0)), 0.0)
    xi = jnp.exp(log_g[:, None] * (i + 1.0))
    zeta = jnp.exp(log_g[:, None] * (chunk - 1.0 - i))
    xi = jnp.broadcast_to(xi[:, :, None], (RET_H, chunk, RET_DK))
    zeta = jnp.broadcast_to(zeta[:, :, None], (RET_H, chunk, RET_DK))
    cd = jnp.exp(log_g * chunk)
    half = RET_DK // 2
    inv = 1.0 / (ROPE_BASE ** (jnp.arange(half, dtype=F32) / half))
    ang = jnp.arange(seq, dtype=F32)[:, None] * inv[None, :]
    cos = jnp.cos(ang)
    sin = jnp.sin(ang)
    cos2 = jnp.concatenate([cos, cos], axis=-1)
    sin2 = jnp.concatenate([-sin, sin], axis=-1)
    return cd, cos2, sin2, decay, xi, zeta


def retention(z, gn_g, tables):
    b, s, _ = z.shape
    cd, cos2, sin2, decay, xi, zeta = tables
    L = decay.shape[1]
    qw = RET_H * RET_DK
    vw = RET_H * RET_DV
    return pl.pallas_call(
        _retention_kernel,
        grid=(b, s // L),
        in_specs=[pl.BlockSpec(memory_space=pltpu.SMEM),
                  pl.BlockSpec((1, L, qw), lambda bi, n: (bi, n, 0)),
                  pl.BlockSpec((1, L, qw), lambda bi, n: (bi, n, 1)),
                  pl.BlockSpec((1, L, vw), lambda bi, n: (bi, n, 1)),
                  pl.BlockSpec((1, L, vw), lambda bi, n: (bi, n, 2)),
                  pl.BlockSpec((L, RET_DK), lambda bi, n: (n, 0)),
                  pl.BlockSpec((L, RET_DK), lambda bi, n: (n, 0)),
                  pl.BlockSpec((RET_H, L, L), lambda bi, n: (0, 0, 0)),
                  pl.BlockSpec((RET_H, L, RET_DK), lambda bi, n: (0, 0, 0)),
                  pl.BlockSpec((RET_H, L, RET_DK), lambda bi, n: (0, 0, 0)),
                  pl.BlockSpec((1, vw), lambda bi, n: (0, 0))],
        out_specs=pl.BlockSpec((1, L, vw), lambda bi, n: (bi, n, 0)),
        out_shape=jax.ShapeDtypeStruct((b, s, vw), BF16),
        scratch_shapes=[pltpu.VMEM((RET_H, RET_DK, RET_DV), F32)],
        compiler_params=_params("parallel", "arbitrary"),
        name="retention",
    )(cd, z, z, z, z, cos2, sin2, decay, xi, zeta, gn_g)


def _fox_cumsum_kernel(f_ref, tri_ref, o_ref):
    s = f_ref.shape[1]
    tri = tri_ref[...]
    carry = jnp.zeros((1, LANES), F32)
    for i in range(s // CUM_BLK):
        f = f_ref[0, i * CUM_BLK:(i + 1) * CUM_BLK, :]
        ls = jnp.minimum(f, 0.0) - jnp.log1p(jnp.exp(-jnp.abs(f)))
        hi = ls.astype(BF16)
        r1 = ls - hi.astype(F32)
        mid = r1.astype(BF16)
        lo = (r1 - mid.astype(F32)).astype(BF16)
        c = (jnp.dot(tri, hi, preferred_element_type=F32)
             + jnp.dot(tri, mid, preferred_element_type=F32)
             + jnp.dot(tri, lo, preferred_element_type=F32)) + carry
        carry = c[CUM_BLK - 1:CUM_BLK, :]
        o_ref[0, :, i * CUM_BLK:(i + 1) * CUM_BLK] = (-c).T[:FOX_H, :]


def fox_cumsum(z, ff_block):
    b, s, _ = z.shape
    r = jnp.arange(CUM_BLK)
    tri = (r[:, None] >= r[None, :]).astype(BF16)
    return pl.pallas_call(
        _fox_cumsum_kernel,
        grid=(b,),
        in_specs=[pl.BlockSpec((1, s, LANES), lambda bi: (bi, 0, ff_block)),
                  pl.BlockSpec((CUM_BLK, CUM_BLK), lambda bi: (0, 0))],
        out_specs=pl.BlockSpec((1, FOX_H, s), lambda bi: (bi, 0, 0)),
        out_shape=jax.ShapeDtypeStruct((b, FOX_H, s), F32),
        compiler_params=_params("parallel"),
        name="fox_cumsum",
    )(z, tri)


def _fox_attn_kernel(q_ref, k_ref, v_ref, nc_ref, o_ref, kt_ref, vx_ref, qb_ref, m_ref, acc_ref):
    qi = pl.program_id(2)
    s_len = k_ref.shape[1]

    def head(g):
        return slice(g * FOX_DH, (g + 1) * FOX_DH)

    @pl.when(qi == 0)
    def _():
        for g in range(FOX_HG):
            for c in range(s_len // FOX_TK):
                rows = slice(c * FOX_TK, (c + 1) * FOX_TK)
                kt_ref[g, :, rows] = k_ref[0, rows, head(g)].T.astype(BF16)
                vx_ref[g, rows, :FOX_DH] = v_ref[0, rows, head(g)].astype(BF16)
            vx_ref[g, :, FOX_DH:] = jnp.ones((s_len, FOX_DH), BF16)

    for g in range(FOX_HG):
        qb_ref[g] = (q_ref[0, :, head(g)] * (FOX_DH ** -0.5 * LOG2E)).astype(BF16)
    m_ref[...] = jnp.full_like(m_ref, NEG_BIG)
    acc_ref[...] = jnp.zeros_like(acc_ref)

    def tile(j, diag):
        k0 = pl.multiple_of(j * FOX_TK, FOX_TK)
        rows = slice(0 if diag is None else diag * FOX_TK, FOX_TQ)
        for g in range(FOX_HG):
            s = jnp.dot(qb_ref[g, rows, :], kt_ref[g, :, pl.ds(k0, FOX_TK)],
                        preferred_element_type=F32)
            s = s + nc_ref[0, g, :, pl.ds(k0, FOX_TK)] * LOG2E
            if diag is not None:
                row = lax.broadcasted_iota(jnp.int32, s.shape, 0)
                col = lax.broadcasted_iota(jnp.int32, s.shape, 1)
                s = jnp.where(col <= row, s, NEG_BIG)
            m_prev = m_ref[g, rows, :]
            m_new = jnp.maximum(m_prev, jnp.max(s, axis=-1, keepdims=True))
            alpha = jnp.exp2(m_prev - m_new)
            p = jnp.concatenate(
                [jnp.exp2(s[:, c * LANES:(c + 1) * LANES] - m_new)
                 for c in range(FOX_TK // LANES)], axis=1).astype(BF16)
            pv = jnp.dot(p, vx_ref[g, pl.ds(k0, FOX_TK), :], preferred_element_type=F32)
            acc_ref[g, rows, :] = jnp.concatenate([alpha, alpha], axis=1) * acc_ref[g, rows, :] + pv
            m_ref[g, rows, :] = m_new

    def body(j, c):
        tile(j, None)
        return c

    per_q = FOX_TQ // FOX_TK
    n_full = qi * per_q
    lax.fori_loop(0, n_full, body, 0)
    for r in range(per_q):
        tile(n_full + r, r)
    for g in range(FOX_HG):
        o_ref[0, :, head(g)] = (acc_ref[g, :, :FOX_DH] / acc_ref[g, :, FOX_DH:]).astype(BF16)


def fox_attention(z, negc, q_blk, k_blk, v_blk):
    b, s, _ = z.shape
    assert FOX_TQ % FOX_TK == 0 and FOX_DH == LANES and FOX_H % FOX_HG == 0
    gw = FOX_HG * FOX_DH
    negc4 = negc.reshape(b, FOX_H, 1, s)
    qg, kg, vg = q_blk // FOX_HG, k_blk // FOX_HG, v_blk // FOX_HG
    return pl.pallas_call(
        _fox_attn_kernel,
        grid=(b, FOX_H // FOX_HG, s // FOX_TQ),
        in_specs=[pl.BlockSpec((1, FOX_TQ, gw), lambda bi, h, qi: (bi, qi, qg + h)),
                  pl.BlockSpec((1, s, gw), lambda bi, h, qi: (bi, 0, kg + h)),
                  pl.BlockSpec((1, s, gw), lambda bi, h, qi: (bi, 0, vg + h)),
                  pl.BlockSpec((1, FOX_HG, 1, s), lambda bi, h, qi: (bi, h, 0, 0))],
        out_specs=pl.BlockSpec((1, FOX_TQ, gw), lambda bi, h, qi: (bi, qi, h)),
        out_shape=jax.ShapeDtypeStruct((b, s, FOX_H * FOX_DH), BF16),
        scratch_shapes=[pltpu.VMEM((FOX_HG, FOX_DH, s), BF16),
                        pltpu.VMEM((FOX_HG, s, 2 * FOX_DH), BF16),
                        pltpu.VMEM((FOX_HG, FOX_TQ, FOX_DH), BF16),
                        pltpu.VMEM((FOX_HG, FOX_TQ, LANES), F32),
                        pltpu.VMEM((FOX_HG, FOX_TQ, 2 * FOX_DH), F32)],
        compiler_params=_params("parallel", "parallel", "arbitrary"),
        name="fox_attn",
    )(z, z, z, negc4)


def _conv_kernel(a_ref, g_ref, ap_ref, gp_ref, w_ref, cb_ref, lg_ref, lb_ref, o_ref, y_ref,
                 c_ref):
    ts = a_ref.shape[1]
    c_all = a_ref.shape[2]
    prev = ap_ref[0] * _sigmoid(gp_ref[0])
    y_ref[0:CONV_HALO, :] = jnp.where(pl.program_id(1) > 0, prev, 0.0)
    y_ref[CONV_HALO:CONV_HALO + ts, :] = a_ref[0] * _sigmoid(g_ref[0])
    lead = CONV_HALO - (CONV_W - 1)

    n_cb = c_all // CONV_CB
    win_rows = CONV_RB + CONV_HALO

    def block(idx, carry):
        r0 = pl.multiple_of((idx // n_cb) * CONV_RB, CONV_RB)
        c0 = pl.multiple_of((idx % n_cb) * CONV_CB, CONV_CB)
        win = y_ref[pl.ds(r0, win_rows), pl.ds(c0, CONV_CB)]
        acc = jnp.zeros((CONV_RB, CONV_CB), F32)
        for ph in range(SUBLANES):
            rot = win if ph == 0 else pltpu.roll(win, win_rows - ph, 0)
            for w in range(CONV_W):
                if (w + lead) % SUBLANES == ph:
                    a0 = (w + lead) - ph
                    acc = acc + w_ref[w:w + 1, pl.ds(c0, CONV_CB)] * rot[a0:a0 + CONV_RB]
        c_ref[pl.ds(r0, CONV_RB), pl.ds(c0, CONV_CB)] = acc + cb_ref[:, pl.ds(c0, CONV_CB)]
        return carry

    lax.fori_loop(0, (ts // CONV_RB) * n_cb, block, 0)
    y = _layernorm_rows(c_ref[...], lg_ref[...], lb_ref[...])
    o_ref[0] = (y * _sigmoid(y)).astype(BF16)


def conformer_conv(z, a_blk, g_blk, conv_w, conv_b, ln_g, ln_b):
    b, s, _ = z.shape
    c = conv_w.shape[1]
    ts = min(CONV_TS, s)
    hb = ts // CONV_HALO

    def halo(col):
        return lambda bi, i: (bi, jnp.maximum(i * hb - 1, 0), col)

    return pl.pallas_call(
        _conv_kernel,
        grid=(b, s // ts),
        in_specs=[pl.BlockSpec((1, ts, c), lambda bi, i: (bi, i, a_blk)),
                  pl.BlockSpec((1, ts, c), lambda bi, i: (bi, i, g_blk)),
                  pl.BlockSpec((1, CONV_HALO, c), halo(a_blk)),
                  pl.BlockSpec((1, CONV_HALO, c), halo(g_blk)),
                  pl.BlockSpec((CONV_W, c), lambda bi, i: (0, 0)),
                  pl.BlockSpec((1, c), lambda bi, i: (0, 0)),
                  pl.BlockSpec((1, c), lambda bi, i: (0, 0)),
                  pl.BlockSpec((1, c), lambda bi, i: (0, 0))],
        out_specs=pl.BlockSpec((1, ts, c), lambda bi, i: (bi, i, 0)),
        out_shape=jax.ShapeDtypeStruct((b, s, c), BF16),
        scratch_shapes=[pltpu.VMEM((CONV_HALO + ts, c), F32),
                        pltpu.VMEM((ts, c), F32)],
        compiler_params=_params("parallel", "parallel"),
        name="conv",
    )(z, z, z, z, conv_w, conv_b, ln_g, ln_b)


def _merge_kernel(a_ref, f_ref, c_ref, wa_ref, wf_ref, wc_ref, g0_ref, g1_ref, g2_ref, o_ref):
    ya = jnp.dot(a_ref[...], wa_ref[...], preferred_element_type=F32)
    m = _sigmoid(g0_ref[...]) * ya
    yf = jnp.dot(f_ref[...], wf_ref[...], preferred_element_type=F32)
    m = m + _sigmoid(g1_ref[...]) * yf
    yc = jnp.dot(c_ref[...], wc_ref[...], preferred_element_type=F32)
    m = m + _sigmoid(g2_ref[...]) * yc
    o_ref[...] = m.astype(BF16)


def merge(ret_o, fox_o, conv_o, w_ret, w_fox, w_conv, z2, gl_blk, tm=1024, tn=256):
    m, k = ret_o.shape
    n = w_ret.shape[1]
    tm = min(tm, m)
    nb = n // tn
    act = pl.BlockSpec((tm, k), lambda i, j: (i, 0))
    wsp = pl.BlockSpec((k, tn), lambda i, j: (0, j))

    def gate(br):
        return pl.BlockSpec((tm, tn), lambda i, j: (i, gl_blk * nb + br * nb + j))

    return pl.pallas_call(
        _merge_kernel,
        grid=(m // tm, nb),
        in_specs=[act, act, act, wsp, wsp, wsp, gate(0), gate(1), gate(2)],
        out_specs=pl.BlockSpec((tm, tn), lambda i, j: (i, j)),
        out_shape=jax.ShapeDtypeStruct((m, n), BF16),
        compiler_params=_params("parallel", "parallel"),
        name="merge",
    )(ret_o, fox_o, conv_o, w_ret, w_fox, w_conv, z2, z2, z2)


def _out_ln_kernel(x_ref, w_ref, h_ref, g_ref, b_ref, o_ref, ob_ref, op_ref):
    y = jnp.dot(x_ref[...], w_ref[...], preferred_element_type=F32)
    out = _layernorm_rows(ALPHA * h_ref[...] + y, g_ref[...], b_ref[...])
    o_ref[...] = out
    ob_ref[...] = out.astype(BF16)
    op_ref[...] = _pack_halves(out)


def out_ln(merged, w_out, h, g, b, tm=512):
    m, k = merged.shape
    n = w_out.shape[1]
    tm = min(tm, m)
    return pl.pallas_call(
        _out_ln_kernel,
        grid=(m // tm,),
        in_specs=[pl.BlockSpec((tm, k), lambda i: (i, 0)),
                  pl.BlockSpec((k, n), lambda i: (0, 0)),
                  pl.BlockSpec((tm, n), lambda i: (i, 0)),
                  pl.BlockSpec((1, n), lambda i: (0, 0)),
                  pl.BlockSpec((1, n), lambda i: (0, 0))],
        out_specs=[pl.BlockSpec((tm, n), lambda i: (i, 0)),
                   pl.BlockSpec((tm, n), lambda i: (i, 0)),
                   pl.BlockSpec((tm, n // 2), lambda i: (i, 0))],
        out_shape=[jax.ShapeDtypeStruct((m, n), F32), jax.ShapeDtypeStruct((m, n), BF16),
                   jax.ShapeDtypeStruct((m, n // 2), jnp.uint32)],
        compiler_params=_params("parallel"),
        name="out_ln",
    )(merged, w_out, h, g, b)


ROUTE_COLS = 8


def _router_kernel(h_ref, w_ref, b_ref, tri_ref, idx_ref, wt_ref, rank_ref, cnt_ref, carry_ref):
    @pl.when(pl.program_id(0) == 0)
    def _():
        carry_ref[...] = jnp.zeros_like(carry_ref)

    h = h_ref[...]
    hh = h.astype(BF16)
    hl = (h - hh.astype(F32)).astype(BF16)
    w = w_ref[...]
    wh = w.astype(BF16)
    wl = (w - wh.astype(F32)).astype(BF16)
    logits = (jnp.dot(hh, wh, preferred_element_type=F32)
              + jnp.dot(hl, wh, preferred_element_type=F32)
              + jnp.dot(hh, wl, preferred_element_type=F32))
    scores = _sigmoid(logits)
    tm, ne = scores.shape
    sel = scores + b_ref[...]
    lane = lax.broadcasted_iota(jnp.int32, (tm, ne), 1).astype(F32)
    mask = jnp.zeros((tm, ne), F32)
    onehots, idxs, wts = [], [], []
    for _ in range(TOP_K):
        mx = jnp.max(sel, axis=-1, keepdims=True)
        ik = jnp.min(jnp.where(sel == mx, lane, float(ne)), axis=-1, keepdims=True)
        oh = lane == ik
        wts.append(jnp.sum(jnp.where(oh, scores, 0.0), axis=-1, keepdims=True))
        idxs.append(ik)
        onehots.append(oh)
        sel = jnp.where(oh, -jnp.inf, sel)
        mask = jnp.where(oh, 1.0, mask)
    wsum = wts[0]
    for t in wts[1:]:
        wsum = wsum + t
    cnt = jnp.dot(tri_ref[...], mask.astype(BF16), preferred_element_type=F32) + carry_ref[...]
    col = lax.broadcasted_iota(jnp.int32, (tm, ROUTE_COLS), 1)
    idx_o = jnp.zeros((tm, ROUTE_COLS), F32)
    wt_o = jnp.zeros((tm, ROUTE_COLS), F32)
    rank_o = jnp.zeros((tm, ROUTE_COLS), F32)
    for kk in range(TOP_K):
        rk = jnp.sum(jnp.where(onehots[kk], cnt, 0.0), axis=-1, keepdims=True)
        idx_o = jnp.where(col == kk, idxs[kk], idx_o)
        wt_o = jnp.where(col == kk, wts[kk] / wsum * ROUTED_SCALE, wt_o)
        rank_o = jnp.where(col == kk, rk, rank_o)
    idx_ref[...] = idx_o.astype(jnp.int32)
    wt_ref[...] = wt_o
    rank_ref[...] = rank_o.astype(jnp.int32)
    carry_ref[...] = carry_ref[...] + jnp.sum(mask, axis=0, keepdims=True)
    cnt_ref[...] = carry_ref[...]


def router(h, w_router, b_router, tm=512):
    m, k = h.shape
    ne = w_router.shape[1]
    tm = min(tm, m)
    r = jnp.arange(tm)
    tri = (r[:, None] > r[None, :]).astype(BF16)
    small = pl.BlockSpec((tm, ROUTE_COLS), lambda i: (i, 0))
    return pl.pallas_call(
        _router_kernel,
        grid=(m // tm,),
        in_specs=[pl.BlockSpec((tm, k), lambda i: (i, 0)),
                  pl.BlockSpec((k, ne), lambda i: (0, 0)),
                  pl.BlockSpec((1, ne), lambda i: (0, 0)),
                  pl.BlockSpec((tm, tm), lambda i: (0, 0))],
        out_specs=[small, small, small, pl.BlockSpec((1, ne), lambda i: (0, 0))],
        out_shape=[jax.ShapeDtypeStruct((m, ROUTE_COLS), jnp.int32),
                   jax.ShapeDtypeStruct((m, ROUTE_COLS), F32),
                   jax.ShapeDtypeStruct((m, ROUTE_COLS), jnp.int32),
                   jax.ShapeDtypeStruct((1, ne), F32)],
        scratch_shapes=[pltpu.VMEM((1, ne), F32)],
        compiler_params=_params("arbitrary"),
        name="router",
    )(h, w_router, b_router, tri)


def _experts_kernel(be_ref, bv_ref, li_ref, x_ref, wg_ref, wu_ref, wd_ref, o_ref, wgb_ref,
                    wub_ref, wdb_ref):
    i = pl.program_id(0)

    @pl.when(jnp.logical_or(i == 0, be_ref[i] != be_ref[jnp.maximum(i - 1, 0)]))
    def _():
        wgb_ref[...] = wg_ref[0, 0].astype(BF16)
        wub_ref[...] = wu_ref[0, 0].astype(BF16)
        wdb_ref[...] = wd_ref[0, 0].astype(BF16)

    @pl.when(bv_ref[i] > 0)
    def _():
        half = x_ref.shape[1]
        xa, xb = _unpack_halves(x_ref[...])
        xa = xa.astype(BF16)
        xb = xb.astype(BF16)
        g = (jnp.dot(xa, wgb_ref[:half, :], preferred_element_type=F32)
             + jnp.dot(xb, wgb_ref[half:, :], preferred_element_type=F32))
        u = (jnp.dot(xa, wub_ref[:half, :], preferred_element_type=F32)
             + jnp.dot(xb, wub_ref[half:, :], preferred_element_type=F32))
        a = (g * _sigmoid(g) * u).astype(BF16)
        o_ref[...] = _pack_halves(jnp.dot(a, wdb_ref[...], preferred_element_type=F32))

    @pl.when(bv_ref[i] == 0)
    def _():
        o_ref[...] = jnp.zeros_like(o_ref)


def experts(xs, block_e, block_valid, layer, w_gate, w_up, w_down):
    r, half = xs.shape
    d = 2 * half
    ff = w_gate.shape[3]
    nblk = r // MOE_BLOCK
    return pl.pallas_call(
        _experts_kernel,
        grid_spec=pltpu.PrefetchScalarGridSpec(
            num_scalar_prefetch=3,
            grid=(nblk,),
            in_specs=[pl.BlockSpec((MOE_BLOCK, half), lambda i, be, bv, li: (i, 0)),
                      pl.BlockSpec((1, 1, d, ff), lambda i, be, bv, li: (li[0], be[i], 0, 0)),
                      pl.BlockSpec((1, 1, d, ff), lambda i, be, bv, li: (li[0], be[i], 0, 0)),
                      pl.BlockSpec((1, 1, ff, d), lambda i, be, bv, li: (li[0], be[i], 0, 0))],
            out_specs=pl.BlockSpec((MOE_BLOCK, half), lambda i, be, bv, li: (i, 0)),
            scratch_shapes=[pltpu.VMEM((d, ff), BF16), pltpu.VMEM((d, ff), BF16),
                            pltpu.VMEM((ff, d), BF16)]),
        out_shape=jax.ShapeDtypeStruct((r, half), jnp.uint32),
        compiler_params=_params("arbitrary"),
        name="experts",
    )(block_e, block_valid, layer, xs, w_gate, w_up, w_down)


def _shared_ln_kernel(xb_ref, h_ref, yg_ref, wt_ref, wgu_ref, wd_ref, g_ref, b_ref, o_ref,
                      ob_ref):
    ff = wd_ref.shape[0]
    gu = jnp.dot(xb_ref[...], wgu_ref[...], preferred_element_type=F32)
    g = gu[:, :ff]
    a = (g * _sigmoid(g) * gu[:, ff:]).astype(BF16)
    shared = jnp.dot(a, wd_ref[...], preferred_element_type=F32)
    wt = wt_ref[...]
    r_lo = r_hi = None
    for kk in range(TOP_K):
        lo, hi = _unpack_halves(yg_ref[kk])
        wk = wt[:, kk:kk + 1]
        r_lo = wk * lo if r_lo is None else r_lo + wk * lo
        r_hi = wk * hi if r_hi is None else r_hi + wk * hi
    routed = jnp.concatenate([r_lo, r_hi], axis=1)
    out = _layernorm_rows(ALPHA * h_ref[...] + (routed + shared), g_ref[...], b_ref[...])
    o_ref[...] = out
    ob_ref[...] = out.astype(BF16)


def shared_ln(h_bf, h, yg, wt, w_gu, w_down, g, b, tm=256):
    m, d = h.shape
    ff2 = w_gu.shape[1]
    tm = min(tm, m)
    row = pl.BlockSpec((tm, d), lambda i: (i, 0))
    vec = pl.BlockSpec((1, d), lambda i: (0, 0))
    return pl.pallas_call(
        _shared_ln_kernel,
        grid=(m // tm,),
        in_specs=[row, row,
                  pl.BlockSpec((TOP_K, tm, d // 2), lambda i: (0, i, 0)),
                  pl.BlockSpec((tm, ROUTE_COLS), lambda i: (i, 0)),
                  pl.BlockSpec((d, ff2), lambda i: (0, 0)),
                  pl.BlockSpec((ff2 // 2, d), lambda i: (0, 0)),
                  vec, vec],
        out_specs=[row, row],
        out_shape=[jax.ShapeDtypeStruct((m, d), F32), jax.ShapeDtypeStruct((m, d), BF16)],
        compiler_params=_params("parallel"),
        name="shared_ln",
    )(h_bf, h, yg, wt, w_gu, w_down, g, b)


def _mixer_layout(d):
    rq = RET_H * RET_DK
    rv = RET_H * RET_DV
    fq = FOX_H * FOX_DH
    main = 2 * rq + 2 * rv + 3 * fq
    ff0 = main
    cu0 = ff0 + FOX_H
    gl0 = cu0 + 2 * d
    end = gl0 + N_BRANCH * d
    return dict(rq=rq, rv=rv, fq=fq, main=main, ff0=ff0, cu0=cu0, gl0=gl0, end=end)


def _prep_in_weights(w_in, b_in, d, tn):
    lay = _mixer_layout(d)
    body = lay["main"] + 2 * d + N_BRANCH * d
    pad = tn - FOX_H
    w = jnp.concatenate([w_in[:, :lay["main"]], w_in[:, lay["cu0"]:lay["end"]],
                         w_in[:, lay["ff0"]:lay["cu0"]],
                         jnp.zeros((w_in.shape[0], pad), w_in.dtype)], axis=1).astype(BF16)
    bb = jnp.concatenate([b_in[:lay["main"]], b_in[lay["cu0"]:lay["end"]],
                          b_in[lay["ff0"]:lay["cu0"]], jnp.zeros((pad,), b_in.dtype)])
    return w, bb.reshape(1, -1).astype(F32), body


def _moe_sublayer(h, h_bf, h_pk, w_router, b_router, layer, w_gate, w_up, w_down, w_sh_gu,
                  w_sh_down, g, b):
    t, d = h.shape
    ne = w_router.shape[1]
    idx8, wt8, rank8, cnt = router(h, w_router, b_router.reshape(1, ne))
    idx = idx8[:, :TOP_K]
    rank = rank8[:, :TOP_K]
    sizes = cnt[0].astype(jnp.int32)
    a = t * TOP_K
    n_blocks = -(-a // MOE_BLOCK) + ne
    r = n_blocks * MOE_BLOCK
    padded = (sizes + MOE_BLOCK - 1) // MOE_BLOCK * MOE_BLOCK
    pad_end = jnp.cumsum(padded)
    pad_start = pad_end - padded
    dest = pad_start[idx] + rank
    tok = jnp.broadcast_to(jnp.arange(t, dtype=jnp.int32)[:, None], (t, TOP_K))
    row_tok = (jnp.arange(r, dtype=jnp.int32) % t).at[dest.reshape(a)].set(
        tok.reshape(a), unique_indices=True, mode="promise_in_bounds")
    blk0 = jnp.arange(n_blocks, dtype=jnp.int32) * MOE_BLOCK
    block_e = jnp.minimum(jnp.sum((pad_end[None, :] <= blk0[:, None]).astype(jnp.int32), axis=1),
                          ne - 1)
    block_valid = (blk0 < pad_end[-1]).astype(jnp.int32)
    xs = h_pk[row_tok]
    ys = experts(xs, block_e, block_valid, layer, w_gate, w_up, w_down)
    yg = ys[dest.T.reshape(a)].reshape(TOP_K, t, d // 2)
    return shared_ln(h_bf, h, yg, wt8, w_sh_gu, w_sh_down, g, b)


def kernel(x, w_in, b_in, ret_gn_g, conv_w, conv_b, conv_ln_g, conv_ln_b, w_ret_o, w_fox_o,
           w_conv_o, w_out, ln1_g, ln1_b, w_router, b_router, w_exp_gate, w_exp_up, w_exp_down,
           w_sh_gate, w_sh_up, w_sh_down, ln2_g, ln2_b):
    bsz, seq, d = x.shape
    t = bsz * seq
    in_tn = IN_PAD
    lay = _mixer_layout(d)
    chunk = min(RET_CHUNK, seq)
    tables = _retention_tables(seq, chunk)
    fq0 = 2 * lay["rq"] + 2 * lay["rv"]

    def row(v):
        return v.reshape(1, -1)

    def layer(carry, p):
        h, h_bf = carry
        w_cat, b_cat, body = _prep_in_weights(p["w_in"], p["b_in"], d, in_tn)
        z2 = in_proj(h_bf, w_cat, b_cat, tn=IN_TN)
        z = z2.reshape(bsz, seq, -1)
        ret_o = retention(z, row(p["ret_gn_g"]), tables)
        negc = fox_cumsum(z, body // LANES)
        fox_o = fox_attention(z, negc, fq0 // FOX_DH, (fq0 + lay["fq"]) // FOX_DH,
                              (fq0 + 2 * lay["fq"]) // FOX_DH)
        conv_o = conformer_conv(z, lay["main"] // d, lay["main"] // d + 1, p["conv_w"],
                                row(p["conv_b"]), row(p["conv_ln_g"]), row(p["conv_ln_b"]))
        merged = merge(ret_o.reshape(t, -1), fox_o.reshape(t, -1), conv_o.reshape(t, -1),
                       p["w_ret_o"].astype(BF16), p["w_fox_o"].astype(BF16),
                       p["w_conv_o"].astype(BF16), z2, (lay["main"] + 2 * d) // d)
        h, h_bf, h_pk = out_ln(merged, p["w_out"].astype(BF16), h, row(p["ln1_g"]),
                               row(p["ln1_b"]))
        w_sh_gu = jnp.concatenate([p["w_sh_gate"], p["w_sh_up"]], axis=-1).astype(BF16)
        h, h_bf = _moe_sublayer(h, h_bf, h_pk, p["w_router"], p["b_router"], p["layer"],
                                w_exp_gate, w_exp_up, w_exp_down, w_sh_gu,
                                p["w_sh_down"].astype(BF16), row(p["ln2_g"]), row(p["ln2_b"]))
        return (h, h_bf), None

    depth = w_in.shape[0]
    params = dict(w_in=w_in, b_in=b_in, ret_gn_g=ret_gn_g, conv_w=conv_w, conv_b=conv_b,
                  conv_ln_g=conv_ln_g, conv_ln_b=conv_ln_b, w_ret_o=w_ret_o, w_fox_o=w_fox_o,
                  w_conv_o=w_conv_o, w_out=w_out, ln1_g=ln1_g, ln1_b=ln1_b, w_router=w_router,
                  b_router=b_router, w_sh_gate=w_sh_gate, w_sh_up=w_sh_up,
                  w_sh_down=w_sh_down, ln2_g=ln2_g, ln2_b=ln2_b,
                  layer=jnp.arange(depth, dtype=jnp.int32).reshape(depth, 1))
    h0 = x.reshape(t, d)
    (h, _), _ = lax.scan(layer, (h0, h0.astype(BF16)), params)
    return h.reshape(bsz, seq, d)
```

```python
import functools

import jax
import jax.numpy as jnp
from jax import lax
from jax.experimental import pallas as pl
from jax.experimental.pallas import tpu as pltpu

F32 = jnp.float32
BF16 = jnp.bfloat16

RET_H = 8
RET_DK = 128
RET_DV = 256
ROPE_BASE = 10000.0
FOX_H = 16
FOX_DH = 128
CONV_W = 31
N_BRANCH = 3
TOP_K = 6
ROUTED_SCALE = 2.5
DEPTH_FOR_NORM = 4
ALPHA = (2 * DEPTH_FOR_NORM) ** 0.25
LN_EPS = 1e-5

LANES = 128
SUBLANES = 8
VMEM_LIMIT = 56 * 1024 * 1024

BATCH_GROUPS = 2
IN_PAD = 512
IN_TN = 1536
RET_CHUNK = 256
FOX_TQ = 2048
FOX_TK = 512
FOX_HG = 1
CUM_BLK = 256
CONV_TS = 256
CONV_HALO = 32
CONV_RB = 64
CONV_CB = 256
MOE_BLOCK = 512
NEG_BIG = -1e30
LOG2E = 1.4426950408889634


def _params(*sem):
    return pltpu.CompilerParams(dimension_semantics=sem, vmem_limit_bytes=VMEM_LIMIT)


def _sigmoid(x):
    return 1.0 / (1.0 + jnp.exp(-x))


def _pack_halves(x):
    c = x.shape[1] // 2
    xb = x.astype(BF16).astype(F32)
    lo = lax.bitcast_convert_type(xb[:, :c], jnp.uint32) >> 16
    hi = lax.bitcast_convert_type(xb[:, c:], jnp.uint32) & jnp.uint32(0xFFFF0000)
    return lo | hi


def _unpack_halves(u):
    lo = lax.bitcast_convert_type(u << 16, F32)
    hi = lax.bitcast_convert_type(u & jnp.uint32(0xFFFF0000), F32)
    return lo, hi


def _layernorm_rows(x, g, b):
    mu = jnp.mean(x, axis=-1, keepdims=True)
    xc = x - mu
    var = jnp.mean(xc * xc, axis=-1, keepdims=True)
    return xc * lax.rsqrt(var + LN_EPS) * g + b


def _in_proj_kernel(x_ref, w_ref, b_ref, o_ref):
    acc = jnp.dot(x_ref[...], w_ref[...], preferred_element_type=F32)
    o_ref[...] = acc + b_ref[...]


def in_proj(x_bf, w_bf, b, tm=1024, tn=512):
    m, k = x_bf.shape
    n = w_bf.shape[1]
    tm = min(tm, m)
    return pl.pallas_call(
        _in_proj_kernel,
        grid=(m // tm, n // tn),
        in_specs=[pl.BlockSpec((tm, k), lambda i, j: (i, 0)),
                  pl.BlockSpec((k, tn), lambda i, j: (0, j)),
                  pl.BlockSpec((1, tn), lambda i, j: (0, j))],
        out_specs=pl.BlockSpec((tm, tn), lambda i, j: (i, j)),
        out_shape=jax.ShapeDtypeStruct((m, n), F32),
        compiler_params=_params("parallel", "parallel"),
        name="in_proj",
    )(x_bf, w_bf, b)


def _retention_kernel(cd_ref, q_ref, k_ref, v_ref, g_ref, cos_ref, sin_ref, decay_ref,
                      xi_ref, zeta_ref, gn_ref, o_ref, state_ref):
    @pl.when(pl.program_id(1) == 0)
    def _():
        state_ref[...] = jnp.zeros_like(state_ref)

    cos = cos_ref[...]
    sin = sin_ref[...]
    for h in range(RET_H):
        qs = slice(h * RET_DK, (h + 1) * RET_DK)
        vs = slice(h * RET_DV, (h + 1) * RET_DV)
        q = q_ref[0, :, qs]
        k = k_ref[0, :, qs]
        qr = q * cos + pltpu.roll(q, RET_DK // 2, 1) * sin
        kr = (k * cos + pltpu.roll(k, RET_DK // 2, 1) * sin) * (RET_DK ** -0.5)
        vb = v_ref[0, :, vs].astype(BF16)
        s = lax.dot_general(qr.astype(BF16), kr.astype(BF16), (((1,), (1,)), ((), ())),
                            preferred_element_type=F32)
        inner = (s * decay_ref[h]).astype(BF16)
        o = jnp.dot(inner, vb, preferred_element_type=F32)
        st = state_ref[h]
        o = o + jnp.dot((qr * xi_ref[h]).astype(BF16), st.astype(BF16),
                        preferred_element_type=F32)
        kz = (kr * zeta_ref[h]).astype(BF16)
        state_ref[h] = st * cd_ref[h] + lax.dot_general(
            kz, vb, (((0,), (0,)), ((), ())), preferred_element_type=F32)
        mu = jnp.mean(o, axis=-1, keepdims=True)
        oc = o - mu
        var = jnp.mean(oc * oc, axis=-1, keepdims=True)
        on = oc * lax.rsqrt(var + LN_EPS) * gn_ref[:, vs]
        g = g_ref[0, :, vs]
        o_ref[0, :, vs] = (g * _sigmoid(g) * on).astype(BF16)


def _retention_tables(seq, chunk):
    h = jnp.arange(RET_H, dtype=F32)
    log_g = jnp.log1p(-jnp.exp2(-5.0 - h))
    i = jnp.arange(chunk, dtype=F32)
    diff = i[:, None] - i[None, :]
    decay = jnp.where(diff >= 0, jnp.exp(log_g[:, None, None] * jnp.maximum(diff, 0.0)), 0.0)
    xi = jnp.exp(log_g[:, None] * (i + 1.0))
    zeta = jnp.exp(log_g[:, None] * (chunk - 1.0 - i))
    xi = jnp.broadcast_to(xi[:, :, None], (RET_H, chunk, RET_DK))
    zeta = jnp.broadcast_to(zeta[:, :, None], (RET_H, chunk, RET_DK))
    cd = jnp.exp(log_g * chunk)
    half = RET_DK // 2
    inv = 1.0 / (ROPE_BASE ** (jnp.arange(half, dtype=F32) / half))
    ang = jnp.arange(seq, dtype=F32)[:, None] * inv[None, :]
    cos = jnp.cos(ang)
    sin = jnp.sin(ang)
    cos2 = jnp.concatenate([cos, cos], axis=-1)
    sin2 = jnp.concatenate([-sin, sin], axis=-1)
    return cd, cos2, sin2, decay, xi, zeta


def retention(z, gn_g, tables):
    b, s, _ = z.shape
    cd, cos2, sin2, decay, xi, zeta = tables
    L = decay.shape[1]
    qw = RET_H * RET_DK
    vw = RET_H * RET_DV
    return pl.pallas_call(
        _retention_kernel,
        grid=(b, s // L),
        in_specs=[pl.BlockSpec(memory_space=pltpu.SMEM),
                  pl.BlockSpec((1, L, qw), lambda bi, n: (bi, n, 0)),
                  pl.BlockSpec((1, L, qw), lambda bi, n: (bi, n, 1)),
                  pl.BlockSpec((1, L, vw), lambda bi, n: (bi, n, 1)),
                  pl.BlockSpec((1, L, vw), lambda bi, n: (bi, n, 2)),
                  pl.BlockSpec((L, RET_DK), lambda bi, n: (n, 0)),
                  pl.BlockSpec((L, RET_DK), lambda bi, n: (n, 0)),
                  pl.BlockSpec((RET_H, L, L), lambda bi, n: (0, 0, 0)),
                  pl.BlockSpec((RET_H, L, RET_DK), lambda bi, n: (0, 0, 0)),
                  pl.BlockSpec((RET_H, L, RET_DK), lambda bi, n: (0, 0, 0)),
                  pl.BlockSpec((1, vw), lambda bi, n: (0, 0))],
        out_specs=pl.BlockSpec((1, L, vw), lambda bi, n: (bi, n, 0)),
        out_shape=jax.ShapeDtypeStruct((b, s, vw), BF16),
        scratch_shapes=[pltpu.VMEM((RET_H, RET_DK, RET_DV), F32)],
        compiler_params=_params("parallel", "arbitrary"),
        name="retention",
    )(cd, z, z, z, z, cos2, sin2, decay, xi, zeta, gn_g)


def _fox_cumsum_kernel(f_ref, tri_ref, o_ref):
    s = f_ref.shape[1]
    tri = tri_ref[...]
    carry = jnp.zeros((1, LANES), F32)
    for i in range(s // CUM_BLK):
        f = f_ref[0, i * CUM_BLK:(i + 1) * CUM_BLK, :]
        ls = jnp.minimum(f, 0.0) - jnp.log1p(jnp.exp(-jnp.abs(f)))
        hi = ls.astype(BF16)
        r1 = ls - hi.astype(F32)
        mid = r1.astype(BF16)
        lo = (r1 - mid.astype(F32)).astype(BF16)
        c = (jnp.dot(tri, hi, preferred_element_type=F32)
             + jnp.dot(tri, mid, preferred_element_type=F32)
             + jnp.dot(tri, lo, preferred_element_type=F32)) + carry
        carry = c[CUM_BLK - 1:CUM_BLK, :]
        o_ref[0, :, i * CUM_BLK:(i + 1) * CUM_BLK] = (-c).T[:FOX_H, :]


def fox_cumsum(z, ff_block):
    b, s, _ = z.shape
    r = jnp.arange(CUM_BLK)
    tri = (r[:, None] >= r[None, :]).astype(BF16)
    return pl.pallas_call(
        _fox_cumsum_kernel,
        grid=(b,),
        in_specs=[pl.BlockSpec((1, s, LANES), lambda bi: (bi, 0, ff_block)),
                  pl.BlockSpec((CUM_BLK, CUM_BLK), lambda bi: (0, 0))],
        out_specs=pl.BlockSpec((1, FOX_H, s), lambda bi: (bi, 0, 0)),
        out_shape=jax.ShapeDtypeStruct((b, FOX_H, s), F32),
        compiler_params=_params("parallel"),
        name="fox_cumsum",
    )(z, tri)


def _fox_attn_kernel(q_ref, k_ref, v_ref, nc_ref, o_ref, kt_ref, vx_ref, qb_ref, m_ref, acc_ref):
    qi = pl.program_id(2)
    s_len = k_ref.shape[1]

    def head(g):
        return slice(g * FOX_DH, (g + 1) * FOX_DH)

    @pl.when(qi == 0)
    def _():
        for g in range(FOX_HG):
            for c in range(s_len // FOX_TK):
                rows = slice(c * FOX_TK, (c + 1) * FOX_TK)
                kt_ref[g, :, rows] = k_ref[0, rows, head(g)].T.astype(BF16)
                vx_ref[g, rows, :FOX_DH] = v_ref[0, rows, head(g)].astype(BF16)
            vx_ref[g, :, FOX_DH:] = jnp.ones((s_len, FOX_DH), BF16)

    for g in range(FOX_HG):
        qb_ref[g] = (q_ref[0, :, head(g)] * (FOX_DH ** -0.5 * LOG2E)).astype(BF16)
    m_ref[...] = jnp.full_like(m_ref, NEG_BIG)
    acc_ref[...] = jnp.zeros_like(acc_ref)

    def tile(j, diag):
        k0 = pl.multiple_of(j * FOX_TK, FOX_TK)
        rows = slice(0 if diag is None else diag * FOX_TK, FOX_TQ)
        for g in range(FOX_HG):
            s = jnp.dot(qb_ref[g, rows, :], kt_ref[g, :, pl.ds(k0, FOX_TK)],
                        preferred_element_type=F32)
            s = s + nc_ref[0, g, :, pl.ds(k0, FOX_TK)] * LOG2E
            if diag is not None:
                row = lax.broadcasted_iota(jnp.int32, s.shape, 0)
                col = lax.broadcasted_iota(jnp.int32, s.shape, 1)
                s = jnp.where(col <= row, s, NEG_BIG)
            m_prev = m_ref[g, rows, :]
            m_new = jnp.maximum(m_prev, jnp.max(s, axis=-1, keepdims=True))
            alpha = jnp.exp2(m_prev - m_new)
            p = jnp.concatenate(
                [jnp.exp2(s[:, c * LANES:(c + 1) * LANES] - m_new)
                 for c in range(FOX_TK // LANES)], axis=1).astype(BF16)
            pv = jnp.dot(p, vx_ref[g, pl.ds(k0, FOX_TK), :], preferred_element_type=F32)
            acc_ref[g, rows, :] = jnp.concatenate([alpha, alpha], axis=1) * acc_ref[g, rows, :] + pv
            m_ref[g, rows, :] = m_new

    def body(j, c):
        tile(j, None)
        return c

    per_q = FOX_TQ // FOX_TK
    n_full = qi * per_q
    lax.fori_loop(0, n_full, body, 0)
    for r in range(per_q):
        tile(n_full + r, r)
    for g in range(FOX_HG):
        o_ref[0, :, head(g)] = (acc_ref[g, :, :FOX_DH] / acc_ref[g, :, FOX_DH:]).astype(BF16)


def fox_attention(z, negc, q_blk, k_blk, v_blk):
    b, s, _ = z.shape
    assert FOX_TQ % FOX_TK == 0 and FOX_DH == LANES and FOX_H % FOX_HG == 0
    gw = FOX_HG * FOX_DH
    negc4 = negc.reshape(b, FOX_H, 1, s)
    qg, kg, vg = q_blk // FOX_HG, k_blk // FOX_HG, v_blk // FOX_HG
    return pl.pallas_call(
        _fox_attn_kernel,
        grid=(b, FOX_H // FOX_HG, s // FOX_TQ),
        in_specs=[pl.BlockSpec((1, FOX_TQ, gw), lambda bi, h, qi: (bi, qi, qg + h)),
                  pl.BlockSpec((1, s, gw), lambda bi, h, qi: (bi, 0, kg + h)),
                  pl.BlockSpec((1, s, gw), lambda bi, h, qi: (bi, 0, vg + h)),
                  pl.BlockSpec((1, FOX_HG, 1, s), lambda bi, h, qi: (bi, h, 0, 0))],
        out_specs=pl.BlockSpec((1, FOX_TQ, gw), lambda bi, h, qi: (bi, qi, h)),
        out_shape=jax.ShapeDtypeStruct((b, s, FOX_H * FOX_DH), BF16),
        scratch_shapes=[pltpu.VMEM((FOX_HG, FOX_DH, s), BF16),
                        pltpu.VMEM((FOX_HG, s, 2 * FOX_DH), BF16),
                        pltpu.VMEM((FOX_HG, FOX_TQ, FOX_DH), BF16),
                        pltpu.VMEM((FOX_HG, FOX_TQ, LANES), F32),
                        pltpu.VMEM((FOX_HG, FOX_TQ, 2 * FOX_DH), F32)],
        compiler_params=_params("parallel", "parallel", "arbitrary"),
        name="fox_attn",
    )(z, z, z, negc4)


def _conv_kernel(a_ref, g_ref, ap_ref, gp_ref, w_ref, cb_ref, lg_ref, lb_ref, o_ref, y_ref,
                 c_ref):
    ts = a_ref.shape[1]
    c_all = a_ref.shape[2]
    prev = ap_ref[0] * _sigmoid(gp_ref[0])
    y_ref[0:CONV_HALO, :] = jnp.where(pl.program_id(1) > 0, prev, 0.0)
    y_ref[CONV_HALO:CONV_HALO + ts, :] = a_ref[0] * _sigmoid(g_ref[0])
    lead = CONV_HALO - (CONV_W - 1)

    n_cb = c_all // CONV_CB
    win_rows = CONV_RB + CONV_HALO

    def block(idx, carry):
        r0 = pl.multiple_of((idx // n_cb) * CONV_RB, CONV_RB)
        c0 = pl.multiple_of((idx % n_cb) * CONV_CB, CONV_CB)
        win = y_ref[pl.ds(r0, win_rows), pl.ds(c0, CONV_CB)]
        acc = jnp.zeros((CONV_RB, CONV_CB), F32)
        for ph in range(SUBLANES):
            rot = win if ph == 0 else pltpu.roll(win, win_rows - ph, 0)
            for w in range(CONV_W):
                if (w + lead) % SUBLANES == ph:
                    a0 = (w + lead) - ph
                    acc = acc + w_ref[w:w + 1, pl.ds(c0, CONV_CB)] * rot[a0:a0 + CONV_RB]
        c_ref[pl.ds(r0, CONV_RB), pl.ds(c0, CONV_CB)] = acc + cb_ref[:, pl.ds(c0, CONV_CB)]
        return carry

    lax.fori_loop(0, (ts // CONV_RB) * n_cb, block, 0)
    y = _layernorm_rows(c_ref[...], lg_ref[...], lb_ref[...])
    o_ref[0] = (y * _sigmoid(y)).astype(BF16)


def conformer_conv(z, a_blk, g_blk, conv_w, conv_b, ln_g, ln_b):
    b, s, _ = z.shape
    c = conv_w.shape[1]
    ts = min(CONV_TS, s)
    hb = ts // CONV_HALO

    def halo(col):
        return lambda bi, i: (bi, jnp.maximum(i * hb - 1, 0), col)

    return pl.pallas_call(
        _conv_kernel,
        grid=(b, s // ts),
        in_specs=[pl.BlockSpec((1, ts, c), lambda bi, i: (bi, i, a_blk)),
                  pl.BlockSpec((1, ts, c), lambda bi, i: (bi, i, g_blk)),
                  pl.BlockSpec((1, CONV_HALO, c), halo(a_blk)),
                  pl.BlockSpec((1, CONV_HALO, c), halo(g_blk)),
                  pl.BlockSpec((CONV_W, c), lambda bi, i: (0, 0)),
                  pl.BlockSpec((1, c), lambda bi, i: (0, 0)),
                  pl.BlockSpec((1, c), lambda bi, i: (0, 0)),
                  pl.BlockSpec((1, c), lambda bi, i: (0, 0))],
        out_specs=pl.BlockSpec((1, ts, c), lambda bi, i: (bi, i, 0)),
        out_shape=jax.ShapeDtypeStruct((b, s, c), BF16),
        scratch_shapes=[pltpu.VMEM((CONV_HALO + ts, c), F32),
                        pltpu.VMEM((ts, c), F32)],
        compiler_params=_params("parallel", "parallel"),
        name="conv",
    )(z, z, z, z, conv_w, conv_b, ln_g, ln_b)


def _merge_kernel(a_ref, f_ref, c_ref, wa_ref, wf_ref, wc_ref, g0_ref, g1_ref, g2_ref, o_ref):
    ya = jnp.dot(a_ref[...], wa_ref[...], preferred_element_type=F32)
    m = _sigmoid(g0_ref[...]) * ya
    yf = jnp.dot(f_ref[...], wf_ref[...], preferred_element_type=F32)
    m = m + _sigmoid(g1_ref[...]) * yf
    yc = jnp.dot(c_ref[...], wc_ref[...], preferred_element_type=F32)
    m = m + _sigmoid(g2_ref[...]) * yc
    o_ref[...] = m.astype(BF16)


def merge(ret_o, fox_o, conv_o, w_ret, w_fox, w_conv, z2, gl_blk, tm=1024, tn=256):
    m, k = ret_o.shape
    n = w_ret.shape[1]
    tm = min(tm, m)
    nb = n // tn
    act = pl.BlockSpec((tm, k), lambda i, j: (i, 0))
    wsp = pl.BlockSpec((k, tn), lambda i, j: (0, j))

    def gate(br):
        return pl.BlockSpec((tm, tn), lambda i, j: (i, gl_blk * nb + br * nb + j))

    return pl.pallas_call(
        _merge_kernel,
        grid=(m // tm, nb),
        in_specs=[act, act, act, wsp, wsp, wsp, gate(0), gate(1), gate(2)],
        out_specs=pl.BlockSpec((tm, tn), lambda i, j: (i, j)),
        out_shape=jax.ShapeDtypeStruct((m, n), BF16),
        compiler_params=_params("parallel", "parallel"),
        name="merge",
    )(ret_o, fox_o, conv_o, w_ret, w_fox, w_conv, z2, z2, z2)


def _out_ln_kernel(x_ref, w_ref, h_ref, g_ref, b_ref, o_ref, ob_ref, op_ref):
    y = jnp.dot(x_ref[...], w_ref[...], preferred_element_type=F32)
    out = _layernorm_rows(ALPHA * h_ref[...] + y, g_ref[...], b_ref[...])
    o_ref[...] = out
    ob_ref[...] = out.astype(BF16)
    op_ref[...] = _pack_halves(out)


def out_ln(merged, w_out, h, g, b, tm=512):
    m, k = merged.shape
    n = w_out.shape[1]
    tm = min(tm, m)
    return pl.pallas_call(
        _out_ln_kernel,
        grid=(m // tm,),
        in_specs=[pl.BlockSpec((tm, k), lambda i: (i, 0)),
                  pl.BlockSpec((k, n), lambda i: (0, 0)),
                  pl.BlockSpec((tm, n), lambda i: (i, 0)),
                  pl.BlockSpec((1, n), lambda i: (0, 0)),
                  pl.BlockSpec((1, n), lambda i: (0, 0))],
        out_specs=[pl.BlockSpec((tm, n), lambda i: (i, 0)),
                   pl.BlockSpec((tm, n), lambda i: (i, 0)),
                   pl.BlockSpec((tm, n // 2), lambda i: (i, 0))],
        out_shape=[jax.ShapeDtypeStruct((m, n), F32), jax.ShapeDtypeStruct((m, n), BF16),
                   jax.ShapeDtypeStruct((m, n // 2), jnp.uint32)],
        compiler_params=_params("parallel"),
        name="out_ln",
    )(merged, w_out, h, g, b)


ROUTE_COLS = 8


def _router_kernel(h_ref, w_ref, b_ref, tri_ref, idx_ref, wt_ref, rank_ref, cnt_ref, carry_ref):
    @pl.when(pl.program_id(0) == 0)
    def _():
        carry_ref[...] = jnp.zeros_like(carry_ref)

    h = h_ref[...]
    hh = h.astype(BF16)
    hl = (h - hh.astype(F32)).astype(BF16)
    w = w_ref[...]
    wh = w.astype(BF16)
    wl = (w - wh.astype(F32)).astype(BF16)
    logits = (jnp.dot(hh, wh, preferred_element_type=F32)
              + jnp.dot(hl, wh, preferred_element_type=F32)
              + jnp.dot(hh, wl, preferred_element_type=F32))
    scores = _sigmoid(logits)
    tm, ne = scores.shape
    sel = scores + b_ref[...]
    lane = lax.broadcasted_iota(jnp.int32, (tm, ne), 1).astype(F32)
    mask = jnp.zeros((tm, ne), F32)
    onehots, idxs, wts = [], [], []
    for _ in range(TOP_K):
        mx = jnp.max(sel, axis=-1, keepdims=True)
        ik = jnp.min(jnp.where(sel == mx, lane, float(ne)), axis=-1, keepdims=True)
        oh = lane == ik
        wts.append(jnp.sum(jnp.where(oh, scores, 0.0), axis=-1, keepdims=True))
        idxs.append(ik)
        onehots.append(oh)
        sel = jnp.where(oh, -jnp.inf, sel)
        mask = jnp.where(oh, 1.0, mask)
    wsum = wts[0]
    for t in wts[1:]:
        wsum = wsum + t
    cnt = jnp.dot(tri_ref[...], mask.astype(BF16), preferred_element_type=F32) + carry_ref[...]
    col = lax.broadcasted_iota(jnp.int32, (tm, ROUTE_COLS), 1)
    idx_o = jnp.zeros((tm, ROUTE_COLS), F32)
    wt_o = jnp.zeros((tm, ROUTE_COLS), F32)
    rank_o = jnp.zeros((tm, ROUTE_COLS), F32)
    for kk in range(TOP_K):
        rk = jnp.sum(jnp.where(onehots[kk], cnt, 0.0), axis=-1, keepdims=True)
        idx_o = jnp.where(col == kk, idxs[kk], idx_o)
        wt_o = jnp.where(col == kk, wts[kk] / wsum * ROUTED_SCALE, wt_o)
        rank_o = jnp.where(col == kk, rk, rank_o)
    idx_ref[...] = idx_o.astype(jnp.int32)
    wt_ref[...] = wt_o
    rank_ref[...] = rank_o.astype(jnp.int32)
    carry_ref[...] = carry_ref[...] + jnp.sum(mask, axis=0, keepdims=True)
    cnt_ref[...] = carry_ref[...]


def router(h, w_router, b_router, tm=512):
    m, k = h.shape
    ne = w_router.shape[1]
    tm = min(tm, m)
    r = jnp.arange(tm)
    tri = (r[:, None] > r[None, :]).astype(BF16)
    small = pl.BlockSpec((tm, ROUTE_COLS), lambda i: (i, 0))
    return pl.pallas_call(
        _router_kernel,
        grid=(m // tm,),
        in_specs=[pl.BlockSpec((tm, k), lambda i: (i, 0)),
                  pl.BlockSpec((k, ne), lambda i: (0, 0)),
                  pl.BlockSpec((1, ne), lambda i: (0, 0)),
                  pl.BlockSpec((tm, tm), lambda i: (0, 0))],
        out_specs=[small, small, small, pl.BlockSpec((1, ne), lambda i: (0, 0))],
        out_shape=[jax.ShapeDtypeStruct((m, ROUTE_COLS), jnp.int32),
                   jax.ShapeDtypeStruct((m, ROUTE_COLS), F32),
                   jax.ShapeDtypeStruct((m, ROUTE_COLS), jnp.int32),
                   jax.ShapeDtypeStruct((1, ne), F32)],
        scratch_shapes=[pltpu.VMEM((1, ne), F32)],
        compiler_params=_params("arbitrary"),
        name="router",
    )(h, w_router, b_router, tri)


def _experts_kernel(be_ref, bv_ref, li_ref, x_ref, wg_ref, wu_ref, wd_ref, o_ref, wgb_ref,
                    wub_ref, wdb_ref):
    i = pl.program_id(0)

    @pl.when(jnp.logical_or(i == 0, be_ref[i] != be_ref[jnp.maximum(i - 1, 0)]))
    def _():
        wgb_ref[...] = wg_ref[0, 0].astype(BF16)
        wub_ref[...] = wu_ref[0, 0].astype(BF16)
        wdb_ref[...] = wd_ref[0, 0].astype(BF16)

    @pl.when(bv_ref[i] > 0)
    def _():
        half = x_ref.shape[1]
        xa, xb = _unpack_halves(x_ref[...])
        xa = xa.astype(BF16)
        xb = xb.astype(BF16)
        g = (jnp.dot(xa, wgb_ref[:half, :], preferred_element_type=F32)
             + jnp.dot(xb, wgb_ref[half:, :], preferred_element_type=F32))
        u = (jnp.dot(xa, wub_ref[:half, :], preferred_element_type=F32)
             + jnp.dot(xb, wub_ref[half:, :], preferred_element_type=F32))
        a = (g * _sigmoid(g) * u).astype(BF16)
        o_ref[...] = _pack_halves(jnp.dot(a, wdb_ref[...], preferred_element_type=F32))

    @pl.when(bv_ref[i] == 0)
    def _():
        o_ref[...] = jnp.zeros_like(o_ref)


def experts(xs, block_e, block_valid, layer, w_gate, w_up, w_down):
    r, half = xs.shape
    d = 2 * half
    ff = w_gate.shape[3]
    nblk = r // MOE_BLOCK
    return pl.pallas_call(
        _experts_kernel,
        grid_spec=pltpu.PrefetchScalarGridSpec(
            num_scalar_prefetch=3,
            grid=(nblk,),
            in_specs=[pl.BlockSpec((MOE_BLOCK, half), lambda i, be, bv, li: (i, 0)),
                      pl.BlockSpec((1, 1, d, ff), lambda i, be, bv, li: (li[0], be[i], 0, 0)),
                      pl.BlockSpec((1, 1, d, ff), lambda i, be, bv, li: (li[0], be[i], 0, 0)),
                      pl.BlockSpec((1, 1, ff, d), lambda i, be, bv, li: (li[0], be[i], 0, 0))],
            out_specs=pl.BlockSpec((MOE_BLOCK, half), lambda i, be, bv, li: (i, 0)),
            scratch_shapes=[pltpu.VMEM((d, ff), BF16), pltpu.VMEM((d, ff), BF16),
                            pltpu.VMEM((ff, d), BF16)]),
        out_shape=jax.ShapeDtypeStruct((r, half), jnp.uint32),
        compiler_params=_params("arbitrary"),
        name="experts",
    )(block_e, block_valid, layer, xs, w_gate, w_up, w_down)


def _shared_ln_kernel(xb_ref, h_ref, yg_ref, wt_ref, wgu_ref, wd_ref, g_ref, b_ref, o_ref,
                      ob_ref):
    ff = wd_ref.shape[0]
    gu = jnp.dot(xb_ref[...], wgu_ref[...], preferred_element_type=F32)
    g = gu[:, :ff]
    a = (g * _sigmoid(g) * gu[:, ff:]).astype(BF16)
    shared = jnp.dot(a, wd_ref[...], preferred_element_type=F32)
    wt = wt_ref[...]
    r_lo = r_hi = None
    for kk in range(TOP_K):
        lo, hi = _unpack_halves(yg_ref[kk])
        wk = wt[:, kk:kk + 1]
        r_lo = wk * lo if r_lo is None else r_lo + wk * lo
        r_hi = wk * hi if r_hi is None else r_hi + wk * hi
    routed = jnp.concatenate([r_lo, r_hi], axis=1)
    out = _layernorm_rows(ALPHA * h_ref[...] + (routed + shared), g_ref[...], b_ref[...])
    o_ref[...] = out
    ob_ref[...] = out.astype(BF16)


def shared_ln(h_bf, h, yg, wt, w_gu, w_down, g, b, tm=256):
    m, d = h.shape
    ff2 = w_gu.shape[1]
    tm = min(tm, m)
    row = pl.BlockSpec((tm, d), lambda i: (i, 0))
    vec = pl.BlockSpec((1, d), lambda i: (0, 0))
    return pl.pallas_call(
        _shared_ln_kernel,
        grid=(m // tm,),
        in_specs=[row, row,
                  pl.BlockSpec((TOP_K, tm, d // 2), lambda i: (0, i, 0)),
                  pl.BlockSpec((tm, ROUTE_COLS), lambda i: (i, 0)),
                  pl.BlockSpec((d, ff2), lambda i: (0, 0)),
                  pl.BlockSpec((ff2 // 2, d), lambda i: (0, 0)),
                  vec, vec],
        out_specs=[row, row],
        out_shape=[jax.ShapeDtypeStruct((m, d), F32), jax.ShapeDtypeStruct((m, d), BF16)],
        compiler_params=_params("parallel"),
        name="shared_ln",
    )(h_bf, h, yg, wt, w_gu, w_down, g, b)


def _mixer_layout(d):
    rq = RET_H * RET_DK
    rv = RET_H * RET_DV
    fq = FOX_H * FOX_DH
    main = 2 * rq + 2 * rv + 3 * fq
    ff0 = main
    cu0 = ff0 + FOX_H
    gl0 = cu0 + 2 * d
    end = gl0 + N_BRANCH * d
    return dict(rq=rq, rv=rv, fq=fq, main=main, ff0=ff0, cu0=cu0, gl0=gl0, end=end)


def _prep_in_weights(w_in, b_in, d, tn):
    lay = _mixer_layout(d)
    body = lay["main"] + 2 * d + N_BRANCH * d
    pad = tn - FOX_H
    w = jnp.concatenate([w_in[:, :lay["main"]], w_in[:, lay["cu0"]:lay["end"]],
                         w_in[:, lay["ff0"]:lay["cu0"]],
                         jnp.zeros((w_in.shape[0], pad), w_in.dtype)], axis=1).astype(BF16)
    bb = jnp.concatenate([b_in[:lay["main"]], b_in[lay["cu0"]:lay["end"]],
                          b_in[lay["ff0"]:lay["cu0"]], jnp.zeros((pad,), b_in.dtype)])
    return w, bb.reshape(1, -1).astype(F32), body


def _moe_sublayer(h, h_bf, h_pk, w_router, b_router, layer, w_gate, w_up, w_down, w_sh_gu,
                  w_sh_down, g, b):
    t, d = h.shape
    ne = w_router.shape[1]
    idx8, wt8, rank8, cnt = router(h, w_router, b_router.reshape(1, ne))
    idx = idx8[:, :TOP_K]
    rank = rank8[:, :TOP_K]
    sizes = cnt[0].astype(jnp.int32)
    a = t * TOP_K
    n_blocks = -(-a // MOE_BLOCK) + ne
    r = n_blocks * MOE_BLOCK
    padded = (sizes + MOE_BLOCK - 1) // MOE_BLOCK * MOE_BLOCK
    pad_end = jnp.cumsum(padded)
    pad_start = pad_end - padded
    dest = pad_start[idx] + rank
    tok = jnp.broadcast_to(jnp.arange(t, dtype=jnp.int32)[:, None], (t, TOP_K))
    row_tok = (jnp.arange(r, dtype=jnp.int32) % t).at[dest.reshape(a)].set(
        tok.reshape(a), unique_indices=True, mode="promise_in_bounds")
    blk0 = jnp.arange(n_blocks, dtype=jnp.int32) * MOE_BLOCK
    block_e = jnp.minimum(jnp.sum((pad_end[None, :] <= blk0[:, None]).astype(jnp.int32), axis=1),
                          ne - 1)
    block_valid = (blk0 < pad_end[-1]).astype(jnp.int32)
    xs = h_pk[row_tok]
    ys = experts(xs, block_e, block_valid, layer, w_gate, w_up, w_down)
    yg = ys[dest.T.reshape(a)].reshape(TOP_K, t, d // 2)
    return shared_ln(h_bf, h, yg, wt8, w_sh_gu, w_sh_down, g, b)


def kernel(x, w_in, b_in, ret_gn_g, conv_w, conv_b, conv_ln_g, conv_ln_b, w_ret_o, w_fox_o,
           w_conv_o, w_out, ln1_g, ln1_b, w_router, b_router, w_exp_gate, w_exp_up, w_exp_down,
           w_sh_gate, w_sh_up, w_sh_down, ln2_g, ln2_b):
    bsz, seq, d = x.shape
    t = bsz * seq
    in_tn = IN_PAD
    lay = _mixer_layout(d)
    chunk = min(RET_CHUNK, seq)
    tables = _retention_tables(seq, chunk)
    fq0 = 2 * lay["rq"] + 2 * lay["rv"]

    def row(v):
        return v.reshape(1, -1)

    n_grp = BATCH_GROUPS if bsz % BATCH_GROUPS == 0 else 1
    gb = bsz // n_grp
    gt = gb * seq

    def group_layer(h, h_bf, p, wts):
        w_cat, b_cat, body, w_ret, w_fox, w_conv, w_o, w_sh_gu, w_sh_down = wts
        z2 = in_proj(h_bf, w_cat, b_cat, tn=IN_TN)
        z = z2.reshape(gb, seq, -1)
        ret_o = retention(z, row(p["ret_gn_g"]), tables)
        negc = fox_cumsum(z, body // LANES)
        fox_o = fox_attention(z, negc, fq0 // FOX_DH, (fq0 + lay["fq"]) // FOX_DH,
                              (fq0 + 2 * lay["fq"]) // FOX_DH)
        conv_o = conformer_conv(z, lay["main"] // d, lay["main"] // d + 1, p["conv_w"],
                                row(p["conv_b"]), row(p["conv_ln_g"]), row(p["conv_ln_b"]))
        merged = merge(ret_o.reshape(gt, -1), fox_o.reshape(gt, -1), conv_o.reshape(gt, -1),
                       w_ret, w_fox, w_conv, z2, (lay["main"] + 2 * d) // d)
        h, h_bf, h_pk = out_ln(merged, w_o, h, row(p["ln1_g"]), row(p["ln1_b"]))
        return _moe_sublayer(h, h_bf, h_pk, p["w_router"], p["b_router"], p["layer"],
                             w_exp_gate, w_exp_up, w_exp_down, w_sh_gu, w_sh_down,
                             row(p["ln2_g"]), row(p["ln2_b"]))

    def layer(carry, p):
        w_cat, b_cat, body = _prep_in_weights(p["w_in"], p["b_in"], d, in_tn)
        wts = (w_cat, b_cat, body, p["w_ret_o"].astype(BF16), p["w_fox_o"].astype(BF16),
               p["w_conv_o"].astype(BF16), p["w_out"].astype(BF16),
               jnp.concatenate([p["w_sh_gate"], p["w_sh_up"]], axis=-1).astype(BF16),
               p["w_sh_down"].astype(BF16))
        return tuple(group_layer(h, h_bf, p, wts) for h, h_bf in carry), None

    depth = w_in.shape[0]
    params = dict(w_in=w_in, b_in=b_in, ret_gn_g=ret_gn_g, conv_w=conv_w, conv_b=conv_b,
                  conv_ln_g=conv_ln_g, conv_ln_b=conv_ln_b, w_ret_o=w_ret_o, w_fox_o=w_fox_o,
                  w_conv_o=w_conv_o, w_out=w_out, ln1_g=ln1_g, ln1_b=ln1_b, w_router=w_router,
                  b_router=b_router, w_sh_gate=w_sh_gate, w_sh_up=w_sh_up,
                  w_sh_down=w_sh_down, ln2_g=ln2_g, ln2_b=ln2_b,
                  layer=jnp.arange(depth, dtype=jnp.int32).reshape(depth, 1))
    h0 = x.reshape(n_grp, gt, d)
    groups, _ = lax.scan(layer, tuple((h0[i], h0[i].astype(BF16)) for i in range(n_grp)), params)
    return jnp.concatenate([h for h, _ in groups], axis=0).reshape(bsz, seq, d)
```

```python
import functools

import jax
import jax.numpy as jnp
from jax import lax
from jax.experimental import pallas as pl
from jax.experimental.pallas import tpu as pltpu

F32 = jnp.float32
BF16 = jnp.bfloat16

RET_H = 8
RET_DK = 128
RET_DV = 256
ROPE_BASE = 10000.0
FOX_H = 16
FOX_DH = 128
CONV_W = 31
N_BRANCH = 3
TOP_K = 6
ROUTED_SCALE = 2.5
DEPTH_FOR_NORM = 4
ALPHA = (2 * DEPTH_FOR_NORM) ** 0.25
LN_EPS = 1e-5

LANES = 128
SUBLANES = 8
VMEM_LIMIT = 56 * 1024 * 1024

IN_PAD = 512
IN_TM = 2048
IN_TN = 768
RET_CHUNK = 256
FOX_TQ = 2048
FOX_TK = 512
FOX_HG = 1
CUM_BLK = 256
CONV_TS = 256
CONV_HALO = 32
CONV_RB = 256
CONV_CB = 128
MOE_BLOCK = 512
NEG_BIG = -1e30
LOG2E = 1.4426950408889634


def _params(*sem):
    return pltpu.CompilerParams(dimension_semantics=sem, vmem_limit_bytes=VMEM_LIMIT)


def _sigmoid(x):
    return 1.0 / (1.0 + jnp.exp(-x))


def _pack_halves(x):
    c = x.shape[1] // 2
    xb = x.astype(BF16).astype(F32)
    lo = lax.bitcast_convert_type(xb[:, :c], jnp.uint32) >> 16
    hi = lax.bitcast_convert_type(xb[:, c:], jnp.uint32) & jnp.uint32(0xFFFF0000)
    return lo | hi


def _unpack_halves(u):
    lo = lax.bitcast_convert_type(u << 16, F32)
    hi = lax.bitcast_convert_type(u & jnp.uint32(0xFFFF0000), F32)
    return lo, hi


def _layernorm_rows(x, g, b):
    mu = jnp.mean(x, axis=-1, keepdims=True)
    xc = x - mu
    var = jnp.mean(xc * xc, axis=-1, keepdims=True)
    return xc * lax.rsqrt(var + LN_EPS) * g + b


def _in_proj_kernel(x_ref, w_ref, b_ref, o_ref):
    acc = jnp.dot(x_ref[...], w_ref[...], preferred_element_type=F32)
    o_ref[...] = acc + b_ref[...]


def in_proj(x_bf, w_bf, b, tm=IN_TM, tn=IN_TN):
    m, k = x_bf.shape
    n = w_bf.shape[1]
    tm = min(tm, m)
    assert m % tm == 0 and n % tn == 0
    return pl.pallas_call(
        _in_proj_kernel,
        grid=(m // tm, n // tn),
        in_specs=[pl.BlockSpec((tm, k), lambda i, j: (i, 0)),
                  pl.BlockSpec((k, tn), lambda i, j: (0, j)),
                  pl.BlockSpec((1, tn), lambda i, j: (0, j))],
        out_specs=pl.BlockSpec((tm, tn), lambda i, j: (i, j)),
        out_shape=jax.ShapeDtypeStruct((m, n), F32),
        compiler_params=_params("parallel", "parallel"),
        name="in_proj",
    )(x_bf, w_bf, b)


def _retention_kernel(cd_ref, q_ref, k_ref, v_ref, g_ref, cos_ref, sin_ref, decay_ref,
                      xi_ref, zeta_ref, gn_ref, o_ref, state_ref):
    @pl.when(pl.program_id(1) == 0)
    def _():
        state_ref[...] = jnp.zeros_like(state_ref)

    cos = cos_ref[...]
    sin = sin_ref[...]
    for h in range(RET_H):
        qs = slice(h * RET_DK, (h + 1) * RET_DK)
        vs = slice(h * RET_DV, (h + 1) * RET_DV)
        q = q_ref[0, :, qs]
        k = k_ref[0, :, qs]
        qr = q * cos + pltpu.roll(q, RET_DK // 2, 1) * sin
        kr = (k * cos + pltpu.roll(k, RET_DK // 2, 1) * sin) * (RET_DK ** -0.5)
        vb = v_ref[0, :, vs].astype(BF16)
        s = lax.dot_general(qr.astype(BF16), kr.astype(BF16), (((1,), (1,)), ((), ())),
                            preferred_element_type=F32)
        inner = (s * decay_ref[h]).astype(BF16)
        o = jnp.dot(inner, vb, preferred_element_type=F32)
        st = state_ref[h]
        o = o + jnp.dot((qr * xi_ref[h]).astype(BF16), st.astype(BF16),
                        preferred_element_type=F32)
        kz = (kr * zeta_ref[h]).astype(BF16)
        state_ref[h] = st * cd_ref[h] + lax.dot_general(
            kz, vb, (((0,), (0,)), ((), ())), preferred_element_type=F32)
        mu = jnp.mean(o, axis=-1, keepdims=True)
        oc = o - mu
        var = jnp.mean(oc * oc, axis=-1, keepdims=True)
        on = oc * lax.rsqrt(var + LN_EPS) * gn_ref[:, vs]
        g = g_ref[0, :, vs]
        o_ref[0, :, vs] = (g * _sigmoid(g) * on).astype(BF16)


def _retention_tables(seq, chunk):
    h = jnp.arange(RET_H, dtype=F32)
    log_g = jnp.log1p(-jnp.exp2(-5.0 - h))
    i = jnp.arange(chunk, dtype=F32)
    diff = i[:, None] - i[None, :]
    decay = jnp.where(diff >= 0, jnp.exp(log_g[:, None, None] * jnp.maximum(diff, 0.0)), 0.0)
    xi = jnp.exp(log_g[:, None] * (i + 1.0))
    zeta = jnp.exp(log_g[:, None] * (chunk - 1.0 - i))
    xi = jnp.broadcast_to(xi[:, :, None], (RET_H, chunk, RET_DK))
    zeta = jnp.broadcast_to(zeta[:, :, None], (RET_H, chunk, RET_DK))
    cd = jnp.exp(log_g * chunk)
    half = RET_DK // 2
    inv = 1.0 / (ROPE_BASE ** (jnp.arange(half, dtype=F32) / half))
    ang = jnp.arange(seq, dtype=F32)[:, None] * inv[None, :]
    cos = jnp.cos(ang)
    sin = jnp.sin(ang)
    cos2 = jnp.concatenate([cos, cos], axis=-1)
    sin2 = jnp.concatenate([-sin, sin], axis=-1)
    return cd, cos2, sin2, decay, xi, zeta


def retention(z, gn_g, tables):
    b, s, _ = z.shape
    cd, cos2, sin2, decay, xi, zeta = tables
    L = decay.shape[1]
    qw = RET_H * RET_DK
    vw = RET_H * RET_DV
    return pl.pallas_call(
        _retention_kernel,
        grid=(b, s // L),
        in_specs=[pl.BlockSpec(memory_space=pltpu.SMEM),
                  pl.BlockSpec((1, L, qw), lambda bi, n: (bi, n, 0)),
                  pl.BlockSpec((1, L, qw), lambda bi, n: (bi, n, 1)),
                  pl.BlockSpec((1, L, vw), lambda bi, n: (bi, n, 1)),
                  pl.BlockSpec((1, L, vw), lambda bi, n: (bi, n, 2)),
                  pl.BlockSpec((L, RET_DK), lambda bi, n: (n, 0)),
                  pl.BlockSpec((L, RET_DK), lambda bi, n: (n, 0)),
                  pl.BlockSpec((RET_H, L, L), lambda bi, n: (0, 0, 0)),
                  pl.BlockSpec((RET_H, L, RET_DK), lambda bi, n: (0, 0, 0)),
                  pl.BlockSpec((RET_H, L, RET_DK), lambda bi, n: (0, 0, 0)),
                  pl.BlockSpec((1, vw), lambda bi, n: (0, 0))],
        out_specs=pl.BlockSpec((1, L, vw), lambda bi, n: (bi, n, 0)),
        out_shape=jax.ShapeDtypeStruct((b, s, vw), BF16),
        scratch_shapes=[pltpu.VMEM((RET_H, RET_DK, RET_DV), F32)],
        compiler_params=_params("parallel", "arbitrary"),
        name="retention",
    )(cd, z, z, z, z, cos2, sin2, decay, xi, zeta, gn_g)


def _fox_cumsum_kernel(f_ref, tri_ref, o_ref):
    s = f_ref.shape[1]
    tri = tri_ref[...]
    carry = jnp.zeros((1, LANES), F32)
    for i in range(s // CUM_BLK):
        f = f_ref[0, i * CUM_BLK:(i + 1) * CUM_BLK, :]
        ls = jnp.minimum(f, 0.0) - jnp.log1p(jnp.exp(-jnp.abs(f)))
        hi = ls.astype(BF16)
        r1 = ls - hi.astype(F32)
        mid = r1.astype(BF16)
        lo = (r1 - mid.astype(F32)).astype(BF16)
        c = (jnp.dot(tri, hi, preferred_element_type=F32)
             + jnp.dot(tri, mid, preferred_element_type=F32)
             + jnp.dot(tri, lo, preferred_element_type=F32)) + carry
        carry = c[CUM_BLK - 1:CUM_BLK, :]
        o_ref[0, :, i * CUM_BLK:(i + 1) * CUM_BLK] = (-c).T[:FOX_H, :]


def fox_cumsum(z, ff_block):
    b, s, _ = z.shape
    r = jnp.arange(CUM_BLK)
    tri = (r[:, None] >= r[None, :]).astype(BF16)
    return pl.pallas_call(
        _fox_cumsum_kernel,
        grid=(b,),
        in_specs=[pl.BlockSpec((1, s, LANES), lambda bi: (bi, 0, ff_block)),
                  pl.BlockSpec((CUM_BLK, CUM_BLK), lambda bi: (0, 0))],
        out_specs=pl.BlockSpec((1, FOX_H, s), lambda bi: (bi, 0, 0)),
        out_shape=jax.ShapeDtypeStruct((b, FOX_H, s), F32),
        compiler_params=_params("parallel"),
        name="fox_cumsum",
    )(z, tri)


def _fox_attn_kernel(q_ref, k_ref, v_ref, nc_ref, o_ref, kt_ref, vx_ref, qb_ref, m_ref, acc_ref):
    qi = pl.program_id(2)
    s_len = k_ref.shape[1]

    def head(g):
        return slice(g * FOX_DH, (g + 1) * FOX_DH)

    @pl.when(qi == 0)
    def _():
        for g in range(FOX_HG):
            for c in range(s_len // FOX_TK):
                rows = slice(c * FOX_TK, (c + 1) * FOX_TK)
                kt_ref[g, :, rows] = k_ref[0, rows, head(g)].T.astype(BF16)
                vx_ref[g, rows, :FOX_DH] = v_ref[0, rows, head(g)].astype(BF16)
            vx_ref[g, :, FOX_DH:] = jnp.ones((s_len, FOX_DH), BF16)

    for g in range(FOX_HG):
        qb_ref[g] = (q_ref[0, :, head(g)] * (FOX_DH ** -0.5 * LOG2E)).astype(BF16)
    m_ref[...] = jnp.full_like(m_ref, NEG_BIG)
    acc_ref[...] = jnp.zeros_like(acc_ref)

    def tile(j, diag):
        k0 = pl.multiple_of(j * FOX_TK, FOX_TK)
        rows = slice(0 if diag is None else diag * FOX_TK, FOX_TQ)
        for g in range(FOX_HG):
            s = jnp.dot(qb_ref[g, rows, :], kt_ref[g, :, pl.ds(k0, FOX_TK)],
                        preferred_element_type=F32)
            s = s + nc_ref[0, g, :, pl.ds(k0, FOX_TK)] * LOG2E
            if diag is not None:
                row = lax.broadcasted_iota(jnp.int32, s.shape, 0)
                col = lax.broadcasted_iota(jnp.int32, s.shape, 1)
                s = jnp.where(col <= row, s, NEG_BIG)
            m_prev = m_ref[g, rows, :]
            m_new = jnp.maximum(m_prev, jnp.max(s, axis=-1, keepdims=True))
            alpha = jnp.exp2(m_prev - m_new)
            p = jnp.concatenate(
                [jnp.exp2(s[:, c * LANES:(c + 1) * LANES] - m_new)
                 for c in range(FOX_TK // LANES)], axis=1).astype(BF16)
            pv = jnp.dot(p, vx_ref[g, pl.ds(k0, FOX_TK), :], preferred_element_type=F32)
            acc_ref[g, rows, :] = jnp.concatenate([alpha, alpha], axis=1) * acc_ref[g, rows, :] + pv
            m_ref[g, rows, :] = m_new

    def body(j, c):
        tile(j, None)
        return c

    per_q = FOX_TQ // FOX_TK
    n_full = qi * per_q
    lax.fori_loop(0, n_full, body, 0)
    for r in range(per_q):
        tile(n_full + r, r)
    for g in range(FOX_HG):
        o_ref[0, :, head(g)] = (acc_ref[g, :, :FOX_DH] / acc_ref[g, :, FOX_DH:]).astype(BF16)


def fox_attention(z, negc, q_blk, k_blk, v_blk):
    b, s, _ = z.shape
    assert FOX_TQ % FOX_TK == 0 and FOX_DH == LANES and FOX_H % FOX_HG == 0
    gw = FOX_HG * FOX_DH
    negc4 = negc.reshape(b, FOX_H, 1, s)
    qg, kg, vg = q_blk // FOX_HG, k_blk // FOX_HG, v_blk // FOX_HG
    return pl.pallas_call(
        _fox_attn_kernel,
        grid=(b, FOX_H // FOX_HG, s // FOX_TQ),
        in_specs=[pl.BlockSpec((1, FOX_TQ, gw), lambda bi, h, qi: (bi, qi, qg + h)),
                  pl.BlockSpec((1, s, gw), lambda bi, h, qi: (bi, 0, kg + h)),
                  pl.BlockSpec((1, s, gw), lambda bi, h, qi: (bi, 0, vg + h)),
                  pl.BlockSpec((1, FOX_HG, 1, s), lambda bi, h, qi: (bi, h, 0, 0))],
        out_specs=pl.BlockSpec((1, FOX_TQ, gw), lambda bi, h, qi: (bi, qi, h)),
        out_shape=jax.ShapeDtypeStruct((b, s, FOX_H * FOX_DH), BF16),
        scratch_shapes=[pltpu.VMEM((FOX_HG, FOX_DH, s), BF16),
                        pltpu.VMEM((FOX_HG, s, 2 * FOX_DH), BF16),
                        pltpu.VMEM((FOX_HG, FOX_TQ, FOX_DH), BF16),
                        pltpu.VMEM((FOX_HG, FOX_TQ, LANES), F32),
                        pltpu.VMEM((FOX_HG, FOX_TQ, 2 * FOX_DH), F32)],
        compiler_params=_params("parallel", "parallel", "arbitrary"),
        name="fox_attn",
    )(z, z, z, negc4)


def _conv_kernel(a_ref, g_ref, ap_ref, gp_ref, w_ref, cb_ref, lg_ref, lb_ref, o_ref, y_ref,
                 c_ref):
    ts = a_ref.shape[1]
    c_all = a_ref.shape[2]
    prev = ap_ref[0] * _sigmoid(gp_ref[0])
    y_ref[0:CONV_HALO, :] = jnp.where(pl.program_id(1) > 0, prev, 0.0)
    y_ref[CONV_HALO:CONV_HALO + ts, :] = a_ref[0] * _sigmoid(g_ref[0])
    lead = CONV_HALO - (CONV_W - 1)

    n_cb = c_all // CONV_CB
    win_rows = CONV_RB + CONV_HALO

    def block(idx, carry):
        r0 = pl.multiple_of((idx // n_cb) * CONV_RB, CONV_RB)
        c0 = pl.multiple_of((idx % n_cb) * CONV_CB, CONV_CB)
        win = y_ref[pl.ds(r0, win_rows), pl.ds(c0, CONV_CB)]
        acc = jnp.zeros((CONV_RB, CONV_CB), F32)
        for ph in range(SUBLANES):
            rot = win if ph == 0 else pltpu.roll(win, win_rows - ph, 0)
            for w in range(CONV_W):
                if (w + lead) % SUBLANES == ph:
                    a0 = (w + lead) - ph
                    acc = acc + w_ref[w:w + 1, pl.ds(c0, CONV_CB)] * rot[a0:a0 + CONV_RB]
        c_ref[pl.ds(r0, CONV_RB), pl.ds(c0, CONV_CB)] = acc + cb_ref[:, pl.ds(c0, CONV_CB)]
        return carry

    lax.fori_loop(0, (ts // CONV_RB) * n_cb, block, 0)
    y = _layernorm_rows(c_ref[...], lg_ref[...], lb_ref[...])
    o_ref[0] = (y * _sigmoid(y)).astype(BF16)


def conformer_conv(z, a_blk, g_blk, conv_w, conv_b, ln_g, ln_b):
    b, s, _ = z.shape
    c = conv_w.shape[1]
    ts = min(CONV_TS, s)
    hb = ts // CONV_HALO

    def halo(col):
        return lambda bi, i: (bi, jnp.maximum(i * hb - 1, 0), col)

    return pl.pallas_call(
        _conv_kernel,
        grid=(b, s // ts),
        in_specs=[pl.BlockSpec((1, ts, c), lambda bi, i: (bi, i, a_blk)),
                  pl.BlockSpec((1, ts, c), lambda bi, i: (bi, i, g_blk)),
                  pl.BlockSpec((1, CONV_HALO, c), halo(a_blk)),
                  pl.BlockSpec((1, CONV_HALO, c), halo(g_blk)),
                  pl.BlockSpec((CONV_W, c), lambda bi, i: (0, 0)),
                  pl.BlockSpec((1, c), lambda bi, i: (0, 0)),
                  pl.BlockSpec((1, c), lambda bi, i: (0, 0)),
                  pl.BlockSpec((1, c), lambda bi, i: (0, 0))],
        out_specs=pl.BlockSpec((1, ts, c), lambda bi, i: (bi, i, 0)),
        out_shape=jax.ShapeDtypeStruct((b, s, c), BF16),
        scratch_shapes=[pltpu.VMEM((CONV_HALO + ts, c), F32),
                        pltpu.VMEM((ts, c), F32)],
        compiler_params=_params("parallel", "parallel"),
        name="conv",
    )(z, z, z, z, conv_w, conv_b, ln_g, ln_b)


def _merge_kernel(a_ref, f_ref, c_ref, wa_ref, wf_ref, wc_ref, g0_ref, g1_ref, g2_ref, o_ref):
    ya = jnp.dot(a_ref[...], wa_ref[...], preferred_element_type=F32)
    m = _sigmoid(g0_ref[...]) * ya
    yf = jnp.dot(f_ref[...], wf_ref[...], preferred_element_type=F32)
    m = m + _sigmoid(g1_ref[...]) * yf
    yc = jnp.dot(c_ref[...], wc_ref[...], preferred_element_type=F32)
    m = m + _sigmoid(g2_ref[...]) * yc
    o_ref[...] = m.astype(BF16)


def merge(ret_o, fox_o, conv_o, w_ret, w_fox, w_conv, z2, gl_blk, tm=1024, tn=256):
    m, k = ret_o.shape
    n = w_ret.shape[1]
    tm = min(tm, m)
    nb = n // tn
    act = pl.BlockSpec((tm, k), lambda i, j: (i, 0))
    wsp = pl.BlockSpec((k, tn), lambda i, j: (0, j))

    def gate(br):
        return pl.BlockSpec((tm, tn), lambda i, j: (i, gl_blk * nb + br * nb + j))

    return pl.pallas_call(
        _merge_kernel,
        grid=(m // tm, nb),
        in_specs=[act, act, act, wsp, wsp, wsp, gate(0), gate(1), gate(2)],
        out_specs=pl.BlockSpec((tm, tn), lambda i, j: (i, j)),
        out_shape=jax.ShapeDtypeStruct((m, n), BF16),
        compiler_params=_params("parallel", "parallel"),
        name="merge",
    )(ret_o, fox_o, conv_o, w_ret, w_fox, w_conv, z2, z2, z2)


def _out_ln_kernel(x_ref, w_ref, h_ref, g_ref, b_ref, o_ref, ob_ref, op_ref):
    y = jnp.dot(x_ref[...], w_ref[...], preferred_element_type=F32)
    out = _layernorm_rows(ALPHA * h_ref[...] + y, g_ref[...], b_ref[...])
    o_ref[...] = out
    ob_ref[...] = out.astype(BF16)
    op_ref[...] = _pack_halves(out)


def out_ln(merged, w_out, h, g, b, tm=512):
    m, k = merged.shape
    n = w_out.shape[1]
    tm = min(tm, m)
    return pl.pallas_call(
        _out_ln_kernel,
        grid=(m // tm,),
        in_specs=[pl.BlockSpec((tm, k), lambda i: (i, 0)),
                  pl.BlockSpec((k, n), lambda i: (0, 0)),
                  pl.BlockSpec((tm, n), lambda i: (i, 0)),
                  pl.BlockSpec((1, n), lambda i: (0, 0)),
                  pl.BlockSpec((1, n), lambda i: (0, 0))],
        out_specs=[pl.BlockSpec((tm, n), lambda i: (i, 0)),
                   pl.BlockSpec((tm, n), lambda i: (i, 0)),
                   pl.BlockSpec((tm, n // 2), lambda i: (i, 0))],
        out_shape=[jax.ShapeDtypeStruct((m, n), F32), jax.ShapeDtypeStruct((m, n), BF16),
                   jax.ShapeDtypeStruct((m, n // 2), jnp.uint32)],
        compiler_params=_params("parallel"),
        name="out_ln",
    )(merged, w_out, h, g, b)


ROUTE_COLS = 8


def _router_kernel(h_ref, w_ref, b_ref, tri_ref, idx_ref, wt_ref, rank_ref, cnt_ref, carry_ref):
    @pl.when(pl.program_id(0) == 0)
    def _():
        carry_ref[...] = jnp.zeros_like(carry_ref)

    h = h_ref[...]
    hh = h.astype(BF16)
    hl = (h - hh.astype(F32)).astype(BF16)
    w = w_ref[...]
    wh = w.astype(BF16)
    wl = (w - wh.astype(F32)).astype(BF16)
    logits = (jnp.dot(hh, wh, preferred_element_type=F32)
              + jnp.dot(hl, wh, preferred_element_type=F32)
              + jnp.dot(hh, wl, preferred_element_type=F32))
    scores = _sigmoid(logits)
    tm, ne = scores.shape
    sel = scores + b_ref[...]
    lane = lax.broadcasted_iota(jnp.int32, (tm, ne), 1).astype(F32)
    mask = jnp.zeros((tm, ne), F32)
    onehots, idxs, wts = [], [], []
    for _ in range(TOP_K):
        mx = jnp.max(sel, axis=-1, keepdims=True)
        ik = jnp.min(jnp.where(sel == mx, lane, float(ne)), axis=-1, keepdims=True)
        oh = lane == ik
        wts.append(jnp.sum(jnp.where(oh, scores, 0.0), axis=-1, keepdims=True))
        idxs.append(ik)
        onehots.append(oh)
        sel = jnp.where(oh, -jnp.inf, sel)
        mask = jnp.where(oh, 1.0, mask)
    wsum = wts[0]
    for t in wts[1:]:
        wsum = wsum + t
    cnt = jnp.dot(tri_ref[...], mask.astype(BF16), preferred_element_type=F32) + carry_ref[...]
    col = lax.broadcasted_iota(jnp.int32, (tm, ROUTE_COLS), 1)
    idx_o = jnp.zeros((tm, ROUTE_COLS), F32)
    wt_o = jnp.zeros((tm, ROUTE_COLS), F32)
    rank_o = jnp.zeros((tm, ROUTE_COLS), F32)
    for kk in range(TOP_K):
        rk = jnp.sum(jnp.where(onehots[kk], cnt, 0.0), axis=-1, keepdims=True)
        idx_o = jnp.where(col == kk, idxs[kk], idx_o)
        wt_o = jnp.where(col == kk, wts[kk] / wsum * ROUTED_SCALE, wt_o)
        rank_o = jnp.where(col == kk, rk, rank_o)
    idx_ref[...] = idx_o.astype(jnp.int32)
    wt_ref[...] = wt_o
    rank_ref[...] = rank_o.astype(jnp.int32)
    carry_ref[...] = carry_ref[...] + jnp.sum(mask, axis=0, keepdims=True)
    cnt_ref[...] = carry_ref[...]


def router(h, w_router, b_router, tm=512):
    m, k = h.shape
    ne = w_router.shape[1]
    tm = min(tm, m)
    r = jnp.arange(tm)
    tri = (r[:, None] > r[None, :]).astype(BF16)
    small = pl.BlockSpec((tm, ROUTE_COLS), lambda i: (i, 0))
    return pl.pallas_call(
        _router_kernel,
        grid=(m // tm,),
        in_specs=[pl.BlockSpec((tm, k), lambda i: (i, 0)),
                  pl.BlockSpec((k, ne), lambda i: (0, 0)),
                  pl.BlockSpec((1, ne), lambda i: (0, 0)),
                  pl.BlockSpec((tm, tm), lambda i: (0, 0))],
        out_specs=[small, small, small, pl.BlockSpec((1, ne), lambda i: (0, 0))],
        out_shape=[jax.ShapeDtypeStruct((m, ROUTE_COLS), jnp.int32),
                   jax.ShapeDtypeStruct((m, ROUTE_COLS), F32),
                   jax.ShapeDtypeStruct((m, ROUTE_COLS), jnp.int32),
                   jax.ShapeDtypeStruct((1, ne), F32)],
        scratch_shapes=[pltpu.VMEM((1, ne), F32)],
        compiler_params=_params("arbitrary"),
        name="router",
    )(h, w_router, b_router, tri)


def _experts_kernel(be_ref, bv_ref, li_ref, x_ref, wg_ref, wu_ref, wd_ref, o_ref, wgb_ref,
                    wub_ref, wdb_ref):
    i = pl.program_id(0)

    @pl.when(jnp.logical_or(i == 0, be_ref[i] != be_ref[jnp.maximum(i - 1, 0)]))
    def _():
        wgb_ref[...] = wg_ref[0, 0].astype(BF16)
        wub_ref[...] = wu_ref[0, 0].astype(BF16)
        wdb_ref[...] = wd_ref[0, 0].astype(BF16)

    @pl.when(bv_ref[i] > 0)
    def _():
        half = x_ref.shape[1]
        xa, xb = _unpack_halves(x_ref[...])
        xa = xa.astype(BF16)
        xb = xb.astype(BF16)
        g = (jnp.dot(xa, wgb_ref[:half, :], preferred_element_type=F32)
             + jnp.dot(xb, wgb_ref[half:, :], preferred_element_type=F32))
        u = (jnp.dot(xa, wub_ref[:half, :], preferred_element_type=F32)
             + jnp.dot(xb, wub_ref[half:, :], preferred_element_type=F32))
        a = (g * _sigmoid(g) * u).astype(BF16)
        o_ref[...] = _pack_halves(jnp.dot(a, wdb_ref[...], preferred_element_type=F32))

    @pl.when(bv_ref[i] == 0)
    def _():
        o_ref[...] = jnp.zeros_like(o_ref)


def experts(xs, block_e, block_valid, layer, w_gate, w_up, w_down):
    r, half = xs.shape
    d = 2 * half
    ff = w_gate.shape[3]
    nblk = r // MOE_BLOCK
    return pl.pallas_call(
        _experts_kernel,
        grid_spec=pltpu.PrefetchScalarGridSpec(
            num_scalar_prefetch=3,
            grid=(nblk,),
            in_specs=[pl.BlockSpec((MOE_BLOCK, half), lambda i, be, bv, li: (i, 0)),
                      pl.BlockSpec((1, 1, d, ff), lambda i, be, bv, li: (li[0], be[i], 0, 0)),
                      pl.BlockSpec((1, 1, d, ff), lambda i, be, bv, li: (li[0], be[i], 0, 0)),
                      pl.BlockSpec((1, 1, ff, d), lambda i, be, bv, li: (li[0], be[i], 0, 0))],
            out_specs=pl.BlockSpec((MOE_BLOCK, half), lambda i, be, bv, li: (i, 0)),
            scratch_shapes=[pltpu.VMEM((d, ff), BF16), pltpu.VMEM((d, ff), BF16),
                            pltpu.VMEM((ff, d), BF16)]),
        out_shape=jax.ShapeDtypeStruct((r, half), jnp.uint32),
        compiler_params=_params("arbitrary"),
        name="experts",
    )(block_e, block_valid, layer, xs, w_gate, w_up, w_down)


def _shared_ln_kernel(xb_ref, h_ref, yg_ref, wt_ref, wgu_ref, wd_ref, g_ref, b_ref, o_ref,
                      ob_ref):
    ff = wd_ref.shape[0]
    gu = jnp.dot(xb_ref[...], wgu_ref[...], preferred_element_type=F32)
    g = gu[:, :ff]
    a = (g * _sigmoid(g) * gu[:, ff:]).astype(BF16)
    shared = jnp.dot(a, wd_ref[...], preferred_element_type=F32)
    wt = wt_ref[...]
    r_lo = r_hi = None
    for kk in range(TOP_K):
        lo, hi = _unpack_halves(yg_ref[kk])
        wk = wt[:, kk:kk + 1]
        r_lo = wk * lo if r_lo is None else r_lo + wk * lo
        r_hi = wk * hi if r_hi is None else r_hi + wk * hi
    routed = jnp.concatenate([r_lo, r_hi], axis=1)
    out = _layernorm_rows(ALPHA * h_ref[...] + (routed + shared), g_ref[...], b_ref[...])
    o_ref[...] = out
    ob_ref[...] = out.astype(BF16)


def shared_ln(h_bf, h, yg, wt, w_gu, w_down, g, b, tm=256):
    m, d = h.shape
    ff2 = w_gu.shape[1]
    tm = min(tm, m)
    row = pl.BlockSpec((tm, d), lambda i: (i, 0))
    vec = pl.BlockSpec((1, d), lambda i: (0, 0))
    return pl.pallas_call(
        _shared_ln_kernel,
        grid=(m // tm,),
        in_specs=[row, row,
                  pl.BlockSpec((TOP_K, tm, d // 2), lambda i: (0, i, 0)),
                  pl.BlockSpec((tm, ROUTE_COLS), lambda i: (i, 0)),
                  pl.BlockSpec((d, ff2), lambda i: (0, 0)),
                  pl.BlockSpec((ff2 // 2, d), lambda i: (0, 0)),
                  vec, vec],
        out_specs=[row, row],
        out_shape=[jax.ShapeDtypeStruct((m, d), F32), jax.ShapeDtypeStruct((m, d), BF16)],
        compiler_params=_params("parallel"),
        name="shared_ln",
    )(h_bf, h, yg, wt, w_gu, w_down, g, b)


def _mixer_layout(d):
    rq = RET_H * RET_DK
    rv = RET_H * RET_DV
    fq = FOX_H * FOX_DH
    main = 2 * rq + 2 * rv + 3 * fq
    ff0 = main
    cu0 = ff0 + FOX_H
    gl0 = cu0 + 2 * d
    end = gl0 + N_BRANCH * d
    return dict(rq=rq, rv=rv, fq=fq, main=main, ff0=ff0, cu0=cu0, gl0=gl0, end=end)


def _prep_in_weights(w_in, b_in, d, tn):
    lay = _mixer_layout(d)
    body = lay["main"] + 2 * d + N_BRANCH * d
    pad = tn - FOX_H
    w = jnp.concatenate([w_in[:, :lay["main"]], w_in[:, lay["cu0"]:lay["end"]],
                         w_in[:, lay["ff0"]:lay["cu0"]],
                         jnp.zeros((w_in.shape[0], pad), w_in.dtype)], axis=1).astype(BF16)
    bb = jnp.concatenate([b_in[:lay["main"]], b_in[lay["cu0"]:lay["end"]],
                          b_in[lay["ff0"]:lay["cu0"]], jnp.zeros((pad,), b_in.dtype)])
    return w, bb.reshape(1, -1).astype(F32), body


def _moe_sublayer(h, h_bf, h_pk, w_router, b_router, layer, w_gate, w_up, w_down, w_sh_gu,
                  w_sh_down, g, b):
    t, d = h.shape
    ne = w_router.shape[1]
    idx8, wt8, rank8, cnt = router(h, w_router, b_router.reshape(1, ne))
    idx = idx8[:, :TOP_K]
    rank = rank8[:, :TOP_K]
    sizes = cnt[0].astype(jnp.int32)
    a = t * TOP_K
    n_blocks = -(-a // MOE_BLOCK) + ne
    r = n_blocks * MOE_BLOCK
    padded = (sizes + MOE_BLOCK - 1) // MOE_BLOCK * MOE_BLOCK
    pad_end = jnp.cumsum(padded)
    pad_start = pad_end - padded
    dest = pad_start[idx] + rank
    tok = jnp.broadcast_to(jnp.arange(t, dtype=jnp.int32)[:, None], (t, TOP_K))
    row_tok = (jnp.arange(r, dtype=jnp.int32) % t).at[dest.reshape(a)].set(
        tok.reshape(a), unique_indices=True, mode="promise_in_bounds")
    blk0 = jnp.arange(n_blocks, dtype=jnp.int32) * MOE_BLOCK
    block_e = jnp.minimum(jnp.sum((pad_end[None, :] <= blk0[:, None]).astype(jnp.int32), axis=1),
                          ne - 1)
    block_valid = (blk0 < pad_end[-1]).astype(jnp.int32)
    xs = h_pk[row_tok]
    ys = experts(xs, block_e, block_valid, layer, w_gate, w_up, w_down)
    yg = ys[dest.T.reshape(a)].reshape(TOP_K, t, d // 2)
    return shared_ln(h_bf, h, yg, wt8, w_sh_gu, w_sh_down, g, b)


def kernel(x, w_in, b_in, ret_gn_g, conv_w, conv_b, conv_ln_g, conv_ln_b, w_ret_o, w_fox_o,
           w_conv_o, w_out, ln1_g, ln1_b, w_router, b_router, w_exp_gate, w_exp_up, w_exp_down,
           w_sh_gate, w_sh_up, w_sh_down, ln2_g, ln2_b):
    bsz, seq, d = x.shape
    t = bsz * seq
    in_tn = IN_PAD
    lay = _mixer_layout(d)
    chunk = min(RET_CHUNK, seq)
    tables = _retention_tables(seq, chunk)
    fq0 = 2 * lay["rq"] + 2 * lay["rv"]

    def row(v):
        return v.reshape(1, -1)

    def layer(carry, p):
        h, h_bf = carry
        w_cat, b_cat, body = _prep_in_weights(p["w_in"], p["b_in"], d, in_tn)
        z2 = in_proj(h_bf, w_cat, b_cat)
        z = z2.reshape(bsz, seq, -1)
        ret_o = retention(z, row(p["ret_gn_g"]), tables)
        negc = fox_cumsum(z, body // LANES)
        fox_o = fox_attention(z, negc, fq0 // FOX_DH, (fq0 + lay["fq"]) // FOX_DH,
                              (fq0 + 2 * lay["fq"]) // FOX_DH)
        conv_o = conformer_conv(z, lay["main"] // d, lay["main"] // d + 1, p["conv_w"],
                                row(p["conv_b"]), row(p["conv_ln_g"]), row(p["conv_ln_b"]))
        merged = merge(ret_o.reshape(t, -1), fox_o.reshape(t, -1), conv_o.reshape(t, -1),
                       p["w_ret_o"].astype(BF16), p["w_fox_o"].astype(BF16),
                       p["w_conv_o"].astype(BF16), z2, (lay["main"] + 2 * d) // d)
        h, h_bf, h_pk = out_ln(merged, p["w_out"].astype(BF16), h, row(p["ln1_g"]),
                               row(p["ln1_b"]))
        w_sh_gu = jnp.concatenate([p["w_sh_gate"], p["w_sh_up"]], axis=-1).astype(BF16)
        h, h_bf = _moe_sublayer(h, h_bf, h_pk, p["w_router"], p["b_router"], p["layer"],
                                w_exp_gate, w_exp_up, w_exp_down, w_sh_gu,
                                p["w_sh_down"].astype(BF16), row(p["ln2_g"]), row(p["ln2_b"]))
        return (h, h_bf), None

    depth = w_in.shape[0]
    params = dict(w_in=w_in, b_in=b_in, ret_gn_g=ret_gn_g, conv_w=conv_w, conv_b=conv_b,
                  conv_ln_g=conv_ln_g, conv_ln_b=conv_ln_b, w_ret_o=w_ret_o, w_fox_o=w_fox_o,
                  w_conv_o=w_conv_o, w_out=w_out, ln1_g=ln1_g, ln1_b=ln1_b, w_router=w_router,
                  b_router=b_router, w_sh_gate=w_sh_gate, w_sh_up=w_sh_up,
                  w_sh_down=w_sh_down, ln2_g=ln2_g, ln2_b=ln2_b,
                  layer=jnp.arange(depth, dtype=jnp.int32).reshape(depth, 1))
    h0 = x.reshape(t, d)
    (h, _), _ = lax.scan(layer, (h0, h0.astype(BF16)), params)
    return h.reshape(bsz, seq, d)
```

```python
import functools

import jax
import jax.numpy as jnp
from jax import lax
from jax.experimental import pallas as pl
from jax.experimental.pallas import tpu as pltpu

F32 = jnp.float32
BF16 = jnp.bfloat16

RET_H = 8
RET_DK = 128
RET_DV = 256
ROPE_BASE = 10000.0
FOX_H = 16
FOX_DH = 128
CONV_W = 31
N_BRANCH = 3
TOP_K = 6
ROUTED_SCALE = 2.5
DEPTH_FOR_NORM = 4
ALPHA = (2 * DEPTH_FOR_NORM) ** 0.25
LN_EPS = 1e-5

LANES = 128
SUBLANES = 8
VMEM_LIMIT = 56 * 1024 * 1024

IN_PAD = 512
IN_TM = 1024
IN_TN = 1536
RET_CHUNK = 256
FOX_TQ = 2048
FOX_TK = 512
FOX_HG = 1
CUM_BLK = 256
CONV_TS = 256
CONV_HALO = 32
CONV_RB = 256
CONV_CB = 128
MOE_BLOCK = 512
NEG_BIG = -1e30
LOG2E = 1.4426950408889634


def _params(*sem):
    return pltpu.CompilerParams(dimension_semantics=sem, vmem_limit_bytes=VMEM_LIMIT)


def _sigmoid(x):
    return 1.0 / (1.0 + jnp.exp(-x))


def _pack_halves(x):
    c = x.shape[1] // 2
    xb = x.astype(BF16).astype(F32)
    lo = lax.bitcast_convert_type(xb[:, :c], jnp.uint32) >> 16
    hi = lax.bitcast_convert_type(xb[:, c:], jnp.uint32) & jnp.uint32(0xFFFF0000)
    return lo | hi


def _unpack_halves(u):
    lo = lax.bitcast_convert_type(u << 16, F32)
    hi = lax.bitcast_convert_type(u & jnp.uint32(0xFFFF0000), F32)
    return lo, hi


def _layernorm_rows(x, g, b):
    mu = jnp.mean(x, axis=-1, keepdims=True)
    xc = x - mu
    var = jnp.mean(xc * xc, axis=-1, keepdims=True)
    return xc * lax.rsqrt(var + LN_EPS) * g + b


def _in_proj_kernel(x_ref, w_ref, b_ref, o_ref):
    acc = jnp.dot(x_ref[...], w_ref[...], preferred_element_type=F32)
    o_ref[...] = acc + b_ref[...]


def in_proj(x_bf, w_bf, b, tm=IN_TM, tn=IN_TN):
    m, k = x_bf.shape
    n = w_bf.shape[1]
    tm = min(tm, m)
    assert m % tm == 0 and n % tn == 0
    return pl.pallas_call(
        _in_proj_kernel,
        grid=(m // tm, n // tn),
        in_specs=[pl.BlockSpec((tm, k), lambda i, j: (i, 0)),
                  pl.BlockSpec((k, tn), lambda i, j: (0, j)),
                  pl.BlockSpec((1, tn), lambda i, j: (0, j))],
        out_specs=pl.BlockSpec((tm, tn), lambda i, j: (i, j)),
        out_shape=jax.ShapeDtypeStruct((m, n), F32),
        compiler_params=_params("parallel", "parallel"),
        name="in_proj",
    )(x_bf, w_bf, b)


def _retention_kernel(cd_ref, q_ref, k_ref, v_ref, g_ref, cos_ref, sin_ref, decay_ref,
                      xi_ref, zeta_ref, gn_ref, o_ref, state_ref):
    @pl.when(pl.program_id(1) == 0)
    def _():
        state_ref[...] = jnp.zeros_like(state_ref)

    cos = cos_ref[...]
    sin = sin_ref[...]
    for h in range(RET_H):
        qs = slice(h * RET_DK, (h + 1) * RET_DK)
        vs = slice(h * RET_DV, (h + 1) * RET_DV)
        q = q_ref[0, :, qs]
        k = k_ref[0, :, qs]
        qr = q * cos + pltpu.roll(q, RET_DK // 2, 1) * sin
        kr = (k * cos + pltpu.roll(k, RET_DK // 2, 1) * sin) * (RET_DK ** -0.5)
        vb = v_ref[0, :, vs].astype(BF16)
        s = lax.dot_general(qr.astype(BF16), kr.astype(BF16), (((1,), (1,)), ((), ())),
                            preferred_element_type=F32)
        inner = (s * decay_ref[h]).astype(BF16)
        o = jnp.dot(inner, vb, preferred_element_type=F32)
        st = state_ref[h]
        o = o + jnp.dot((qr * xi_ref[h]).astype(BF16), st.astype(BF16),
                        preferred_element_type=F32)
        kz = (kr * zeta_ref[h]).astype(BF16)
        state_ref[h] = st * cd_ref[h] + lax.dot_general(
            kz, vb, (((0,), (0,)), ((), ())), preferred_element_type=F32)
        mu = jnp.mean(o, axis=-1, keepdims=True)
        oc = o - mu
        var = jnp.mean(oc * oc, axis=-1, keepdims=True)
        on = oc * lax.rsqrt(var + LN_EPS) * gn_ref[:, vs]
        g = g_ref[0, :, vs]
        o_ref[0, :, vs] = (g * _sigmoid(g) * on).astype(BF16)


def _retention_tables(seq, chunk):
    h = jnp.arange(RET_H, dtype=F32)
    log_g = jnp.log1p(-jnp.exp2(-5.0 - h))
    i = jnp.arange(chunk, dtype=F32)
    diff = i[:, None] - i[None, :]
    decay = jnp.where(diff >= 0, jnp.exp(log_g[:, None, None] * jnp.maximum(diff, 0.0)), 0.0)
    xi = jnp.exp(log_g[:, None] * (i + 1.0))
    zeta = jnp.exp(log_g[:, None] * (chunk - 1.0 - i))
    xi = jnp.broadcast_to(xi[:, :, None], (RET_H, chunk, RET_DK))
    zeta = jnp.broadcast_to(zeta[:, :, None], (RET_H, chunk, RET_DK))
    cd = jnp.exp(log_g * chunk)
    half = RET_DK // 2
    inv = 1.0 / (ROPE_BASE ** (jnp.arange(half, dtype=F32) / half))
    ang = jnp.arange(seq, dtype=F32)[:, None] * inv[None, :]
    cos = jnp.cos(ang)
    sin = jnp.sin(ang)
    cos2 = jnp.concatenate([cos, cos], axis=-1)
    sin2 = jnp.concatenate([-sin, sin], axis=-1)
    return cd, cos2, sin2, decay, xi, zeta


def retention(z, gn_g, tables):
    b, s, _ = z.shape
    cd, cos2, sin2, decay, xi, zeta = tables
    L = decay.shape[1]
    qw = RET_H * RET_DK
    vw = RET_H * RET_DV
    return pl.pallas_call(
        _retention_kernel,
        grid=(b, s // L),
        in_specs=[pl.BlockSpec(memory_space=pltpu.SMEM),
                  pl.BlockSpec((1, L, qw), lambda bi, n: (bi, n, 0)),
                  pl.BlockSpec((1, L, qw), lambda bi, n: (bi, n, 1)),
                  pl.BlockSpec((1, L, vw), lambda bi, n: (bi, n, 1)),
                  pl.BlockSpec((1, L, vw), lambda bi, n: (bi, n, 2)),
                  pl.BlockSpec((L, RET_DK), lambda bi, n: (n, 0)),
                  pl.BlockSpec((L, RET_DK), lambda bi, n: (n, 0)),
                  pl.BlockSpec((RET_H, L, L), lambda bi, n: (0, 0, 0)),
                  pl.BlockSpec((RET_H, L, RET_DK), lambda bi, n: (0, 0, 0)),
                  pl.BlockSpec((RET_H, L, RET_DK), lambda bi, n: (0, 0, 0)),
                  pl.BlockSpec((1, vw), lambda bi, n: (0, 0))],
        out_specs=pl.BlockSpec((1, L, vw), lambda bi, n: (bi, n, 0)),
        out_shape=jax.ShapeDtypeStruct((b, s, vw), BF16),
        scratch_shapes=[pltpu.VMEM((RET_H, RET_DK, RET_DV), F32)],
        compiler_params=_params("parallel", "arbitrary"),
        name="retention",
    )(cd, z, z, z, z, cos2, sin2, decay, xi, zeta, gn_g)


def _fox_cumsum_kernel(f_ref, tri_ref, o_ref):
    s = f_ref.shape[1]
    tri = tri_ref[...]
    carry = jnp.zeros((1, LANES), F32)
    for i in range(s // CUM_BLK):
        f = f_ref[0, i * CUM_BLK:(i + 1) * CUM_BLK, :]
        ls = jnp.minimum(f, 0.0) - jnp.log1p(jnp.exp(-jnp.abs(f)))
        hi = ls.astype(BF16)
        r1 = ls - hi.astype(F32)
        mid = r1.astype(BF16)
        lo = (r1 - mid.astype(F32)).astype(BF16)
        c = (jnp.dot(tri, hi, preferred_element_type=F32)
             + jnp.dot(tri, mid, preferred_element_type=F32)
             + jnp.dot(tri, lo, preferred_element_type=F32)) + carry
        carry = c[CUM_BLK - 1:CUM_BLK, :]
        o_ref[0, :, i * CUM_BLK:(i + 1) * CUM_BLK] = (-c).T[:FOX_H, :]


def fox_cumsum(z, ff_block):
    b, s, _ = z.shape
    r = jnp.arange(CUM_BLK)
    tri = (r[:, None] >= r[None, :]).astype(BF16)
    return pl.pallas_call(
        _fox_cumsum_kernel,
        grid=(b,),
        in_specs=[pl.BlockSpec((1, s, LANES), lambda bi: (bi, 0, ff_block)),
                  pl.BlockSpec((CUM_BLK, CUM_BLK), lambda bi: (0, 0))],
        out_specs=pl.BlockSpec((1, FOX_H, s), lambda bi: (bi, 0, 0)),
        out_shape=jax.ShapeDtypeStruct((b, FOX_H, s), F32),
        compiler_params=_params("parallel"),
        name="fox_cumsum",
    )(z, tri)


def _fox_attn_kernel(q_ref, k_ref, v_ref, nc_ref, o_ref, kt_ref, vx_ref, qb_ref, m_ref, acc_ref):
    qi = pl.program_id(2)
    s_len = k_ref.shape[1]

    def head(g):
        return slice(g * FOX_DH, (g + 1) * FOX_DH)

    @pl.when(qi == 0)
    def _():
        for g in range(FOX_HG):
            for c in range(s_len // FOX_TK):
                rows = slice(c * FOX_TK, (c + 1) * FOX_TK)
                kt_ref[g, :, rows] = k_ref[0, rows, head(g)].T.astype(BF16)
                vx_ref[g, rows, :FOX_DH] = v_ref[0, rows, head(g)].astype(BF16)
            vx_ref[g, :, FOX_DH:] = jnp.ones((s_len, FOX_DH), BF16)

    for g in range(FOX_HG):
        qb_ref[g] = (q_ref[0, :, head(g)] * (FOX_DH ** -0.5 * LOG2E)).astype(BF16)
    m_ref[...] = jnp.full_like(m_ref, NEG_BIG)
    acc_ref[...] = jnp.zeros_like(acc_ref)

    def tile(j, diag):
        k0 = pl.multiple_of(j * FOX_TK, FOX_TK)
        rows = slice(0 if diag is None else diag * FOX_TK, FOX_TQ)
        for g in range(FOX_HG):
            s = jnp.dot(qb_ref[g, rows, :], kt_ref[g, :, pl.ds(k0, FOX_TK)],
                        preferred_element_type=F32)
            s = s + nc_ref[0, g, :, pl.ds(k0, FOX_TK)] * LOG2E
            if diag is not None:
                row = lax.broadcasted_iota(jnp.int32, s.shape, 0)
                col = lax.broadcasted_iota(jnp.int32, s.shape, 1)
                s = jnp.where(col <= row, s, NEG_BIG)
            m_prev = m_ref[g, rows, :]
            m_new = jnp.maximum(m_prev, jnp.max(s, axis=-1, keepdims=True))
            alpha = jnp.exp2(m_prev - m_new)
            p = jnp.concatenate(
                [jnp.exp2(s[:, c * LANES:(c + 1) * LANES] - m_new)
                 for c in range(FOX_TK // LANES)], axis=1).astype(BF16)
            pv = jnp.dot(p, vx_ref[g, pl.ds(k0, FOX_TK), :], preferred_element_type=F32)
            acc_ref[g, rows, :] = jnp.concatenate([alpha, alpha], axis=1) * acc_ref[g, rows, :] + pv
            m_ref[g, rows, :] = m_new

    def body(j, c):
        tile(j, None)
        return c

    per_q = FOX_TQ // FOX_TK
    n_full = qi * per_q
    lax.fori_loop(0, n_full, body, 0)
    for r in range(per_q):
        tile(n_full + r, r)
    for g in range(FOX_HG):
        o_ref[0, :, head(g)] = (acc_ref[g, :, :FOX_DH] / acc_ref[g, :, FOX_DH:]).astype(BF16)


def fox_attention(z, negc, q_blk, k_blk, v_blk):
    b, s, _ = z.shape
    assert FOX_TQ % FOX_TK == 0 and FOX_DH == LANES and FOX_H % FOX_HG == 0
    gw = FOX_HG * FOX_DH
    negc4 = negc.reshape(b, FOX_H, 1, s)
    qg, kg, vg = q_blk // FOX_HG, k_blk // FOX_HG, v_blk // FOX_HG
    return pl.pallas_call(
        _fox_attn_kernel,
        grid=(b, FOX_H // FOX_HG, s // FOX_TQ),
        in_specs=[pl.BlockSpec((1, FOX_TQ, gw), lambda bi, h, qi: (bi, qi, qg + h)),
                  pl.BlockSpec((1, s, gw), lambda bi, h, qi: (bi, 0, kg + h)),
                  pl.BlockSpec((1, s, gw), lambda bi, h, qi: (bi, 0, vg + h)),
                  pl.BlockSpec((1, FOX_HG, 1, s), lambda bi, h, qi: (bi, h, 0, 0))],
        out_specs=pl.BlockSpec((1, FOX_TQ, gw), lambda bi, h, qi: (bi, qi, h)),
        out_shape=jax.ShapeDtypeStruct((b, s, FOX_H * FOX_DH), BF16),
        scratch_shapes=[pltpu.VMEM((FOX_HG, FOX_DH, s), BF16),
                        pltpu.VMEM((FOX_HG, s, 2 * FOX_DH), BF16),
                        pltpu.VMEM((FOX_HG, FOX_TQ, FOX_DH), BF16),
                        pltpu.VMEM((FOX_HG, FOX_TQ, LANES), F32),
                        pltpu.VMEM((FOX_HG, FOX_TQ, 2 * FOX_DH), F32)],
        compiler_params=_params("parallel", "parallel", "arbitrary"),
        name="fox_attn",
    )(z, z, z, negc4)


def _conv_kernel(a_ref, g_ref, ap_ref, gp_ref, w_ref, cb_ref, lg_ref, lb_ref, o_ref, y_ref,
                 c_ref):
    ts = a_ref.shape[1]
    c_all = a_ref.shape[2]
    prev = ap_ref[0] * _sigmoid(gp_ref[0])
    y_ref[0:CONV_HALO, :] = jnp.where(pl.program_id(1) > 0, prev, 0.0)
    y_ref[CONV_HALO:CONV_HALO + ts, :] = a_ref[0] * _sigmoid(g_ref[0])
    lead = CONV_HALO - (CONV_W - 1)

    n_cb = c_all // CONV_CB
    win_rows = CONV_RB + CONV_HALO

    def block(idx, carry):
        r0 = pl.multiple_of((idx // n_cb) * CONV_RB, CONV_RB)
        c0 = pl.multiple_of((idx % n_cb) * CONV_CB, CONV_CB)
        win = y_ref[pl.ds(r0, win_rows), pl.ds(c0, CONV_CB)]
        acc = jnp.zeros((CONV_RB, CONV_CB), F32)
        for ph in range(SUBLANES):
            rot = win if ph == 0 else pltpu.roll(win, win_rows - ph, 0)
            for w in range(CONV_W):
                if (w + lead) % SUBLANES == ph:
                    a0 = (w + lead) - ph
                    acc = acc + w_ref[w:w + 1, pl.ds(c0, CONV_CB)] * rot[a0:a0 + CONV_RB]
        c_ref[pl.ds(r0, CONV_RB), pl.ds(c0, CONV_CB)] = acc + cb_ref[:, pl.ds(c0, CONV_CB)]
        return carry

    lax.fori_loop(0, (ts // CONV_RB) * n_cb, block, 0)
    y = _layernorm_rows(c_ref[...], lg_ref[...], lb_ref[...])
    o_ref[0] = (y * _sigmoid(y)).astype(BF16)


def conformer_conv(z, a_blk, g_blk, conv_w, conv_b, ln_g, ln_b):
    b, s, _ = z.shape
    c = conv_w.shape[1]
    ts = min(CONV_TS, s)
    hb = ts // CONV_HALO

    def halo(col):
        return lambda bi, i: (bi, jnp.maximum(i * hb - 1, 0), col)

    return pl.pallas_call(
        _conv_kernel,
        grid=(b, s // ts),
        in_specs=[pl.BlockSpec((1, ts, c), lambda bi, i: (bi, i, a_blk)),
                  pl.BlockSpec((1, ts, c), lambda bi, i: (bi, i, g_blk)),
                  pl.BlockSpec((1, CONV_HALO, c), halo(a_blk)),
                  pl.BlockSpec((1, CONV_HALO, c), halo(g_blk)),
                  pl.BlockSpec((CONV_W, c), lambda bi, i: (0, 0)),
                  pl.BlockSpec((1, c), lambda bi, i: (0, 0)),
                  pl.BlockSpec((1, c), lambda bi, i: (0, 0)),
                  pl.BlockSpec((1, c), lambda bi, i: (0, 0))],
        out_specs=pl.BlockSpec((1, ts, c), lambda bi, i: (bi, i, 0)),
        out_shape=jax.ShapeDtypeStruct((b, s, c), BF16),
        scratch_shapes=[pltpu.VMEM((CONV_HALO + ts, c), F32),
                        pltpu.VMEM((ts, c), F32)],
        compiler_params=_params("parallel", "parallel"),
        name="conv",
    )(z, z, z, z, conv_w, conv_b, ln_g, ln_b)


def _merge_kernel(a_ref, f_ref, c_ref, wa_ref, wf_ref, wc_ref, g0_ref, g1_ref, g2_ref, o_ref):
    ya = jnp.dot(a_ref[...], wa_ref[...], preferred_element_type=F32)
    m = _sigmoid(g0_ref[...]) * ya
    yf = jnp.dot(f_ref[...], wf_ref[...], preferred_element_type=F32)
    m = m + _sigmoid(g1_ref[...]) * yf
    yc = jnp.dot(c_ref[...], wc_ref[...], preferred_element_type=F32)
    m = m + _sigmoid(g2_ref[...]) * yc
    o_ref[...] = m.astype(BF16)


def merge(ret_o, fox_o, conv_o, w_ret, w_fox, w_conv, z2, gl_blk, tm=1024, tn=256):
    m, k = ret_o.shape
    n = w_ret.shape[1]
    tm = min(tm, m)
    nb = n // tn
    act = pl.BlockSpec((tm, k), lambda i, j: (i, 0))
    wsp = pl.BlockSpec((k, tn), lambda i, j: (0, j))

    def gate(br):
        return pl.BlockSpec((tm, tn), lambda i, j: (i, gl_blk * nb + br * nb + j))

    return pl.pallas_call(
        _merge_kernel,
        grid=(m // tm, nb),
        in_specs=[act, act, act, wsp, wsp, wsp, gate(0), gate(1), gate(2)],
        out_specs=pl.BlockSpec((tm, tn), lambda i, j: (i, j)),
        out_shape=jax.ShapeDtypeStruct((m, n), BF16),
        compiler_params=_params("parallel", "parallel"),
        name="merge",
    )(ret_o, fox_o, conv_o, w_ret, w_fox, w_conv, z2, z2, z2)


def _out_ln_kernel(x_ref, w_ref, h_ref, g_ref, b_ref, o_ref, ob_ref, op_ref):
    y = jnp.dot(x_ref[...], w_ref[...], preferred_element_type=F32)
    out = _layernorm_rows(ALPHA * h_ref[...] + y, g_ref[...], b_ref[...])
    o_ref[...] = out
    ob_ref[...] = out.astype(BF16)
    op_ref[...] = _pack_halves(out)


def out_ln(merged, w_out, h, g, b, tm=512):
    m, k = merged.shape
    n = w_out.shape[1]
    tm = min(tm, m)
    return pl.pallas_call(
        _out_ln_kernel,
        grid=(m // tm,),
        in_specs=[pl.BlockSpec((tm, k), lambda i: (i, 0)),
                  pl.BlockSpec((k, n), lambda i: (0, 0)),
                  pl.BlockSpec((tm, n), lambda i: (i, 0)),
                  pl.BlockSpec((1, n), lambda i: (0, 0)),
                  pl.BlockSpec((1, n), lambda i: (0, 0))],
        out_specs=[pl.BlockSpec((tm, n), lambda i: (i, 0)),
                   pl.BlockSpec((tm, n), lambda i: (i, 0)),
                   pl.BlockSpec((tm, n // 2), lambda i: (i, 0))],
        out_shape=[jax.ShapeDtypeStruct((m, n), F32), jax.ShapeDtypeStruct((m, n), BF16),
                   jax.ShapeDtypeStruct((m, n // 2), jnp.uint32)],
        compiler_params=_params("parallel"),
        name="out_ln",
    )(merged, w_out, h, g, b)


ROUTE_COLS = 8


def _router_kernel(h_ref, w_ref, b_ref, tri_ref, idx_ref, wt_ref, rank_ref, cnt_ref, carry_ref):
    @pl.when(pl.program_id(0) == 0)
    def _():
        carry_ref[...] = jnp.zeros_like(carry_ref)

    h = h_ref[...]
    hh = h.astype(BF16)
    hl = (h - hh.astype(F32)).astype(BF16)
    w = w_ref[...]
    wh = w.astype(BF16)
    wl = (w - wh.astype(F32)).astype(BF16)
    logits = (jnp.dot(hh, wh, preferred_element_type=F32)
              + jnp.dot(hl, wh, preferred_element_type=F32)
              + jnp.dot(hh, wl, preferred_element_type=F32))
    scores = _sigmoid(logits)
    tm, ne = scores.shape
    sel = scores + b_ref[...]
    lane = lax.broadcasted_iota(jnp.int32, (tm, ne), 1).astype(F32)
    mask = jnp.zeros((tm, ne), F32)
    onehots, idxs, wts = [], [], []
    for _ in range(TOP_K):
        mx = jnp.max(sel, axis=-1, keepdims=True)
        ik = jnp.min(jnp.where(sel == mx, lane, float(ne)), axis=-1, keepdims=True)
        oh = lane == ik
        wts.append(jnp.sum(jnp.where(oh, scores, 0.0), axis=-1, keepdims=True))
        idxs.append(ik)
        onehots.append(oh)
        sel = jnp.where(oh, -jnp.inf, sel)
        mask = jnp.where(oh, 1.0, mask)
    wsum = wts[0]
    for t in wts[1:]:
        wsum = wsum + t
    cnt = jnp.dot(tri_ref[...], mask.astype(BF16), preferred_element_type=F32) + carry_ref[...]
    col = lax.broadcasted_iota(jnp.int32, (tm, ROUTE_COLS), 1)
    idx_o = jnp.zeros((tm, ROUTE_COLS), F32)
    wt_o = jnp.zeros((tm, ROUTE_COLS), F32)
    rank_o = jnp.zeros((tm, ROUTE_COLS), F32)
    for kk in range(TOP_K):
        rk = jnp.sum(jnp.where(onehots[kk], cnt, 0.0), axis=-1, keepdims=True)
        idx_o = jnp.where(col == kk, idxs[kk], idx_o)
        wt_o = jnp.where(col == kk, wts[kk] / wsum * ROUTED_SCALE, wt_o)
        rank_o = jnp.where(col == kk, rk, rank_o)
    idx_ref[...] = idx_o.astype(jnp.int32)
    wt_ref[...] = wt_o
    rank_ref[...] = rank_o.astype(jnp.int32)
    carry_ref[...] = carry_ref[...] + jnp.sum(mask, axis=0, keepdims=True)
    cnt_ref[...] = carry_ref[...]


def router(h, w_router, b_router, tm=512):
    m, k = h.shape
    ne = w_router.shape[1]
    tm = min(tm, m)
    r = jnp.arange(tm)
    tri = (r[:, None] > r[None, :]).astype(BF16)
    small = pl.BlockSpec((tm, ROUTE_COLS), lambda i: (i, 0))
    return pl.pallas_call(
        _router_kernel,
        grid=(m // tm,),
        in_specs=[pl.BlockSpec((tm, k), lambda i: (i, 0)),
                  pl.BlockSpec((k, ne), lambda i: (0, 0)),
                  pl.BlockSpec((1, ne), lambda i: (0, 0)),
                  pl.BlockSpec((tm, tm), lambda i: (0, 0))],
        out_specs=[small, small, small, pl.BlockSpec((1, ne), lambda i: (0, 0))],
        out_shape=[jax.ShapeDtypeStruct((m, ROUTE_COLS), jnp.int32),
                   jax.ShapeDtypeStruct((m, ROUTE_COLS), F32),
                   jax.ShapeDtypeStruct((m, ROUTE_COLS), jnp.int32),
                   jax.ShapeDtypeStruct((1, ne), F32)],
        scratch_shapes=[pltpu.VMEM((1, ne), F32)],
        compiler_params=_params("arbitrary"),
        name="router",
    )(h, w_router, b_router, tri)


def _experts_kernel(be_ref, bv_ref, li_ref, x_ref, wg_ref, wu_ref, wd_ref, o_ref, wgb_ref,
                    wub_ref, wdb_ref):
    i = pl.program_id(0)

    @pl.when(jnp.logical_or(i == 0, be_ref[i] != be_ref[jnp.maximum(i - 1, 0)]))
    def _():
        wgb_ref[...] = wg_ref[0, 0].astype(BF16)
        wub_ref[...] = wu_ref[0, 0].astype(BF16)
        wdb_ref[...] = wd_ref[0, 0].astype(BF16)

    @pl.when(bv_ref[i] > 0)
    def _():
        half = x_ref.shape[1]
        xa, xb = _unpack_halves(x_ref[...])
        xa = xa.astype(BF16)
        xb = xb.astype(BF16)
        g = (jnp.dot(xa, wgb_ref[:half, :], preferred_element_type=F32)
             + jnp.dot(xb, wgb_ref[half:, :], preferred_element_type=F32))
        u = (jnp.dot(xa, wub_ref[:half, :], preferred_element_type=F32)
             + jnp.dot(xb, wub_ref[half:, :], preferred_element_type=F32))
        a = (g * _sigmoid(g) * u).astype(BF16)
        o_ref[...] = _pack_halves(jnp.dot(a, wdb_ref[...], preferred_element_type=F32))

    @pl.when(bv_ref[i] == 0)
    def _():
        o_ref[...] = jnp.zeros_like(o_ref)


def experts(xs, block_e, block_valid, layer, w_gate, w_up, w_down):
    r, half = xs.shape
    d = 2 * half
    ff = w_gate.shape[3]
    nblk = r // MOE_BLOCK
    return pl.pallas_call(
        _experts_kernel,
        grid_spec=pltpu.PrefetchScalarGridSpec(
            num_scalar_prefetch=3,
            grid=(nblk,),
            in_specs=[pl.BlockSpec((MOE_BLOCK, half), lambda i, be, bv, li: (i, 0)),
                      pl.BlockSpec((1, 1, d, ff), lambda i, be, bv, li: (li[0], be[i], 0, 0)),
                      pl.BlockSpec((1, 1, d, ff), lambda i, be, bv, li: (li[0], be[i], 0, 0)),
                      pl.BlockSpec((1, 1, ff, d), lambda i, be, bv, li: (li[0], be[i], 0, 0))],
            out_specs=pl.BlockSpec((MOE_BLOCK, half), lambda i, be, bv, li: (i, 0)),
            scratch_shapes=[pltpu.VMEM((d, ff), BF16), pltpu.VMEM((d, ff), BF16),
                            pltpu.VMEM((ff, d), BF16)]),
        out_shape=jax.ShapeDtypeStruct((r, half), jnp.uint32),
        compiler_params=_params("arbitrary"),
        name="experts",
    )(block_e, block_valid, layer, xs, w_gate, w_up, w_down)


def _shared_ln_kernel(xb_ref, h_ref, yg_ref, wt_ref, wgu_ref, wd_ref, g_ref, b_ref, o_ref,
                      ob_ref):
    ff = wd_ref.shape[0]
    gu = jnp.dot(xb_ref[...], wgu_ref[...], preferred_element_type=F32)
    g = gu[:, :ff]
    a = (g * _sigmoid(g) * gu[:, ff:]).astype(BF16)
    shared = jnp.dot(a, wd_ref[...], preferred_element_type=F32)
    wt = wt_ref[...]
    r_lo = r_hi = None
    for kk in range(TOP_K):
        lo, hi = _unpack_halves(yg_ref[kk])
        wk = wt[:, kk:kk + 1]
        r_lo = wk * lo if r_lo is None else r_lo + wk * lo
        r_hi = wk * hi if r_hi is None else r_hi + wk * hi
    routed = jnp.concatenate([r_lo, r_hi], axis=1)
    out = _layernorm_rows(ALPHA * h_ref[...] + (routed + shared), g_ref[...], b_ref[...])
    o_ref[...] = out
    ob_ref[...] = out.astype(BF16)


def shared_ln(h_bf, h, yg, wt, w_gu, w_down, g, b, tm=256):
    m, d = h.shape
    ff2 = w_gu.shape[1]
    tm = min(tm, m)
    row = pl.BlockSpec((tm, d), lambda i: (i, 0))
    vec = pl.BlockSpec((1, d), lambda i: (0, 0))
    return pl.pallas_call(
        _shared_ln_kernel,
        grid=(m // tm,),
        in_specs=[row, row,
                  pl.BlockSpec((TOP_K, tm, d // 2), lambda i: (0, i, 0)),
                  pl.BlockSpec((tm, ROUTE_COLS), lambda i: (i, 0)),
                  pl.BlockSpec((d, ff2), lambda i: (0, 0)),
                  pl.BlockSpec((ff2 // 2, d), lambda i: (0, 0)),
                  vec, vec],
        out_specs=[row, row],
        out_shape=[jax.ShapeDtypeStruct((m, d), F32), jax.ShapeDtypeStruct((m, d), BF16)],
        compiler_params=_params("parallel"),
        name="shared_ln",
    )(h_bf, h, yg, wt, w_gu, w_down, g, b)


def _mixer_layout(d):
    rq = RET_H * RET_DK
    rv = RET_H * RET_DV
    fq = FOX_H * FOX_DH
    main = 2 * rq + 2 * rv + 3 * fq
    ff0 = main
    cu0 = ff0 + FOX_H
    gl0 = cu0 + 2 * d
    end = gl0 + N_BRANCH * d
    return dict(rq=rq, rv=rv, fq=fq, main=main, ff0=ff0, cu0=cu0, gl0=gl0, end=end)


def _prep_in_weights(w_in, b_in, d, tn):
    lay = _mixer_layout(d)
    body = lay["main"] + 2 * d + N_BRANCH * d
    pad = tn - FOX_H
    w = jnp.concatenate([w_in[:, :lay["main"]], w_in[:, lay["cu0"]:lay["end"]],
                         w_in[:, lay["ff0"]:lay["cu0"]],
                         jnp.zeros((w_in.shape[0], pad), w_in.dtype)], axis=1).astype(BF16)
    bb = jnp.concatenate([b_in[:lay["main"]], b_in[lay["cu0"]:lay["end"]],
                          b_in[lay["ff0"]:lay["cu0"]], jnp.zeros((pad,), b_in.dtype)])
    return w, bb.reshape(1, -1).astype(F32), body


def _moe_sublayer(h, h_bf, h_pk, w_router, b_router, layer, w_gate, w_up, w_down, w_sh_gu,
                  w_sh_down, g, b):
    t, d = h.shape
    ne = w_router.shape[1]
    idx8, wt8, rank8, cnt = router(h, w_router, b_router.reshape(1, ne))
    idx = idx8[:, :TOP_K]
    rank = rank8[:, :TOP_K]
    sizes = cnt[0].astype(jnp.int32)
    a = t * TOP_K
    n_blocks = -(-a // MOE_BLOCK) + ne
    r = n_blocks * MOE_BLOCK
    padded = (sizes + MOE_BLOCK - 1) // MOE_BLOCK * MOE_BLOCK
    pad_end = jnp.cumsum(padded)
    pad_start = pad_end - padded
    dest = pad_start[idx] + rank
    shift = (t + MOE_BLOCK).bit_length()
    tok = jnp.arange(t, dtype=jnp.int32)[:, None]
    real_keys = ((idx << shift) | tok).reshape(a)
    slot = jnp.arange(r - a, dtype=jnp.int32)
    slot_e = jnp.minimum(slot // MOE_BLOCK, ne - 1)
    slot_q = slot % MOE_BLOCK
    needed = (slot < ne * MOE_BLOCK) & (slot_q < (padded - sizes)[slot_e])
    pad_keys = jnp.where(needed, (slot_e << shift) | (t + slot_q), (ne << shift) + slot)
    order = jnp.sort(jnp.concatenate([real_keys, pad_keys]))
    row_tok = (order & ((1 << shift) - 1)) % t
    blk0 = jnp.arange(n_blocks, dtype=jnp.int32) * MOE_BLOCK
    block_e = jnp.minimum(jnp.sum((pad_end[None, :] <= blk0[:, None]).astype(jnp.int32), axis=1),
                          ne - 1)
    block_valid = (blk0 < pad_end[-1]).astype(jnp.int32)
    xs = h_pk[row_tok]
    ys = experts(xs, block_e, block_valid, layer, w_gate, w_up, w_down)
    yg = ys[dest.T.reshape(a)].reshape(TOP_K, t, d // 2)
    return shared_ln(h_bf, h, yg, wt8, w_sh_gu, w_sh_down, g, b)


def kernel(x, w_in, b_in, ret_gn_g, conv_w, conv_b, conv_ln_g, conv_ln_b, w_ret_o, w_fox_o,
           w_conv_o, w_out, ln1_g, ln1_b, w_router, b_router, w_exp_gate, w_exp_up, w_exp_down,
           w_sh_gate, w_sh_up, w_sh_down, ln2_g, ln2_b):
    bsz, seq, d = x.shape
    t = bsz * seq
    in_tn = IN_PAD
    lay = _mixer_layout(d)
    chunk = min(RET_CHUNK, seq)
    tables = _retention_tables(seq, chunk)
    fq0 = 2 * lay["rq"] + 2 * lay["rv"]

    def row(v):
        return v.reshape(1, -1)

    def layer(carry, p):
        h, h_bf = carry
        w_cat, b_cat, body = _prep_in_weights(p["w_in"], p["b_in"], d, in_tn)
        z2 = in_proj(h_bf, w_cat, b_cat)
        z = z2.reshape(bsz, seq, -1)
        ret_o = retention(z, row(p["ret_gn_g"]), tables)
        negc = fox_cumsum(z, body // LANES)
        fox_o = fox_attention(z, negc, fq0 // FOX_DH, (fq0 + lay["fq"]) // FOX_DH,
                              (fq0 + 2 * lay["fq"]) // FOX_DH)
        conv_o = conformer_conv(z, lay["main"] // d, lay["main"] // d + 1, p["conv_w"],
                                row(p["conv_b"]), row(p["conv_ln_g"]), row(p["conv_ln_b"]))
        merged = merge(ret_o.reshape(t, -1), fox_o.reshape(t, -1), conv_o.reshape(t, -1),
                       p["w_ret_o"].astype(BF16), p["w_fox_o"].astype(BF16),
                       p["w_conv_o"].astype(BF16), z2, (lay["main"] + 2 * d) // d)
        h, h_bf, h_pk = out_ln(merged, p["w_out"].astype(BF16), h, row(p["ln1_g"]),
                               row(p["ln1_b"]))
        w_sh_gu = jnp.concatenate([p["w_sh_gate"], p["w_sh_up"]], axis=-1).astype(BF16)
        h, h_bf = _moe_sublayer(h, h_bf, h_pk, p["w_router"], p["b_router"], p["layer"],
                                w_exp_gate, w_exp_up, w_exp_down, w_sh_gu,
                                p["w_sh_down"].astype(BF16), row(p["ln2_g"]), row(p["ln2_b"]))
        return (h, h_bf), None

    depth = w_in.shape[0]
    params = dict(w_in=w_in, b_in=b_in, ret_gn_g=ret_gn_g, conv_w=conv_w, conv_b=conv_b,
                  conv_ln_g=conv_ln_g, conv_ln_b=conv_ln_b, w_ret_o=w_ret_o, w_fox_o=w_fox_o,
                  w_conv_o=w_conv_o, w_out=w_out, ln1_g=ln1_g, ln1_b=ln1_b, w_router=w_router,
                  b_router=b_router, w_sh_gate=w_sh_gate, w_sh_up=w_sh_up,
                  w_sh_down=w_sh_down, ln2_g=ln2_g, ln2_b=ln2_b,
                  layer=jnp.arange(depth, dtype=jnp.int32).reshape(depth, 1))
    h0 = x.reshape(t, d)
    (h, _), _ = lax.scan(layer, (h0, h0.astype(BF16)), params)
    return h.reshape(bsz, seq, d)
```

```python
import functools

import jax
import jax.numpy as jnp
from jax import lax
from jax.experimental import pallas as pl
from jax.experimental.pallas import tpu as pltpu

F32 = jnp.float32
BF16 = jnp.bfloat16

RET_H = 8
RET_DK = 128
RET_DV = 256
ROPE_BASE = 10000.0
FOX_H = 16
FOX_DH = 128
CONV_W = 31
N_BRANCH = 3
TOP_K = 6
ROUTED_SCALE = 2.5
DEPTH_FOR_NORM = 4
ALPHA = (2 * DEPTH_FOR_NORM) ** 0.25
LN_EPS = 1e-5

LANES = 128
SUBLANES = 8
VMEM_LIMIT = 56 * 1024 * 1024

IN_PAD = 512
IN_TM = 1024
IN_TN = 1536
RET_CHUNK = 256
FOX_TQ = 2048
FOX_TK = 512
FOX_HG = 1
CUM_BLK = 256
CONV_TS = 256
CONV_HALO = 32
CONV_RB = 256
CONV_CB = 128
MOE_BLOCK = 512
NEG_BIG = -1e30
LOG2E = 1.4426950408889634


def _params(*sem):
    return pltpu.CompilerParams(dimension_semantics=sem, vmem_limit_bytes=VMEM_LIMIT)


def _sigmoid(x):
    return 1.0 / (1.0 + jnp.exp(-x))


def _pack_halves(x):
    c = x.shape[1] // 2
    xb = x.astype(BF16).astype(F32)
    lo = lax.bitcast_convert_type(xb[:, :c], jnp.uint32) >> 16
    hi = lax.bitcast_convert_type(xb[:, c:], jnp.uint32) & jnp.uint32(0xFFFF0000)
    return lo | hi


def _unpack_halves(u):
    lo = lax.bitcast_convert_type(u << 16, F32)
    hi = lax.bitcast_convert_type(u & jnp.uint32(0xFFFF0000), F32)
    return lo, hi


def _layernorm_rows(x, g, b):
    mu = jnp.mean(x, axis=-1, keepdims=True)
    xc = x - mu
    var = jnp.mean(xc * xc, axis=-1, keepdims=True)
    return xc * lax.rsqrt(var + LN_EPS) * g + b


def _in_proj_kernel(li_ref, x_ref, w_ref, b_ref, o_ref):
    acc = jnp.dot(x_ref[...], w_ref[0], preferred_element_type=F32)
    o_ref[...] = acc + b_ref[0]


def in_proj(x_bf, layer, w_all, b_all, tm=IN_TM, tn=IN_TN):
    m, k = x_bf.shape
    n = w_all.shape[2]
    tm = min(tm, m)
    assert m % tm == 0 and n % tn == 0
    return pl.pallas_call(
        _in_proj_kernel,
        grid_spec=pltpu.PrefetchScalarGridSpec(
            num_scalar_prefetch=1,
            grid=(m // tm, n // tn),
            in_specs=[pl.BlockSpec((tm, k), lambda i, j, li: (i, 0)),
                      pl.BlockSpec((1, k, tn), lambda i, j, li: (li[0], 0, j)),
                      pl.BlockSpec((1, 1, tn), lambda i, j, li: (li[0], 0, j))],
            out_specs=pl.BlockSpec((tm, tn), lambda i, j, li: (i, j))),
        out_shape=jax.ShapeDtypeStruct((m, n), F32),
        compiler_params=_params("parallel", "parallel"),
        name="in_proj",
    )(layer, x_bf, w_all, b_all)


def _retention_kernel(cd_ref, q_ref, k_ref, v_ref, g_ref, cos_ref, sin_ref, decay_ref,
                      xi_ref, zeta_ref, gn_ref, o_ref, state_ref):
    @pl.when(pl.program_id(1) == 0)
    def _():
        state_ref[...] = jnp.zeros_like(state_ref)

    cos = cos_ref[...]
    sin = sin_ref[...]
    for h in range(RET_H):
        qs = slice(h * RET_DK, (h + 1) * RET_DK)
        vs = slice(h * RET_DV, (h + 1) * RET_DV)
        q = q_ref[0, :, qs]
        k = k_ref[0, :, qs]
        qr = q * cos + pltpu.roll(q, RET_DK // 2, 1) * sin
        kr = (k * cos + pltpu.roll(k, RET_DK // 2, 1) * sin) * (RET_DK ** -0.5)
        vb = v_ref[0, :, vs].astype(BF16)
        s = lax.dot_general(qr.astype(BF16), kr.astype(BF16), (((1,), (1,)), ((), ())),
                            preferred_element_type=F32)
        inner = (s * decay_ref[h]).astype(BF16)
        o = jnp.dot(inner, vb, preferred_element_type=F32)
        st = state_ref[h]
        o = o + jnp.dot((qr * xi_ref[h]).astype(BF16), st.astype(BF16),
                        preferred_element_type=F32)
        kz = (kr * zeta_ref[h]).astype(BF16)
        state_ref[h] = st * cd_ref[h] + lax.dot_general(
            kz, vb, (((0,), (0,)), ((), ())), preferred_element_type=F32)
        mu = jnp.mean(o, axis=-1, keepdims=True)
        oc = o - mu
        var = jnp.mean(oc * oc, axis=-1, keepdims=True)
        on = oc * lax.rsqrt(var + LN_EPS) * gn_ref[:, vs]
        g = g_ref[0, :, vs]
        o_ref[0, :, vs] = (g * _sigmoid(g) * on).astype(BF16)


def _retention_tables(seq, chunk):
    h = jnp.arange(RET_H, dtype=F32)
    log_g = jnp.log1p(-jnp.exp2(-5.0 - h))
    i = jnp.arange(chunk, dtype=F32)
    diff = i[:, None] - i[None, :]
    decay = jnp.where(diff >= 0, jnp.exp(log_g[:, None, None] * jnp.maximum(diff, 0.0)), 0.0)
    xi = jnp.exp(log_g[:, None] * (i + 1.0))
    zeta = jnp.exp(log_g[:, None] * (chunk - 1.0 - i))
    xi = jnp.broadcast_to(xi[:, :, None], (RET_H, chunk, RET_DK))
    zeta = jnp.broadcast_to(zeta[:, :, None], (RET_H, chunk, RET_DK))
    cd = jnp.exp(log_g * chunk)
    half = RET_DK // 2
    inv = 1.0 / (ROPE_BASE ** (jnp.arange(half, dtype=F32) / half))
    ang = jnp.arange(seq, dtype=F32)[:, None] * inv[None, :]
    cos = jnp.cos(ang)
    sin = jnp.sin(ang)
    cos2 = jnp.concatenate([cos, cos], axis=-1)
    sin2 = jnp.concatenate([-sin, sin], axis=-1)
    return cd, cos2, sin2, decay, xi, zeta


def retention(z, gn_g, tables):
    b, s, _ = z.shape
    cd, cos2, sin2, decay, xi, zeta = tables
    L = decay.shape[1]
    qw = RET_H * RET_DK
    vw = RET_H * RET_DV
    return pl.pallas_call(
        _retention_kernel,
        grid=(b, s // L),
        in_specs=[pl.BlockSpec(memory_space=pltpu.SMEM),
                  pl.BlockSpec((1, L, qw), lambda bi, n: (bi, n, 0)),
                  pl.BlockSpec((1, L, qw), lambda bi, n: (bi, n, 1)),
                  pl.BlockSpec((1, L, vw), lambda bi, n: (bi, n, 1)),
                  pl.BlockSpec((1, L, vw), lambda bi, n: (bi, n, 2)),
                  pl.BlockSpec((L, RET_DK), lambda bi, n: (n, 0)),
                  pl.BlockSpec((L, RET_DK), lambda bi, n: (n, 0)),
                  pl.BlockSpec((RET_H, L, L), lambda bi, n: (0, 0, 0)),
                  pl.BlockSpec((RET_H, L, RET_DK), lambda bi, n: (0, 0, 0)),
                  pl.BlockSpec((RET_H, L, RET_DK), lambda bi, n: (0, 0, 0)),
                  pl.BlockSpec((1, vw), lambda bi, n: (0, 0))],
        out_specs=pl.BlockSpec((1, L, vw), lambda bi, n: (bi, n, 0)),
        out_shape=jax.ShapeDtypeStruct((b, s, vw), BF16),
        scratch_shapes=[pltpu.VMEM((RET_H, RET_DK, RET_DV), F32)],
        compiler_params=_params("parallel", "arbitrary"),
        name="retention",
    )(cd, z, z, z, z, cos2, sin2, decay, xi, zeta, gn_g)


def _fox_cumsum_kernel(f_ref, tri_ref, o_ref):
    s = f_ref.shape[1]
    tri = tri_ref[...]
    carry = jnp.zeros((1, LANES), F32)
    for i in range(s // CUM_BLK):
        f = f_ref[0, i * CUM_BLK:(i + 1) * CUM_BLK, :]
        ls = jnp.minimum(f, 0.0) - jnp.log1p(jnp.exp(-jnp.abs(f)))
        hi = ls.astype(BF16)
        r1 = ls - hi.astype(F32)
        mid = r1.astype(BF16)
        lo = (r1 - mid.astype(F32)).astype(BF16)
        c = (jnp.dot(tri, hi, preferred_element_type=F32)
             + jnp.dot(tri, mid, preferred_element_type=F32)
             + jnp.dot(tri, lo, preferred_element_type=F32)) + carry
        carry = c[CUM_BLK - 1:CUM_BLK, :]
        o_ref[0, :, i * CUM_BLK:(i + 1) * CUM_BLK] = (-c).T[:FOX_H, :]


def fox_cumsum(z, ff_block):
    b, s, _ = z.shape
    r = jnp.arange(CUM_BLK)
    tri = (r[:, None] >= r[None, :]).astype(BF16)
    return pl.pallas_call(
        _fox_cumsum_kernel,
        grid=(b,),
        in_specs=[pl.BlockSpec((1, s, LANES), lambda bi: (bi, 0, ff_block)),
                  pl.BlockSpec((CUM_BLK, CUM_BLK), lambda bi: (0, 0))],
        out_specs=pl.BlockSpec((1, FOX_H, s), lambda bi: (bi, 0, 0)),
        out_shape=jax.ShapeDtypeStruct((b, FOX_H, s), F32),
        compiler_params=_params("parallel"),
        name="fox_cumsum",
    )(z, tri)


def _fox_attn_kernel(q_ref, k_ref, v_ref, nc_ref, o_ref, kt_ref, vx_ref, qb_ref, m_ref, acc_ref):
    qi = pl.program_id(2)
    s_len = k_ref.shape[1]

    def head(g):
        return slice(g * FOX_DH, (g + 1) * FOX_DH)

    @pl.when(qi == 0)
    def _():
        for g in range(FOX_HG):
            for c in range(s_len // FOX_TK):
                rows = slice(c * FOX_TK, (c + 1) * FOX_TK)
                kt_ref[g, :, rows] = k_ref[0, rows, head(g)].T.astype(BF16)
                vx_ref[g, rows, :FOX_DH] = v_ref[0, rows, head(g)].astype(BF16)
            vx_ref[g, :, FOX_DH:] = jnp.ones((s_len, FOX_DH), BF16)

    for g in range(FOX_HG):
        qb_ref[g] = (q_ref[0, :, head(g)] * (FOX_DH ** -0.5 * LOG2E)).astype(BF16)
    m_ref[...] = jnp.full_like(m_ref, NEG_BIG)
    acc_ref[...] = jnp.zeros_like(acc_ref)

    def tile(j, diag):
        k0 = pl.multiple_of(j * FOX_TK, FOX_TK)
        rows = slice(0 if diag is None else diag * FOX_TK, FOX_TQ)
        for g in range(FOX_HG):
            s = jnp.dot(qb_ref[g, rows, :], kt_ref[g, :, pl.ds(k0, FOX_TK)],
                        preferred_element_type=F32)
            s = s + nc_ref[0, g, :, pl.ds(k0, FOX_TK)] * LOG2E
            if diag is not None:
                row = lax.broadcasted_iota(jnp.int32, s.shape, 0)
                col = lax.broadcasted_iota(jnp.int32, s.shape, 1)
                s = jnp.where(col <= row, s, NEG_BIG)
            m_prev = m_ref[g, rows, :]
            m_new = jnp.maximum(m_prev, jnp.max(s, axis=-1, keepdims=True))
            alpha = jnp.exp2(m_prev - m_new)
            p = jnp.concatenate(
                [jnp.exp2(s[:, c * LANES:(c + 1) * LANES] - m_new)
                 for c in range(FOX_TK // LANES)], axis=1).astype(BF16)
            pv = jnp.dot(p, vx_ref[g, pl.ds(k0, FOX_TK), :], preferred_element_type=F32)
            acc_ref[g, rows, :] = jnp.concatenate([alpha, alpha], axis=1) * acc_ref[g, rows, :] + pv
            m_ref[g, rows, :] = m_new

    def body(j, c):
        tile(j, None)
        return c

    per_q = FOX_TQ // FOX_TK
    n_full = qi * per_q
    lax.fori_loop(0, n_full, body, 0)
    for r in range(per_q):
        tile(n_full + r, r)
    for g in range(FOX_HG):
        o_ref[0, :, head(g)] = (acc_ref[g, :, :FOX_DH] / acc_ref[g, :, FOX_DH:]).astype(BF16)


def fox_attention(z, negc, q_blk, k_blk, v_blk):
    b, s, _ = z.shape
    assert FOX_TQ % FOX_TK == 0 and FOX_DH == LANES and FOX_H % FOX_HG == 0
    gw = FOX_HG * FOX_DH
    negc4 = negc.reshape(b, FOX_H, 1, s)
    qg, kg, vg = q_blk // FOX_HG, k_blk // FOX_HG, v_blk // FOX_HG
    return pl.pallas_call(
        _fox_attn_kernel,
        grid=(b, FOX_H // FOX_HG, s // FOX_TQ),
        in_specs=[pl.BlockSpec((1, FOX_TQ, gw), lambda bi, h, qi: (bi, qi, qg + h)),
                  pl.BlockSpec((1, s, gw), lambda bi, h, qi: (bi, 0, kg + h)),
                  pl.BlockSpec((1, s, gw), lambda bi, h, qi: (bi, 0, vg + h)),
                  pl.BlockSpec((1, FOX_HG, 1, s), lambda bi, h, qi: (bi, h, 0, 0))],
        out_specs=pl.BlockSpec((1, FOX_TQ, gw), lambda bi, h, qi: (bi, qi, h)),
        out_shape=jax.ShapeDtypeStruct((b, s, FOX_H * FOX_DH), BF16),
        scratch_shapes=[pltpu.VMEM((FOX_HG, FOX_DH, s), BF16),
                        pltpu.VMEM((FOX_HG, s, 2 * FOX_DH), BF16),
                        pltpu.VMEM((FOX_HG, FOX_TQ, FOX_DH), BF16),
                        pltpu.VMEM((FOX_HG, FOX_TQ, LANES), F32),
                        pltpu.VMEM((FOX_HG, FOX_TQ, 2 * FOX_DH), F32)],
        compiler_params=_params("parallel", "parallel", "arbitrary"),
        name="fox_attn",
    )(z, z, z, negc4)


def _conv_kernel(a_ref, g_ref, ap_ref, gp_ref, w_ref, cb_ref, lg_ref, lb_ref, o_ref, y_ref,
                 c_ref):
    ts = a_ref.shape[1]
    c_all = a_ref.shape[2]
    prev = ap_ref[0] * _sigmoid(gp_ref[0])
    y_ref[0:CONV_HALO, :] = jnp.where(pl.program_id(1) > 0, prev, 0.0)
    y_ref[CONV_HALO:CONV_HALO + ts, :] = a_ref[0] * _sigmoid(g_ref[0])
    lead = CONV_HALO - (CONV_W - 1)

    n_cb = c_all // CONV_CB
    win_rows = CONV_RB + CONV_HALO

    def block(idx, carry):
        r0 = pl.multiple_of((idx // n_cb) * CONV_RB, CONV_RB)
        c0 = pl.multiple_of((idx % n_cb) * CONV_CB, CONV_CB)
        win = y_ref[pl.ds(r0, win_rows), pl.ds(c0, CONV_CB)]
        acc = jnp.zeros((CONV_RB, CONV_CB), F32)
        for ph in range(SUBLANES):
            rot = win if ph == 0 else pltpu.roll(win, win_rows - ph, 0)
            for w in range(CONV_W):
                if (w + lead) % SUBLANES == ph:
                    a0 = (w + lead) - ph
                    acc = acc + w_ref[w:w + 1, pl.ds(c0, CONV_CB)] * rot[a0:a0 + CONV_RB]
        c_ref[pl.ds(r0, CONV_RB), pl.ds(c0, CONV_CB)] = acc + cb_ref[:, pl.ds(c0, CONV_CB)]
        return carry

    lax.fori_loop(0, (ts // CONV_RB) * n_cb, block, 0)
    y = _layernorm_rows(c_ref[...], lg_ref[...], lb_ref[...])
    o_ref[0] = (y * _sigmoid(y)).astype(BF16)


def conformer_conv(z, a_blk, g_blk, conv_w, conv_b, ln_g, ln_b):
    b, s, _ = z.shape
    c = conv_w.shape[1]
    ts = min(CONV_TS, s)
    hb = ts // CONV_HALO

    def halo(col):
        return lambda bi, i: (bi, jnp.maximum(i * hb - 1, 0), col)

    return pl.pallas_call(
        _conv_kernel,
        grid=(b, s // ts),
        in_specs=[pl.BlockSpec((1, ts, c), lambda bi, i: (bi, i, a_blk)),
                  pl.BlockSpec((1, ts, c), lambda bi, i: (bi, i, g_blk)),
                  pl.BlockSpec((1, CONV_HALO, c), halo(a_blk)),
                  pl.BlockSpec((1, CONV_HALO, c), halo(g_blk)),
                  pl.BlockSpec((CONV_W, c), lambda bi, i: (0, 0)),
                  pl.BlockSpec((1, c), lambda bi, i: (0, 0)),
                  pl.BlockSpec((1, c), lambda bi, i: (0, 0)),
                  pl.BlockSpec((1, c), lambda bi, i: (0, 0))],
        out_specs=pl.BlockSpec((1, ts, c), lambda bi, i: (bi, i, 0)),
        out_shape=jax.ShapeDtypeStruct((b, s, c), BF16),
        scratch_shapes=[pltpu.VMEM((CONV_HALO + ts, c), F32),
                        pltpu.VMEM((ts, c), F32)],
        compiler_params=_params("parallel", "parallel"),
        name="conv",
    )(z, z, z, z, conv_w, conv_b, ln_g, ln_b)


def _merge_kernel(a_ref, f_ref, c_ref, wa_ref, wf_ref, wc_ref, g0_ref, g1_ref, g2_ref, o_ref):
    ya = jnp.dot(a_ref[...], wa_ref[...], preferred_element_type=F32)
    m = _sigmoid(g0_ref[...]) * ya
    yf = jnp.dot(f_ref[...], wf_ref[...], preferred_element_type=F32)
    m = m + _sigmoid(g1_ref[...]) * yf
    yc = jnp.dot(c_ref[...], wc_ref[...], preferred_element_type=F32)
    m = m + _sigmoid(g2_ref[...]) * yc
    o_ref[...] = m.astype(BF16)


def merge(ret_o, fox_o, conv_o, w_ret, w_fox, w_conv, z2, gl_blk, tm=1024, tn=256):
    m, k = ret_o.shape
    n = w_ret.shape[1]
    tm = min(tm, m)
    nb = n // tn
    act = pl.BlockSpec((tm, k), lambda i, j: (i, 0))
    wsp = pl.BlockSpec((k, tn), lambda i, j: (0, j))

    def gate(br):
        return pl.BlockSpec((tm, tn), lambda i, j: (i, gl_blk * nb + br * nb + j))

    return pl.pallas_call(
        _merge_kernel,
        grid=(m // tm, nb),
        in_specs=[act, act, act, wsp, wsp, wsp, gate(0), gate(1), gate(2)],
        out_specs=pl.BlockSpec((tm, tn), lambda i, j: (i, j)),
        out_shape=jax.ShapeDtypeStruct((m, n), BF16),
        compiler_params=_params("parallel", "parallel"),
        name="merge",
    )(ret_o, fox_o, conv_o, w_ret, w_fox, w_conv, z2, z2, z2)


def _out_ln_kernel(x_ref, w_ref, h_ref, g_ref, b_ref, o_ref, ob_ref, op_ref):
    y = jnp.dot(x_ref[...], w_ref[...], preferred_element_type=F32)
    out = _layernorm_rows(ALPHA * h_ref[...] + y, g_ref[...], b_ref[...])
    o_ref[...] = out
    ob_ref[...] = out.astype(BF16)
    op_ref[...] = _pack_halves(out)


def out_ln(merged, w_out, h, g, b, tm=512):
    m, k = merged.shape
    n = w_out.shape[1]
    tm = min(tm, m)
    return pl.pallas_call(
        _out_ln_kernel,
        grid=(m // tm,),
        in_specs=[pl.BlockSpec((tm, k), lambda i: (i, 0)),
                  pl.BlockSpec((k, n), lambda i: (0, 0)),
                  pl.BlockSpec((tm, n), lambda i: (i, 0)),
                  pl.BlockSpec((1, n), lambda i: (0, 0)),
                  pl.BlockSpec((1, n), lambda i: (0, 0))],
        out_specs=[pl.BlockSpec((tm, n), lambda i: (i, 0)),
                   pl.BlockSpec((tm, n), lambda i: (i, 0)),
                   pl.BlockSpec((tm, n // 2), lambda i: (i, 0))],
        out_shape=[jax.ShapeDtypeStruct((m, n), F32), jax.ShapeDtypeStruct((m, n), BF16),
                   jax.ShapeDtypeStruct((m, n // 2), jnp.uint32)],
        compiler_params=_params("parallel"),
        name="out_ln",
    )(merged, w_out, h, g, b)


ROUTE_COLS = 8


def _router_kernel(h_ref, w_ref, b_ref, tri_ref, idx_ref, wt_ref, rank_ref, cnt_ref, carry_ref):
    @pl.when(pl.program_id(0) == 0)
    def _():
        carry_ref[...] = jnp.zeros_like(carry_ref)

    h = h_ref[...]
    hh = h.astype(BF16)
    hl = (h - hh.astype(F32)).astype(BF16)
    w = w_ref[...]
    wh = w.astype(BF16)
    wl = (w - wh.astype(F32)).astype(BF16)
    logits = (jnp.dot(hh, wh, preferred_element_type=F32)
              + jnp.dot(hl, wh, preferred_element_type=F32)
              + jnp.dot(hh, wl, preferred_element_type=F32))
    scores = _sigmoid(logits)
    tm, ne = scores.shape
    sel = scores + b_ref[...]
    lane = lax.broadcasted_iota(jnp.int32, (tm, ne), 1).astype(F32)
    mask = jnp.zeros((tm, ne), F32)
    onehots, idxs, wts = [], [], []
    for _ in range(TOP_K):
        mx = jnp.max(sel, axis=-1, keepdims=True)
        ik = jnp.min(jnp.where(sel == mx, lane, float(ne)), axis=-1, keepdims=True)
        oh = lane == ik
        wts.append(jnp.sum(jnp.where(oh, scores, 0.0), axis=-1, keepdims=True))
        idxs.append(ik)
        onehots.append(oh)
        sel = jnp.where(oh, -jnp.inf, sel)
        mask = jnp.where(oh, 1.0, mask)
    wsum = wts[0]
    for t in wts[1:]:
        wsum = wsum + t
    cnt = jnp.dot(tri_ref[...], mask.astype(BF16), preferred_element_type=F32) + carry_ref[...]
    col = lax.broadcasted_iota(jnp.int32, (tm, LANES), 1)
    idx_o = jnp.zeros((tm, LANES), F32)
    wt_o = jnp.zeros((tm, LANES), F32)
    rank_o = jnp.zeros((tm, LANES), F32)
    for kk in range(TOP_K):
        rk = jnp.sum(jnp.where(onehots[kk], cnt, 0.0), axis=-1, keepdims=True)
        idx_o = jnp.where(col == kk, idxs[kk], idx_o)
        wt_o = jnp.where(col == kk, wts[kk] / wsum * ROUTED_SCALE, wt_o)
        rank_o = jnp.where(col == kk, rk, rank_o)
    idx_ref[...] = idx_o.T[:ROUTE_COLS].astype(jnp.int32)
    wt_ref[...] = wt_o[:, :ROUTE_COLS]
    rank_ref[...] = rank_o.T[:ROUTE_COLS].astype(jnp.int32)
    carry_ref[...] = carry_ref[...] + jnp.sum(mask, axis=0, keepdims=True)
    cnt_ref[...] = carry_ref[...]


def router(h, w_router, b_router, tm=512):
    m, k = h.shape
    ne = w_router.shape[1]
    tm = min(tm, m)
    r = jnp.arange(tm)
    tri = (r[:, None] > r[None, :]).astype(BF16)
    small = pl.BlockSpec((tm, ROUTE_COLS), lambda i: (i, 0))
    small_t = pl.BlockSpec((ROUTE_COLS, tm), lambda i: (0, i))
    return pl.pallas_call(
        _router_kernel,
        grid=(m // tm,),
        in_specs=[pl.BlockSpec((tm, k), lambda i: (i, 0)),
                  pl.BlockSpec((k, ne), lambda i: (0, 0)),
                  pl.BlockSpec((1, ne), lambda i: (0, 0)),
                  pl.BlockSpec((tm, tm), lambda i: (0, 0))],
        out_specs=[small_t, small, small_t, pl.BlockSpec((1, ne), lambda i: (0, 0))],
        out_shape=[jax.ShapeDtypeStruct((ROUTE_COLS, m), jnp.int32),
                   jax.ShapeDtypeStruct((m, ROUTE_COLS), F32),
                   jax.ShapeDtypeStruct((ROUTE_COLS, m), jnp.int32),
                   jax.ShapeDtypeStruct((1, ne), F32)],
        scratch_shapes=[pltpu.VMEM((1, ne), F32)],
        compiler_params=_params("arbitrary"),
        name="router",
    )(h, w_router, b_router, tri)


def _experts_kernel(be_ref, bv_ref, li_ref, x_ref, wg_ref, wu_ref, wd_ref, o_ref, wgb_ref,
                    wub_ref, wdb_ref):
    i = pl.program_id(0)

    @pl.when(jnp.logical_or(i == 0, be_ref[i] != be_ref[jnp.maximum(i - 1, 0)]))
    def _():
        wgb_ref[...] = wg_ref[0, 0].astype(BF16)
        wub_ref[...] = wu_ref[0, 0].astype(BF16)
        wdb_ref[...] = wd_ref[0, 0].astype(BF16)

    @pl.when(bv_ref[i] > 0)
    def _():
        half = x_ref.shape[1]
        xa, xb = _unpack_halves(x_ref[...])
        xa = xa.astype(BF16)
        xb = xb.astype(BF16)
        g = (jnp.dot(xa, wgb_ref[:half, :], preferred_element_type=F32)
             + jnp.dot(xb, wgb_ref[half:, :], preferred_element_type=F32))
        u = (jnp.dot(xa, wub_ref[:half, :], preferred_element_type=F32)
             + jnp.dot(xb, wub_ref[half:, :], preferred_element_type=F32))
        a = (g * _sigmoid(g) * u).astype(BF16)
        o_ref[...] = _pack_halves(jnp.dot(a, wdb_ref[...], preferred_element_type=F32))

    @pl.when(bv_ref[i] == 0)
    def _():
        o_ref[...] = jnp.zeros_like(o_ref)


def experts(xs, block_e, block_valid, layer, w_gate, w_up, w_down):
    r, half = xs.shape
    d = 2 * half
    ff = w_gate.shape[3]
    nblk = r // MOE_BLOCK
    return pl.pallas_call(
        _experts_kernel,
        grid_spec=pltpu.PrefetchScalarGridSpec(
            num_scalar_prefetch=3,
            grid=(nblk,),
            in_specs=[pl.BlockSpec((MOE_BLOCK, half), lambda i, be, bv, li: (i, 0)),
                      pl.BlockSpec((1, 1, d, ff), lambda i, be, bv, li: (li[0], be[i], 0, 0)),
                      pl.BlockSpec((1, 1, d, ff), lambda i, be, bv, li: (li[0], be[i], 0, 0)),
                      pl.BlockSpec((1, 1, ff, d), lambda i, be, bv, li: (li[0], be[i], 0, 0))],
            out_specs=pl.BlockSpec((MOE_BLOCK, half), lambda i, be, bv, li: (i, 0)),
            scratch_shapes=[pltpu.VMEM((d, ff), BF16), pltpu.VMEM((d, ff), BF16),
                            pltpu.VMEM((ff, d), BF16)]),
        out_shape=jax.ShapeDtypeStruct((r, half), jnp.uint32),
        compiler_params=_params("arbitrary"),
        name="experts",
    )(block_e, block_valid, layer, xs, w_gate, w_up, w_down)


def _shared_ln_kernel(xb_ref, h_ref, yg_ref, wt_ref, wgu_ref, wd_ref, g_ref, b_ref, o_ref,
                      ob_ref):
    ff = wd_ref.shape[0]
    gu = jnp.dot(xb_ref[...], wgu_ref[...], preferred_element_type=F32)
    g = gu[:, :ff]
    a = (g * _sigmoid(g) * gu[:, ff:]).astype(BF16)
    shared = jnp.dot(a, wd_ref[...], preferred_element_type=F32)
    wt = wt_ref[...]
    r_lo = r_hi = None
    for kk in range(TOP_K):
        lo, hi = _unpack_halves(yg_ref[kk])
        wk = wt[:, kk:kk + 1]
        r_lo = wk * lo if r_lo is None else r_lo + wk * lo
        r_hi = wk * hi if r_hi is None else r_hi + wk * hi
    routed = jnp.concatenate([r_lo, r_hi], axis=1)
    out = _layernorm_rows(ALPHA * h_ref[...] + (routed + shared), g_ref[...], b_ref[...])
    o_ref[...] = out
    ob_ref[...] = out.astype(BF16)


def shared_ln(h_bf, h, yg, wt, w_gu, w_down, g, b, tm=256):
    m, d = h.shape
    ff2 = w_gu.shape[1]
    tm = min(tm, m)
    row = pl.BlockSpec((tm, d), lambda i: (i, 0))
    vec = pl.BlockSpec((1, d), lambda i: (0, 0))
    return pl.pallas_call(
        _shared_ln_kernel,
        grid=(m // tm,),
        in_specs=[row, row,
                  pl.BlockSpec((TOP_K, tm, d // 2), lambda i: (0, i, 0)),
                  pl.BlockSpec((tm, ROUTE_COLS), lambda i: (i, 0)),
                  pl.BlockSpec((d, ff2), lambda i: (0, 0)),
                  pl.BlockSpec((ff2 // 2, d), lambda i: (0, 0)),
                  vec, vec],
        out_specs=[row, row],
        out_shape=[jax.ShapeDtypeStruct((m, d), F32), jax.ShapeDtypeStruct((m, d), BF16)],
        compiler_params=_params("parallel"),
        name="shared_ln",
    )(h_bf, h, yg, wt, w_gu, w_down, g, b)


def _mixer_layout(d):
    rq = RET_H * RET_DK
    rv = RET_H * RET_DV
    fq = FOX_H * FOX_DH
    main = 2 * rq + 2 * rv + 3 * fq
    ff0 = main
    cu0 = ff0 + FOX_H
    gl0 = cu0 + 2 * d
    end = gl0 + N_BRANCH * d
    return dict(rq=rq, rv=rv, fq=fq, main=main, ff0=ff0, cu0=cu0, gl0=gl0, end=end)


def _prep_in_weights(w_in, b_in, d, tn):
    lay = _mixer_layout(d)
    body = lay["main"] + 2 * d + N_BRANCH * d
    pad = tn - FOX_H
    w = jnp.concatenate([w_in[..., :lay["main"]], w_in[..., lay["cu0"]:lay["end"]],
                         w_in[..., lay["ff0"]:lay["cu0"]],
                         jnp.zeros(w_in.shape[:2] + (pad,), w_in.dtype)], axis=-1).astype(BF16)
    bb = jnp.concatenate([b_in[:, :lay["main"]], b_in[:, lay["cu0"]:lay["end"]],
                          b_in[:, lay["ff0"]:lay["cu0"]],
                          jnp.zeros((b_in.shape[0], pad), b_in.dtype)], axis=-1)
    return w, bb[:, None, :].astype(F32), body


def _moe_sublayer(h, h_bf, h_pk, w_router, b_router, layer, w_gate, w_up, w_down, w_sh_gu,
                  w_sh_down, g, b):
    t, d = h.shape
    ne = w_router.shape[1]
    idx8, wt8, rank8, cnt = router(h, w_router, b_router.reshape(1, ne))
    idx = idx8[:TOP_K]
    rank = rank8[:TOP_K]
    sizes = cnt[0].astype(jnp.int32)
    a = t * TOP_K
    n_blocks = -(-a // MOE_BLOCK) + ne
    r = n_blocks * MOE_BLOCK
    padded = (sizes + MOE_BLOCK - 1) // MOE_BLOCK * MOE_BLOCK
    pad_end = jnp.cumsum(padded)
    pad_start = pad_end - padded
    dest = pad_start[idx] + rank
    shift = (t + MOE_BLOCK).bit_length()
    tok = jnp.arange(t, dtype=jnp.int32)[None, :]
    real_keys = ((idx << shift) | tok).reshape(a)
    slot = jnp.arange(r - a, dtype=jnp.int32)
    slot_e = jnp.minimum(slot // MOE_BLOCK, ne - 1)
    slot_q = slot % MOE_BLOCK
    needed = (slot < ne * MOE_BLOCK) & (slot_q < (padded - sizes)[slot_e])
    pad_keys = jnp.where(needed, (slot_e << shift) | (t + slot_q), (ne << shift) + slot)
    order = jnp.sort(jnp.concatenate([real_keys, pad_keys]))
    row_tok = (order & ((1 << shift) - 1)) % t
    blk0 = jnp.arange(n_blocks, dtype=jnp.int32) * MOE_BLOCK
    block_e = jnp.minimum(jnp.sum((pad_end[None, :] <= blk0[:, None]).astype(jnp.int32), axis=1),
                          ne - 1)
    block_valid = (blk0 < pad_end[-1]).astype(jnp.int32)
    xs = h_pk[row_tok]
    ys = experts(xs, block_e, block_valid, layer, w_gate, w_up, w_down)
    yg = ys[dest.reshape(a)].reshape(TOP_K, t, d // 2)
    return shared_ln(h_bf, h, yg, wt8, w_sh_gu, w_sh_down, g, b)


def kernel(x, w_in, b_in, ret_gn_g, conv_w, conv_b, conv_ln_g, conv_ln_b, w_ret_o, w_fox_o,
           w_conv_o, w_out, ln1_g, ln1_b, w_router, b_router, w_exp_gate, w_exp_up, w_exp_down,
           w_sh_gate, w_sh_up, w_sh_down, ln2_g, ln2_b):
    bsz, seq, d = x.shape
    t = bsz * seq
    in_tn = IN_PAD
    lay = _mixer_layout(d)
    chunk = min(RET_CHUNK, seq)
    tables = _retention_tables(seq, chunk)
    fq0 = 2 * lay["rq"] + 2 * lay["rv"]

    def row(v):
        return v.reshape(1, -1)

    w_cat, b_cat, body = _prep_in_weights(w_in, b_in, d, in_tn)

    def layer(carry, p):
        h, h_bf = carry
        z2 = in_proj(h_bf, p["layer"], w_cat, b_cat)
        z = z2.reshape(bsz, seq, -1)
        ret_o = retention(z, row(p["ret_gn_g"]), tables)
        negc = fox_cumsum(z, body // LANES)
        fox_o = fox_attention(z, negc, fq0 // FOX_DH, (fq0 + lay["fq"]) // FOX_DH,
                              (fq0 + 2 * lay["fq"]) // FOX_DH)
        conv_o = conformer_conv(z, lay["main"] // d, lay["main"] // d + 1, p["conv_w"],
                                row(p["conv_b"]), row(p["conv_ln_g"]), row(p["conv_ln_b"]))
        merged = merge(ret_o.reshape(t, -1), fox_o.reshape(t, -1), conv_o.reshape(t, -1),
                       p["w_ret_o"].astype(BF16), p["w_fox_o"].astype(BF16),
                       p["w_conv_o"].astype(BF16), z2, (lay["main"] + 2 * d) // d)
        h, h_bf, h_pk = out_ln(merged, p["w_out"].astype(BF16), h, row(p["ln1_g"]),
                               row(p["ln1_b"]))
        w_sh_gu = jnp.concatenate([p["w_sh_gate"], p["w_sh_up"]], axis=-1).astype(BF16)
        h, h_bf = _moe_sublayer(h, h_bf, h_pk, p["w_router"], p["b_router"], p["layer"],
                                w_exp_gate, w_exp_up, w_exp_down, w_sh_gu,
                                p["w_sh_down"].astype(BF16), row(p["ln2_g"]), row(p["ln2_b"]))
        return (h, h_bf), None

    depth = w_in.shape[0]
    params = dict(ret_gn_g=ret_gn_g, conv_w=conv_w, conv_b=conv_b,
                  conv_ln_g=conv_ln_g, conv_ln_b=conv_ln_b, w_ret_o=w_ret_o, w_fox_o=w_fox_o,
                  w_conv_o=w_conv_o, w_out=w_out, ln1_g=ln1_g, ln1_b=ln1_b, w_router=w_router,
                  b_router=b_router, w_sh_gate=w_sh_gate, w_sh_up=w_sh_up,
                  w_sh_down=w_sh_down, ln2_g=ln2_g, ln2_b=ln2_b,
                  layer=jnp.arange(depth, dtype=jnp.int32).reshape(depth, 1))
    h0 = x.reshape(t, d)
    (h, _), _ = lax.scan(layer, (h0, h0.astype(BF16)), params)
    return h.reshape(bsz, seq, d)
```

```python
import functools

import jax
import jax.numpy as jnp
from jax import lax
from jax.experimental import pallas as pl
from jax.experimental.pallas import tpu as pltpu

F32 = jnp.float32
BF16 = jnp.bfloat16

RET_H = 8
RET_DK = 128
RET_DV = 256
ROPE_BASE = 10000.0
FOX_H = 16
FOX_DH = 128
CONV_W = 31
N_BRANCH = 3
TOP_K = 6
ROUTED_SCALE = 2.5
DEPTH_FOR_NORM = 4
ALPHA = (2 * DEPTH_FOR_NORM) ** 0.25
LN_EPS = 1e-5

LANES = 128
SUBLANES = 8
VMEM_LIMIT = 56 * 1024 * 1024

IN_PAD = 512
IN_TM = 1024
IN_TN = 1536
RET_CHUNK = 256
FOX_TQ = 2048
FOX_TK = 512
FOX_HG = 1
CUM_BLK = 256
CONV_TS = 256
CONV_HALO = 32
CONV_RB = 256
CONV_CB = 128
MOE_BLOCK = 512
NEG_BIG = -1e30
LOG2E = 1.4426950408889634


def _params(*sem):
    return pltpu.CompilerParams(dimension_semantics=sem, vmem_limit_bytes=VMEM_LIMIT)


def _sigmoid(x):
    return 1.0 / (1.0 + jnp.exp(-x))


def _pack_halves(x):
    c = x.shape[1] // 2
    xb = x.astype(BF16).astype(F32)
    lo = lax.bitcast_convert_type(xb[:, :c], jnp.uint32) >> 16
    hi = lax.bitcast_convert_type(xb[:, c:], jnp.uint32) & jnp.uint32(0xFFFF0000)
    return lo | hi


def _unpack_halves(u):
    lo = lax.bitcast_convert_type(u << 16, F32)
    hi = lax.bitcast_convert_type(u & jnp.uint32(0xFFFF0000), F32)
    return lo, hi


def _layernorm_rows(x, g, b):
    mu = jnp.mean(x, axis=-1, keepdims=True)
    xc = x - mu
    var = jnp.mean(xc * xc, axis=-1, keepdims=True)
    return xc * lax.rsqrt(var + LN_EPS) * g + b


def _in_proj_kernel(x_ref, w_ref, b_ref, o_ref):
    acc = jnp.dot(x_ref[...], w_ref[...], preferred_element_type=F32)
    o_ref[...] = acc + b_ref[...]


def in_proj(x_bf, w_bf, b, tm=IN_TM, tn=IN_TN):
    m, k = x_bf.shape
    n = w_bf.shape[1]
    tm = min(tm, m)
    assert m % tm == 0 and n % tn == 0
    return pl.pallas_call(
        _in_proj_kernel,
        grid=(m // tm, n // tn),
        in_specs=[pl.BlockSpec((tm, k), lambda i, j: (i, 0)),
                  pl.BlockSpec((k, tn), lambda i, j: (0, j)),
                  pl.BlockSpec((1, tn), lambda i, j: (0, j))],
        out_specs=pl.BlockSpec((tm, tn), lambda i, j: (i, j)),
        out_shape=jax.ShapeDtypeStruct((m, n), F32),
        compiler_params=_params("parallel", "parallel"),
        name="in_proj",
    )(x_bf, w_bf, b)


def _retention_kernel(cd_ref, q_ref, k_ref, v_ref, g_ref, cos_ref, sin_ref, decay_ref,
                      xi_ref, zeta_ref, gn_ref, o_ref, state_ref):
    @pl.when(pl.program_id(1) == 0)
    def _():
        state_ref[...] = jnp.zeros_like(state_ref)

    cos = cos_ref[...]
    sin = sin_ref[...]
    for h in range(RET_H):
        qs = slice(h * RET_DK, (h + 1) * RET_DK)
        vs = slice(h * RET_DV, (h + 1) * RET_DV)
        q = q_ref[0, :, qs]
        k = k_ref[0, :, qs]
        qr = q * cos + pltpu.roll(q, RET_DK // 2, 1) * sin
        kr = (k * cos + pltpu.roll(k, RET_DK // 2, 1) * sin) * (RET_DK ** -0.5)
        vb = v_ref[0, :, vs].astype(BF16)
        s = lax.dot_general(qr.astype(BF16), kr.astype(BF16), (((1,), (1,)), ((), ())),
                            preferred_element_type=F32)
        inner = (s * decay_ref[h]).astype(BF16)
        o = jnp.dot(inner, vb, preferred_element_type=F32)
        st = state_ref[h]
        o = o + jnp.dot((qr * xi_ref[h]).astype(BF16), st.astype(BF16),
                        preferred_element_type=F32)
        kz = (kr * zeta_ref[h]).astype(BF16)
        state_ref[h] = st * cd_ref[h] + lax.dot_general(
            kz, vb, (((0,), (0,)), ((), ())), preferred_element_type=F32)
        mu = jnp.mean(o, axis=-1, keepdims=True)
        oc = o - mu
        var = jnp.mean(oc * oc, axis=-1, keepdims=True)
        on = oc * lax.rsqrt(var + LN_EPS) * gn_ref[:, vs]
        g = g_ref[0, :, vs]
        o_ref[0, :, vs] = (g * _sigmoid(g) * on).astype(BF16)


def _retention_tables(seq, chunk):
    h = jnp.arange(RET_H, dtype=F32)
    log_g = jnp.log1p(-jnp.exp2(-5.0 - h))
    i = jnp.arange(chunk, dtype=F32)
    diff = i[:, None] - i[None, :]
    decay = jnp.where(diff >= 0, jnp.exp(log_g[:, None, None] * jnp.maximum(diff, 0.0)), 0.0)
    xi = jnp.exp(log_g[:, None] * (i + 1.0))
    zeta = jnp.exp(log_g[:, None] * (chunk - 1.0 - i))
    xi = jnp.broadcast_to(xi[:, :, None], (RET_H, chunk, RET_DK))
    zeta = jnp.broadcast_to(zeta[:, :, None], (RET_H, chunk, RET_DK))
    cd = jnp.exp(log_g * chunk)
    half = RET_DK // 2
    inv = 1.0 / (ROPE_BASE ** (jnp.arange(half, dtype=F32) / half))
    ang = jnp.arange(seq, dtype=F32)[:, None] * inv[None, :]
    cos = jnp.cos(ang)
    sin = jnp.sin(ang)
    cos2 = jnp.concatenate([cos, cos], axis=-1)
    sin2 = jnp.concatenate([-sin, sin], axis=-1)
    return cd, cos2, sin2, decay, xi, zeta


def retention(z, gn_g, tables):
    b, s, _ = z.shape
    cd, cos2, sin2, decay, xi, zeta = tables
    L = decay.shape[1]
    qw = RET_H * RET_DK
    vw = RET_H * RET_DV
    return pl.pallas_call(
        _retention_kernel,
        grid=(b, s // L),
        in_specs=[pl.BlockSpec(memory_space=pltpu.SMEM),
                  pl.BlockSpec((1, L, qw), lambda bi, n: (bi, n, 0)),
                  pl.BlockSpec((1, L, qw), lambda bi, n: (bi, n, 1)),
                  pl.BlockSpec((1, L, vw), lambda bi, n: (bi, n, 1)),
                  pl.BlockSpec((1, L, vw), lambda bi, n: (bi, n, 2)),
                  pl.BlockSpec((L, RET_DK), lambda bi, n: (n, 0)),
                  pl.BlockSpec((L, RET_DK), lambda bi, n: (n, 0)),
                  pl.BlockSpec((RET_H, L, L), lambda bi, n: (0, 0, 0)),
                  pl.BlockSpec((RET_H, L, RET_DK), lambda bi, n: (0, 0, 0)),
                  pl.BlockSpec((RET_H, L, RET_DK), lambda bi, n: (0, 0, 0)),
                  pl.BlockSpec((1, vw), lambda bi, n: (0, 0))],
        out_specs=pl.BlockSpec((1, L, vw), lambda bi, n: (bi, n, 0)),
        out_shape=jax.ShapeDtypeStruct((b, s, vw), BF16),
        scratch_shapes=[pltpu.VMEM((RET_H, RET_DK, RET_DV), F32)],
        compiler_params=_params("parallel", "arbitrary"),
        name="retention",
    )(cd, z, z, z, z, cos2, sin2, decay, xi, zeta, gn_g)


def _fox_cumsum_kernel(f_ref, tri_ref, o_ref):
    s = f_ref.shape[1]
    tri = tri_ref[...]
    carry = jnp.zeros((1, LANES), F32)
    for i in range(s // CUM_BLK):
        f = f_ref[0, i * CUM_BLK:(i + 1) * CUM_BLK, :]
        ls = jnp.minimum(f, 0.0) - jnp.log1p(jnp.exp(-jnp.abs(f)))
        hi = ls.astype(BF16)
        r1 = ls - hi.astype(F32)
        mid = r1.astype(BF16)
        lo = (r1 - mid.astype(F32)).astype(BF16)
        c = (jnp.dot(tri, hi, preferred_element_type=F32)
             + jnp.dot(tri, mid, preferred_element_type=F32)
             + jnp.dot(tri, lo, preferred_element_type=F32)) + carry
        carry = c[CUM_BLK - 1:CUM_BLK, :]
        o_ref[0, :, i * CUM_BLK:(i + 1) * CUM_BLK] = (-c).T[:FOX_H, :]


def fox_cumsum(z, ff_block):
    b, s, _ = z.shape
    r = jnp.arange(CUM_BLK)
    tri = (r[:, None] >= r[None, :]).astype(BF16)
    return pl.pallas_call(
        _fox_cumsum_kernel,
        grid=(b,),
        in_specs=[pl.BlockSpec((1, s, LANES), lambda bi: (bi, 0, ff_block)),
                  pl.BlockSpec((CUM_BLK, CUM_BLK), lambda bi: (0, 0))],
        out_specs=pl.BlockSpec((1, FOX_H, s), lambda bi: (bi, 0, 0)),
        out_shape=jax.ShapeDtypeStruct((b, FOX_H, s), F32),
        compiler_params=_params("parallel"),
        name="fox_cumsum",
    )(z, tri)


def _fox_attn_kernel(q_ref, k_ref, v_ref, nc_ref, o_ref, kt_ref, vx_ref, qb_ref, m_ref, acc_ref):
    qi = pl.program_id(2)
    s_len = k_ref.shape[1]

    def head(g):
        return slice(g * FOX_DH, (g + 1) * FOX_DH)

    @pl.when(qi == 0)
    def _():
        for g in range(FOX_HG):
            for c in range(s_len // FOX_TK):
                rows = slice(c * FOX_TK, (c + 1) * FOX_TK)
                kt_ref[g, :, rows] = k_ref[0, rows, head(g)].T.astype(BF16)
                vx_ref[g, rows, :FOX_DH] = v_ref[0, rows, head(g)].astype(BF16)
            vx_ref[g, :, FOX_DH:] = jnp.ones((s_len, FOX_DH), BF16)

    for g in range(FOX_HG):
        qb_ref[g] = (q_ref[0, :, head(g)] * (FOX_DH ** -0.5 * LOG2E)).astype(BF16)
    m_ref[...] = jnp.full_like(m_ref, NEG_BIG)
    acc_ref[...] = jnp.zeros_like(acc_ref)

    def tile(j, diag):
        k0 = pl.multiple_of(j * FOX_TK, FOX_TK)
        rows = slice(0 if diag is None else diag * FOX_TK, FOX_TQ)
        for g in range(FOX_HG):
            s = jnp.dot(qb_ref[g, rows, :], kt_ref[g, :, pl.ds(k0, FOX_TK)],
                        preferred_element_type=F32)
            s = s + nc_ref[0, g, :, pl.ds(k0, FOX_TK)] * LOG2E
            if diag is not None:
                row = lax.broadcasted_iota(jnp.int32, s.shape, 0)
                col = lax.broadcasted_iota(jnp.int32, s.shape, 1)
                s = jnp.where(col <= row, s, NEG_BIG)
            m_prev = m_ref[g, rows, :]
            m_new = jnp.maximum(m_prev, jnp.max(s, axis=-1, keepdims=True))
            alpha = jnp.exp2(m_prev - m_new)
            p = jnp.concatenate(
                [jnp.exp2(s[:, c * LANES:(c + 1) * LANES] - m_new)
                 for c in range(FOX_TK // LANES)], axis=1).astype(BF16)
            pv = jnp.dot(p, vx_ref[g, pl.ds(k0, FOX_TK), :], preferred_element_type=F32)
            acc_ref[g, rows, :] = jnp.concatenate([alpha, alpha], axis=1) * acc_ref[g, rows, :] + pv
            m_ref[g, rows, :] = m_new

    def body(j, c):
        tile(j, None)
        return c

    per_q = FOX_TQ // FOX_TK
    n_full = qi * per_q
    lax.fori_loop(0, n_full, body, 0)
    for r in range(per_q):
        tile(n_full + r, r)
    for g in range(FOX_HG):
        o_ref[0, :, head(g)] = (acc_ref[g, :, :FOX_DH] / acc_ref[g, :, FOX_DH:]).astype(BF16)


def fox_attention(z, negc, q_blk, k_blk, v_blk):
    b, s, _ = z.shape
    assert FOX_TQ % FOX_TK == 0 and FOX_DH == LANES and FOX_H % FOX_HG == 0
    gw = FOX_HG * FOX_DH
    negc4 = negc.reshape(b, FOX_H, 1, s)
    qg, kg, vg = q_blk // FOX_HG, k_blk // FOX_HG, v_blk // FOX_HG
    return pl.pallas_call(
        _fox_attn_kernel,
        grid=(b, FOX_H // FOX_HG, s // FOX_TQ),
        in_specs=[pl.BlockSpec((1, FOX_TQ, gw), lambda bi, h, qi: (bi, qi, qg + h)),
                  pl.BlockSpec((1, s, gw), lambda bi, h, qi: (bi, 0, kg + h)),
                  pl.BlockSpec((1, s, gw), lambda bi, h, qi: (bi, 0, vg + h)),
                  pl.BlockSpec((1, FOX_HG, 1, s), lambda bi, h, qi: (bi, h, 0, 0))],
        out_specs=pl.BlockSpec((1, FOX_TQ, gw), lambda bi, h, qi: (bi, qi, h)),
        out_shape=jax.ShapeDtypeStruct((b, s, FOX_H * FOX_DH), BF16),
        scratch_shapes=[pltpu.VMEM((FOX_HG, FOX_DH, s), BF16),
                        pltpu.VMEM((FOX_HG, s, 2 * FOX_DH), BF16),
                        pltpu.VMEM((FOX_HG, FOX_TQ, FOX_DH), BF16),
                        pltpu.VMEM((FOX_HG, FOX_TQ, LANES), F32),
                        pltpu.VMEM((FOX_HG, FOX_TQ, 2 * FOX_DH), F32)],
        compiler_params=_params("parallel", "parallel", "arbitrary"),
        name="fox_attn",
    )(z, z, z, negc4)


def _conv_kernel(a_ref, g_ref, ap_ref, gp_ref, w_ref, cb_ref, lg_ref, lb_ref, o_ref, y_ref,
                 c_ref):
    ts = a_ref.shape[1]
    c_all = a_ref.shape[2]
    prev = ap_ref[0] * _sigmoid(gp_ref[0])
    y_ref[0:CONV_HALO, :] = jnp.where(pl.program_id(1) > 0, prev, 0.0)
    y_ref[CONV_HALO:CONV_HALO + ts, :] = a_ref[0] * _sigmoid(g_ref[0])
    lead = CONV_HALO - (CONV_W - 1)

    n_cb = c_all // CONV_CB
    win_rows = CONV_RB + CONV_HALO

    def block(idx, carry):
        r0 = pl.multiple_of((idx // n_cb) * CONV_RB, CONV_RB)
        c0 = pl.multiple_of((idx % n_cb) * CONV_CB, CONV_CB)
        win = y_ref[pl.ds(r0, win_rows), pl.ds(c0, CONV_CB)]
        acc = jnp.zeros((CONV_RB, CONV_CB), F32)
        for ph in range(SUBLANES):
            rot = win if ph == 0 else pltpu.roll(win, win_rows - ph, 0)
            for w in range(CONV_W):
                if (w + lead) % SUBLANES == ph:
                    a0 = (w + lead) - ph
                    acc = acc + w_ref[w:w + 1, pl.ds(c0, CONV_CB)] * rot[a0:a0 + CONV_RB]
        c_ref[pl.ds(r0, CONV_RB), pl.ds(c0, CONV_CB)] = acc + cb_ref[:, pl.ds(c0, CONV_CB)]
        return carry

    lax.fori_loop(0, (ts // CONV_RB) * n_cb, block, 0)
    y = _layernorm_rows(c_ref[...], lg_ref[...], lb_ref[...])
    o_ref[0] = (y * _sigmoid(y)).astype(BF16)


def conformer_conv(z, a_blk, g_blk, conv_w, conv_b, ln_g, ln_b):
    b, s, _ = z.shape
    c = conv_w.shape[1]
    ts = min(CONV_TS, s)
    hb = ts // CONV_HALO

    def halo(col):
        return lambda bi, i: (bi, jnp.maximum(i * hb - 1, 0), col)

    return pl.pallas_call(
        _conv_kernel,
        grid=(b, s // ts),
        in_specs=[pl.BlockSpec((1, ts, c), lambda bi, i: (bi, i, a_blk)),
                  pl.BlockSpec((1, ts, c), lambda bi, i: (bi, i, g_blk)),
                  pl.BlockSpec((1, CONV_HALO, c), halo(a_blk)),
                  pl.BlockSpec((1, CONV_HALO, c), halo(g_blk)),
                  pl.BlockSpec((CONV_W, c), lambda bi, i: (0, 0)),
                  pl.BlockSpec((1, c), lambda bi, i: (0, 0)),
                  pl.BlockSpec((1, c), lambda bi, i: (0, 0)),
                  pl.BlockSpec((1, c), lambda bi, i: (0, 0))],
        out_specs=pl.BlockSpec((1, ts, c), lambda bi, i: (bi, i, 0)),
        out_shape=jax.ShapeDtypeStruct((b, s, c), BF16),
        scratch_shapes=[pltpu.VMEM((CONV_HALO + ts, c), F32),
                        pltpu.VMEM((ts, c), F32)],
        compiler_params=_params("parallel", "parallel"),
        name="conv",
    )(z, z, z, z, conv_w, conv_b, ln_g, ln_b)


def _merge_kernel(a_ref, f_ref, c_ref, wa_ref, wf_ref, wc_ref, g0_ref, g1_ref, g2_ref, o_ref):
    ya = jnp.dot(a_ref[...], wa_ref[...], preferred_element_type=F32)
    m = _sigmoid(g0_ref[...]) * ya
    yf = jnp.dot(f_ref[...], wf_ref[...], preferred_element_type=F32)
    m = m + _sigmoid(g1_ref[...]) * yf
    yc = jnp.dot(c_ref[...], wc_ref[...], preferred_element_type=F32)
    m = m + _sigmoid(g2_ref[...]) * yc
    o_ref[...] = m.astype(BF16)


def merge(ret_o, fox_o, conv_o, w_ret, w_fox, w_conv, z2, gl_blk, tm=1024, tn=256):
    m, k = ret_o.shape
    n = w_ret.shape[1]
    tm = min(tm, m)
    nb = n // tn
    act = pl.BlockSpec((tm, k), lambda i, j: (i, 0))
    wsp = pl.BlockSpec((k, tn), lambda i, j: (0, j))

    def gate(br):
        return pl.BlockSpec((tm, tn), lambda i, j: (i, gl_blk * nb + br * nb + j))

    return pl.pallas_call(
        _merge_kernel,
        grid=(m // tm, nb),
        in_specs=[act, act, act, wsp, wsp, wsp, gate(0), gate(1), gate(2)],
        out_specs=pl.BlockSpec((tm, tn), lambda i, j: (i, j)),
        out_shape=jax.ShapeDtypeStruct((m, n), BF16),
        compiler_params=_params("parallel", "parallel"),
        name="merge",
    )(ret_o, fox_o, conv_o, w_ret, w_fox, w_conv, z2, z2, z2)


def _out_ln_kernel(x_ref, w_ref, h_ref, g_ref, b_ref, o_ref, ob_ref, op_ref):
    y = jnp.dot(x_ref[...], w_ref[...], preferred_element_type=F32)
    out = _layernorm_rows(ALPHA * h_ref[...] + y, g_ref[...], b_ref[...])
    o_ref[...] = out
    ob_ref[...] = out.astype(BF16)
    op_ref[...] = _pack_halves(out)


def out_ln(merged, w_out, h, g, b, tm=512):
    m, k = merged.shape
    n = w_out.shape[1]
    tm = min(tm, m)
    return pl.pallas_call(
        _out_ln_kernel,
        grid=(m // tm,),
        in_specs=[pl.BlockSpec((tm, k), lambda i: (i, 0)),
                  pl.BlockSpec((k, n), lambda i: (0, 0)),
                  pl.BlockSpec((tm, n), lambda i: (i, 0)),
                  pl.BlockSpec((1, n), lambda i: (0, 0)),
                  pl.BlockSpec((1, n), lambda i: (0, 0))],
        out_specs=[pl.BlockSpec((tm, n), lambda i: (i, 0)),
                   pl.BlockSpec((tm, n), lambda i: (i, 0)),
                   pl.BlockSpec((tm, n // 2), lambda i: (i, 0))],
        out_shape=[jax.ShapeDtypeStruct((m, n), F32), jax.ShapeDtypeStruct((m, n), BF16),
                   jax.ShapeDtypeStruct((m, n // 2), jnp.uint32)],
        compiler_params=_params("parallel"),
        name="out_ln",
    )(merged, w_out, h, g, b)


ROUTE_COLS = 8


def _router_kernel(h_ref, w_ref, b_ref, tri_ref, idx_ref, wt_ref, rank_ref, cnt_ref, carry_ref):
    @pl.when(pl.program_id(0) == 0)
    def _():
        carry_ref[...] = jnp.zeros_like(carry_ref)

    h = h_ref[...]
    hh = h.astype(BF16)
    hl = (h - hh.astype(F32)).astype(BF16)
    w = w_ref[...]
    wh = w.astype(BF16)
    wl = (w - wh.astype(F32)).astype(BF16)
    logits = (jnp.dot(hh, wh, preferred_element_type=F32)
              + jnp.dot(hl, wh, preferred_element_type=F32)
              + jnp.dot(hh, wl, preferred_element_type=F32))
    scores = _sigmoid(logits)
    tm, ne = scores.shape
    sel = scores + b_ref[...]
    lane = lax.broadcasted_iota(jnp.int32, (tm, ne), 1).astype(F32)
    mask = jnp.zeros((tm, ne), F32)
    onehots, idxs, wts = [], [], []
    for _ in range(TOP_K):
        mx = jnp.max(sel, axis=-1, keepdims=True)
        ik = jnp.min(jnp.where(sel == mx, lane, float(ne)), axis=-1, keepdims=True)
        oh = lane == ik
        wts.append(jnp.sum(jnp.where(oh, scores, 0.0), axis=-1, keepdims=True))
        idxs.append(ik)
        onehots.append(oh)
        sel = jnp.where(oh, -jnp.inf, sel)
        mask = jnp.where(oh, 1.0, mask)
    wsum = wts[0]
    for t in wts[1:]:
        wsum = wsum + t
    cnt = jnp.dot(tri_ref[...], mask.astype(BF16), preferred_element_type=F32) + carry_ref[...]
    col = lax.broadcasted_iota(jnp.int32, (tm, LANES), 1)
    idx_o = jnp.zeros((tm, LANES), F32)
    wt_o = jnp.zeros((tm, LANES), F32)
    rank_o = jnp.zeros((tm, LANES), F32)
    for kk in range(TOP_K):
        rk = jnp.sum(jnp.where(onehots[kk], cnt, 0.0), axis=-1, keepdims=True)
        idx_o = jnp.where(col == kk, idxs[kk], idx_o)
        wt_o = jnp.where(col == kk, wts[kk] / wsum * ROUTED_SCALE, wt_o)
        rank_o = jnp.where(col == kk, rk, rank_o)
    idx_ref[...] = idx_o.T[:ROUTE_COLS].astype(jnp.int32)
    wt_ref[...] = wt_o[:, :ROUTE_COLS]
    rank_ref[...] = rank_o.T[:ROUTE_COLS].astype(jnp.int32)
    carry_ref[...] = carry_ref[...] + jnp.sum(mask, axis=0, keepdims=True)
    cnt_ref[...] = carry_ref[...]


def router(h, w_router, b_router, tm=512):
    m, k = h.shape
    ne = w_router.shape[1]
    tm = min(tm, m)
    r = jnp.arange(tm)
    tri = (r[:, None] > r[None, :]).astype(BF16)
    small = pl.BlockSpec((tm, ROUTE_COLS), lambda i: (i, 0))
    small_t = pl.BlockSpec((ROUTE_COLS, tm), lambda i: (0, i))
    return pl.pallas_call(
        _router_kernel,
        grid=(m // tm,),
        in_specs=[pl.BlockSpec((tm, k), lambda i: (i, 0)),
                  pl.BlockSpec((k, ne), lambda i: (0, 0)),
                  pl.BlockSpec((1, ne), lambda i: (0, 0)),
                  pl.BlockSpec((tm, tm), lambda i: (0, 0))],
        out_specs=[small_t, small, small_t, pl.BlockSpec((1, ne), lambda i: (0, 0))],
        out_shape=[jax.ShapeDtypeStruct((ROUTE_COLS, m), jnp.int32),
                   jax.ShapeDtypeStruct((m, ROUTE_COLS), F32),
                   jax.ShapeDtypeStruct((ROUTE_COLS, m), jnp.int32),
                   jax.ShapeDtypeStruct((1, ne), F32)],
        scratch_shapes=[pltpu.VMEM((1, ne), F32)],
        compiler_params=_params("arbitrary"),
        name="router",
    )(h, w_router, b_router, tri)


def _experts_kernel(be_ref, bv_ref, li_ref, x_ref, wg_ref, wu_ref, wd_ref, o_ref, wgb_ref,
                    wub_ref, wdb_ref):
    i = pl.program_id(0)

    @pl.when(jnp.logical_or(i == 0, be_ref[i] != be_ref[jnp.maximum(i - 1, 0)]))
    def _():
        wgb_ref[...] = wg_ref[0, 0].astype(BF16)
        wub_ref[...] = wu_ref[0, 0].astype(BF16)
        wdb_ref[...] = wd_ref[0, 0].astype(BF16)

    @pl.when(bv_ref[i] > 0)
    def _():
        half = x_ref.shape[1]
        xa, xb = _unpack_halves(x_ref[...])
        xa = xa.astype(BF16)
        xb = xb.astype(BF16)
        g = (jnp.dot(xa, wgb_ref[:half, :], preferred_element_type=F32)
             + jnp.dot(xb, wgb_ref[half:, :], preferred_element_type=F32))
        u = (jnp.dot(xa, wub_ref[:half, :], preferred_element_type=F32)
             + jnp.dot(xb, wub_ref[half:, :], preferred_element_type=F32))
        a = (g * _sigmoid(g) * u).astype(BF16)
        o_ref[...] = _pack_halves(jnp.dot(a, wdb_ref[...], preferred_element_type=F32))

    @pl.when(bv_ref[i] == 0)
    def _():
        o_ref[...] = jnp.zeros_like(o_ref)


def experts(xs, block_e, block_valid, layer, w_gate, w_up, w_down):
    r, half = xs.shape
    d = 2 * half
    ff = w_gate.shape[3]
    nblk = r // MOE_BLOCK
    return pl.pallas_call(
        _experts_kernel,
        grid_spec=pltpu.PrefetchScalarGridSpec(
            num_scalar_prefetch=3,
            grid=(nblk,),
            in_specs=[pl.BlockSpec((MOE_BLOCK, half), lambda i, be, bv, li: (i, 0)),
                      pl.BlockSpec((1, 1, d, ff), lambda i, be, bv, li: (li[0], be[i], 0, 0)),
                      pl.BlockSpec((1, 1, d, ff), lambda i, be, bv, li: (li[0], be[i], 0, 0)),
                      pl.BlockSpec((1, 1, ff, d), lambda i, be, bv, li: (li[0], be[i], 0, 0))],
            out_specs=pl.BlockSpec((MOE_BLOCK, half), lambda i, be, bv, li: (i, 0)),
            scratch_shapes=[pltpu.VMEM((d, ff), BF16), pltpu.VMEM((d, ff), BF16),
                            pltpu.VMEM((ff, d), BF16)]),
        out_shape=jax.ShapeDtypeStruct((r, half), jnp.uint32),
        compiler_params=_params("arbitrary"),
        name="experts",
    )(block_e, block_valid, layer, xs, w_gate, w_up, w_down)


def _shared_ln_kernel(xb_ref, h_ref, yg_ref, wt_ref, wgu_ref, wd_ref, g_ref, b_ref, o_ref,
                      ob_ref):
    ff = wd_ref.shape[0]
    gu = jnp.dot(xb_ref[...], wgu_ref[...], preferred_element_type=F32)
    g = gu[:, :ff]
    a = (g * _sigmoid(g) * gu[:, ff:]).astype(BF16)
    shared = jnp.dot(a, wd_ref[...], preferred_element_type=F32)
    wt = wt_ref[...]
    r_lo = r_hi = None
    for kk in range(TOP_K):
        lo, hi = _unpack_halves(yg_ref[kk])
        wk = wt[:, kk:kk + 1]
        r_lo = wk * lo if r_lo is None else r_lo + wk * lo
        r_hi = wk * hi if r_hi is None else r_hi + wk * hi
    routed = jnp.concatenate([r_lo, r_hi], axis=1)
    out = _layernorm_rows(ALPHA * h_ref[...] + (routed + shared), g_ref[...], b_ref[...])
    o_ref[...] = out
    ob_ref[...] = out.astype(BF16)


def shared_ln(h_bf, h, yg, wt, w_gu, w_down, g, b, tm=256):
    m, d = h.shape
    ff2 = w_gu.shape[1]
    tm = min(tm, m)
    row = pl.BlockSpec((tm, d), lambda i: (i, 0))
    vec = pl.BlockSpec((1, d), lambda i: (0, 0))
    return pl.pallas_call(
        _shared_ln_kernel,
        grid=(m // tm,),
        in_specs=[row, row,
                  pl.BlockSpec((TOP_K, tm, d // 2), lambda i: (0, i, 0)),
                  pl.BlockSpec((tm, ROUTE_COLS), lambda i: (i, 0)),
                  pl.BlockSpec((d, ff2), lambda i: (0, 0)),
                  pl.BlockSpec((ff2 // 2, d), lambda i: (0, 0)),
                  vec, vec],
        out_specs=[row, row],
        out_shape=[jax.ShapeDtypeStruct((m, d), F32), jax.ShapeDtypeStruct((m, d), BF16)],
        compiler_params=_params("parallel"),
        name="shared_ln",
    )(h_bf, h, yg, wt, w_gu, w_down, g, b)


def _mixer_layout(d):
    rq = RET_H * RET_DK
    rv = RET_H * RET_DV
    fq = FOX_H * FOX_DH
    main = 2 * rq + 2 * rv + 3 * fq
    ff0 = main
    cu0 = ff0 + FOX_H
    gl0 = cu0 + 2 * d
    end = gl0 + N_BRANCH * d
    return dict(rq=rq, rv=rv, fq=fq, main=main, ff0=ff0, cu0=cu0, gl0=gl0, end=end)


def _prep_in_weights(w_in, b_in, d, tn):
    lay = _mixer_layout(d)
    body = lay["main"] + 2 * d + N_BRANCH * d
    pad = tn - FOX_H
    w = jnp.concatenate([w_in[:, :lay["main"]], w_in[:, lay["cu0"]:lay["end"]],
                         w_in[:, lay["ff0"]:lay["cu0"]],
                         jnp.zeros((w_in.shape[0], pad), w_in.dtype)], axis=1).astype(BF16)
    bb = jnp.concatenate([b_in[:lay["main"]], b_in[lay["cu0"]:lay["end"]],
                          b_in[lay["ff0"]:lay["cu0"]], jnp.zeros((pad,), b_in.dtype)])
    return w, bb.reshape(1, -1).astype(F32), body


def _moe_sublayer(h, h_bf, h_pk, w_router, b_router, layer, w_gate, w_up, w_down, w_sh_gu,
                  w_sh_down, g, b):
    t, d = h.shape
    ne = w_router.shape[1]
    idx8, wt8, rank8, cnt = router(h, w_router, b_router.reshape(1, ne))
    idx = idx8[:TOP_K]
    rank = rank8[:TOP_K]
    sizes = cnt[0].astype(jnp.int32)
    a = t * TOP_K
    n_blocks = -(-a // MOE_BLOCK) + ne
    r = n_blocks * MOE_BLOCK
    padded = (sizes + MOE_BLOCK - 1) // MOE_BLOCK * MOE_BLOCK
    pad_end = jnp.cumsum(padded)
    pad_start = pad_end - padded
    start_of = jnp.sum(jnp.where(idx[..., None] == jnp.arange(ne, dtype=jnp.int32), pad_start, 0),
                       axis=-1)
    dest = lax.optimization_barrier(start_of + rank)
    shift = (t + MOE_BLOCK).bit_length()
    tok = jnp.arange(t, dtype=jnp.int32)[None, :]
    real_keys = ((idx << shift) | tok).reshape(a)
    slot = jnp.arange(r - a, dtype=jnp.int32)
    slot_e = jnp.minimum(slot // MOE_BLOCK, ne - 1)
    slot_q = slot % MOE_BLOCK
    needed = (slot < ne * MOE_BLOCK) & (slot_q < (padded - sizes)[slot_e])
    pad_keys = jnp.where(needed, (slot_e << shift) | (t + slot_q), (ne << shift) + slot)
    order = jnp.sort(jnp.concatenate([real_keys, pad_keys]))
    row_tok = (order & ((1 << shift) - 1)) % t
    blk0 = jnp.arange(n_blocks, dtype=jnp.int32) * MOE_BLOCK
    block_e = jnp.minimum(jnp.sum((pad_end[None, :] <= blk0[:, None]).astype(jnp.int32), axis=1),
                          ne - 1)
    block_valid = (blk0 < pad_end[-1]).astype(jnp.int32)
    xs = h_pk[row_tok]
    ys = experts(xs, block_e, block_valid, layer, w_gate, w_up, w_down)
    yg = ys[dest.reshape(a)].reshape(TOP_K, t, d // 2)
    return shared_ln(h_bf, h, yg, wt8, w_sh_gu, w_sh_down, g, b)


def kernel(x, w_in, b_in, ret_gn_g, conv_w, conv_b, conv_ln_g, conv_ln_b, w_ret_o, w_fox_o,
           w_conv_o, w_out, ln1_g, ln1_b, w_router, b_router, w_exp_gate, w_exp_up, w_exp_down,
           w_sh_gate, w_sh_up, w_sh_down, ln2_g, ln2_b):
    bsz, seq, d = x.shape
    t = bsz * seq
    in_tn = IN_PAD
    lay = _mixer_layout(d)
    chunk = min(RET_CHUNK, seq)
    tables = _retention_tables(seq, chunk)
    fq0 = 2 * lay["rq"] + 2 * lay["rv"]

    def row(v):
        return v.reshape(1, -1)

    def layer(carry, p):
        h, h_bf = carry
        w_cat, b_cat, body = _prep_in_weights(p["w_in"], p["b_in"], d, in_tn)
        z2 = in_proj(h_bf, w_cat, b_cat)
        z = z2.reshape(bsz, seq, -1)
        ret_o = retention(z, row(p["ret_gn_g"]), tables)
        negc = fox_cumsum(z, body // LANES)
        fox_o = fox_attention(z, negc, fq0 // FOX_DH, (fq0 + lay["fq"]) // FOX_DH,
                              (fq0 + 2 * lay["fq"]) // FOX_DH)
        conv_o = conformer_conv(z, lay["main"] // d, lay["main"] // d + 1, p["conv_w"],
                                row(p["conv_b"]), row(p["conv_ln_g"]), row(p["conv_ln_b"]))
        merged = merge(ret_o.reshape(t, -1), fox_o.reshape(t, -1), conv_o.reshape(t, -1),
                       p["w_ret_o"].astype(BF16), p["w_fox_o"].astype(BF16),
                       p["w_conv_o"].astype(BF16), z2, (lay["main"] + 2 * d) // d)
        h, h_bf, h_pk = out_ln(merged, p["w_out"].astype(BF16), h, row(p["ln1_g"]),
                               row(p["ln1_b"]))
        w_sh_gu = jnp.concatenate([p["w_sh_gate"], p["w_sh_up"]], axis=-1).astype(BF16)
        h, h_bf = _moe_sublayer(h, h_bf, h_pk, p["w_router"], p["b_router"], p["layer"],
                                w_exp_gate, w_exp_up, w_exp_down, w_sh_gu,
                                p["w_sh_down"].astype(BF16), row(p["ln2_g"]), row(p["ln2_b"]))
        return (h, h_bf), None

    depth = w_in.shape[0]
    params = dict(w_in=w_in, b_in=b_in, ret_gn_g=ret_gn_g, conv_w=conv_w, conv_b=conv_b,
                  conv_ln_g=conv_ln_g, conv_ln_b=conv_ln_b, w_ret_o=w_ret_o, w_fox_o=w_fox_o,
                  w_conv_o=w_conv_o, w_out=w_out, ln1_g=ln1_g, ln1_b=ln1_b, w_router=w_router,
                  b_router=b_router, w_sh_gate=w_sh_gate, w_sh_up=w_sh_up,
                  w_sh_down=w_sh_down, ln2_g=ln2_g, ln2_b=ln2_b,
                  layer=jnp.arange(depth, dtype=jnp.int32).reshape(depth, 1))
    h0 = x.reshape(t, d)
    (h, _), _ = lax.scan(layer, (h0, h0.astype(BF16)), params)
    return h.reshape(bsz, seq, d)
```

```python
import functools

import jax
import jax.numpy as jnp
from jax import lax
from jax.experimental import pallas as pl
from jax.experimental.pallas import tpu as pltpu

F32 = jnp.float32
BF16 = jnp.bfloat16

RET_H = 8
RET_DK = 128
RET_DV = 256
ROPE_BASE = 10000.0
FOX_H = 16
FOX_DH = 128
CONV_W = 31
N_BRANCH = 3
TOP_K = 6
ROUTED_SCALE = 2.5
DEPTH_FOR_NORM = 4
ALPHA = (2 * DEPTH_FOR_NORM) ** 0.25
LN_EPS = 1e-5

LANES = 128
SUBLANES = 8
VMEM_LIMIT = 56 * 1024 * 1024

IN_TM = 1024
IN_TN = 2048
RET_CHUNK = 256
FOX_TQ = 2048
FOX_TK = 512
FOX_HG = 1
CUM_BLK = 256
CONV_TS = 256
CONV_HALO = 32
CONV_RB = 256
CONV_CB = 128
MOE_BLOCK = 512
NEG_BIG = -1e30
LOG2E = 1.4426950408889634


def _params(*sem):
    return pltpu.CompilerParams(dimension_semantics=sem, vmem_limit_bytes=VMEM_LIMIT)


def _sigmoid(x):
    return 1.0 / (1.0 + jnp.exp(-x))


def _pack_halves(x):
    c = x.shape[1] // 2
    xb = x.astype(BF16).astype(F32)
    lo = lax.bitcast_convert_type(xb[:, :c], jnp.uint32) >> 16
    hi = lax.bitcast_convert_type(xb[:, c:], jnp.uint32) & jnp.uint32(0xFFFF0000)
    return lo | hi


def _unpack_halves(u):
    lo = lax.bitcast_convert_type(u << 16, F32)
    hi = lax.bitcast_convert_type(u & jnp.uint32(0xFFFF0000), F32)
    return lo, hi


def _layernorm_rows(x, g, b):
    mu = jnp.mean(x, axis=-1, keepdims=True)
    xc = x - mu
    var = jnp.mean(xc * xc, axis=-1, keepdims=True)
    return xc * lax.rsqrt(var + LN_EPS) * g + b


def _in_proj_kernel(x_ref, w_ref, b_ref, o_ref):
    acc = jnp.dot(x_ref[...], w_ref[...], preferred_element_type=F32)
    o_ref[...] = acc + b_ref[...]


def in_proj(x_bf, w_bf, b, tm=IN_TM, tn=IN_TN):
    m, k = x_bf.shape
    n = w_bf.shape[1]
    tm = min(tm, m)
    assert m % tm == 0 and n % tn == 0
    return pl.pallas_call(
        _in_proj_kernel,
        grid=(m // tm, n // tn),
        in_specs=[pl.BlockSpec((tm, k), lambda i, j: (i, 0)),
                  pl.BlockSpec((k, tn), lambda i, j: (0, j)),
                  pl.BlockSpec((1, tn), lambda i, j: (0, j))],
        out_specs=pl.BlockSpec((tm, tn), lambda i, j: (i, j)),
        out_shape=jax.ShapeDtypeStruct((m, n), F32),
        compiler_params=_params("parallel", "parallel"),
        name="in_proj",
    )(x_bf, w_bf, b)


def _retention_kernel(cd_ref, q_ref, k_ref, v_ref, g_ref, cos_ref, sin_ref, decay_ref,
                      xi_ref, zeta_ref, gn_ref, o_ref, state_ref):
    @pl.when(pl.program_id(1) == 0)
    def _():
        state_ref[...] = jnp.zeros_like(state_ref)

    cos = cos_ref[...]
    sin = sin_ref[...]
    for h in range(RET_H):
        qs = slice(h * RET_DK, (h + 1) * RET_DK)
        vs = slice(h * RET_DV, (h + 1) * RET_DV)
        q = q_ref[0, :, qs]
        k = k_ref[0, :, qs]
        qr = q * cos + pltpu.roll(q, RET_DK // 2, 1) * sin
        kr = (k * cos + pltpu.roll(k, RET_DK // 2, 1) * sin) * (RET_DK ** -0.5)
        vb = v_ref[0, :, vs].astype(BF16)
        s = lax.dot_general(qr.astype(BF16), kr.astype(BF16), (((1,), (1,)), ((), ())),
                            preferred_element_type=F32)
        inner = (s * decay_ref[h]).astype(BF16)
        o = jnp.dot(inner, vb, preferred_element_type=F32)
        st = state_ref[h]
        o = o + jnp.dot((qr * xi_ref[h]).astype(BF16), st.astype(BF16),
                        preferred_element_type=F32)
        kz = (kr * zeta_ref[h]).astype(BF16)
        state_ref[h] = st * cd_ref[h] + lax.dot_general(
            kz, vb, (((0,), (0,)), ((), ())), preferred_element_type=F32)
        mu = jnp.mean(o, axis=-1, keepdims=True)
        oc = o - mu
        var = jnp.mean(oc * oc, axis=-1, keepdims=True)
        on = oc * lax.rsqrt(var + LN_EPS) * gn_ref[:, vs]
        g = g_ref[0, :, vs]
        o_ref[0, :, vs] = (g * _sigmoid(g) * on).astype(BF16)


def _retention_tables(seq, chunk):
    h = jnp.arange(RET_H, dtype=F32)
    log_g = jnp.log1p(-jnp.exp2(-5.0 - h))
    i = jnp.arange(chunk, dtype=F32)
    diff = i[:, None] - i[None, :]
    decay = jnp.where(diff >= 0, jnp.exp(log_g[:, None, None] * jnp.maximum(diff, 0.0)), 0.0)
    xi = jnp.exp(log_g[:, None] * (i + 1.0))
    zeta = jnp.exp(log_g[:, None] * (chunk - 1.0 - i))
    xi = jnp.broadcast_to(xi[:, :, None], (RET_H, chunk, RET_DK))
    zeta = jnp.broadcast_to(zeta[:, :, None], (RET_H, chunk, RET_DK))
    cd = jnp.exp(log_g * chunk)
    half = RET_DK // 2
    inv = 1.0 / (ROPE_BASE ** (jnp.arange(half, dtype=F32) / half))
    ang = jnp.arange(seq, dtype=F32)[:, None] * inv[None, :]
    cos = jnp.cos(ang)
    sin = jnp.sin(ang)
    cos2 = jnp.concatenate([cos, cos], axis=-1)
    sin2 = jnp.concatenate([-sin, sin], axis=-1)
    return cd, cos2, sin2, decay, xi, zeta


def retention(z, gn_g, tables):
    b, s, _ = z.shape
    cd, cos2, sin2, decay, xi, zeta = tables
    L = decay.shape[1]
    qw = RET_H * RET_DK
    vw = RET_H * RET_DV
    return pl.pallas_call(
        _retention_kernel,
        grid=(b, s // L),
        in_specs=[pl.BlockSpec(memory_space=pltpu.SMEM),
                  pl.BlockSpec((1, L, qw), lambda bi, n: (bi, n, 0)),
                  pl.BlockSpec((1, L, qw), lambda bi, n: (bi, n, 1)),
                  pl.BlockSpec((1, L, vw), lambda bi, n: (bi, n, 1)),
                  pl.BlockSpec((1, L, vw), lambda bi, n: (bi, n, 2)),
                  pl.BlockSpec((L, RET_DK), lambda bi, n: (n, 0)),
                  pl.BlockSpec((L, RET_DK), lambda bi, n: (n, 0)),
                  pl.BlockSpec((RET_H, L, L), lambda bi, n: (0, 0, 0)),
                  pl.BlockSpec((RET_H, L, RET_DK), lambda bi, n: (0, 0, 0)),
                  pl.BlockSpec((RET_H, L, RET_DK), lambda bi, n: (0, 0, 0)),
                  pl.BlockSpec((1, vw), lambda bi, n: (0, 0))],
        out_specs=pl.BlockSpec((1, L, vw), lambda bi, n: (bi, n, 0)),
        out_shape=jax.ShapeDtypeStruct((b, s, vw), BF16),
        scratch_shapes=[pltpu.VMEM((RET_H, RET_DK, RET_DV), F32)],
        compiler_params=_params("parallel", "arbitrary"),
        name="retention",
    )(cd, z, z, z, z, cos2, sin2, decay, xi, zeta, gn_g)


def _fox_cumsum_kernel(h_ref, w_ref, b_ref, tri_ref, o_ref):
    s = h_ref.shape[1]
    tri = tri_ref[...]
    carry = jnp.zeros((1, LANES), F32)
    for i in range(s // CUM_BLK):
        f = jnp.dot(h_ref[0, i * CUM_BLK:(i + 1) * CUM_BLK, :], w_ref[...],
                    preferred_element_type=F32) + b_ref[...]
        ls = jnp.minimum(f, 0.0) - jnp.log1p(jnp.exp(-jnp.abs(f)))
        hi = ls.astype(BF16)
        r1 = ls - hi.astype(F32)
        mid = r1.astype(BF16)
        lo = (r1 - mid.astype(F32)).astype(BF16)
        c = (jnp.dot(tri, hi, preferred_element_type=F32)
             + jnp.dot(tri, mid, preferred_element_type=F32)
             + jnp.dot(tri, lo, preferred_element_type=F32)) + carry
        carry = c[CUM_BLK - 1:CUM_BLK, :]
        o_ref[0, :, i * CUM_BLK:(i + 1) * CUM_BLK] = (-c).T[:FOX_H, :]


def fox_cumsum(h_bf, w_ff, b_ff):
    b, s, d = h_bf.shape
    r = jnp.arange(CUM_BLK)
    tri = (r[:, None] >= r[None, :]).astype(BF16)
    return pl.pallas_call(
        _fox_cumsum_kernel,
        grid=(b,),
        in_specs=[pl.BlockSpec((1, s, d), lambda bi: (bi, 0, 0)),
                  pl.BlockSpec((d, LANES), lambda bi: (0, 0)),
                  pl.BlockSpec((1, LANES), lambda bi: (0, 0)),
                  pl.BlockSpec((CUM_BLK, CUM_BLK), lambda bi: (0, 0))],
        out_specs=pl.BlockSpec((1, FOX_H, s), lambda bi: (bi, 0, 0)),
        out_shape=jax.ShapeDtypeStruct((b, FOX_H, s), F32),
        compiler_params=_params("parallel"),
        name="fox_cumsum",
    )(h_bf, w_ff, b_ff, tri)


def _fox_attn_kernel(q_ref, k_ref, v_ref, nc_ref, o_ref, kt_ref, vx_ref, qb_ref, m_ref, acc_ref):
    qi = pl.program_id(2)
    s_len = k_ref.shape[1]

    def head(g):
        return slice(g * FOX_DH, (g + 1) * FOX_DH)

    @pl.when(qi == 0)
    def _():
        for g in range(FOX_HG):
            for c in range(s_len // FOX_TK):
                rows = slice(c * FOX_TK, (c + 1) * FOX_TK)
                kt_ref[g, :, rows] = k_ref[0, rows, head(g)].T.astype(BF16)
                vx_ref[g, rows, :FOX_DH] = v_ref[0, rows, head(g)].astype(BF16)
            vx_ref[g, :, FOX_DH:] = jnp.ones((s_len, FOX_DH), BF16)

    for g in range(FOX_HG):
        qb_ref[g] = (q_ref[0, :, head(g)] * (FOX_DH ** -0.5 * LOG2E)).astype(BF16)
    m_ref[...] = jnp.full_like(m_ref, NEG_BIG)
    acc_ref[...] = jnp.zeros_like(acc_ref)

    def tile(j, diag):
        k0 = pl.multiple_of(j * FOX_TK, FOX_TK)
        rows = slice(0 if diag is None else diag * FOX_TK, FOX_TQ)
        for g in range(FOX_HG):
            s = jnp.dot(qb_ref[g, rows, :], kt_ref[g, :, pl.ds(k0, FOX_TK)],
                        preferred_element_type=F32)
            s = s + nc_ref[0, g, :, pl.ds(k0, FOX_TK)] * LOG2E
            if diag is not None:
                row = lax.broadcasted_iota(jnp.int32, s.shape, 0)
                col = lax.broadcasted_iota(jnp.int32, s.shape, 1)
                s = jnp.where(col <= row, s, NEG_BIG)
            m_prev = m_ref[g, rows, :]
            m_new = jnp.maximum(m_prev, jnp.max(s, axis=-1, keepdims=True))
            alpha = jnp.exp2(m_prev - m_new)
            p = jnp.concatenate(
                [jnp.exp2(s[:, c * LANES:(c + 1) * LANES] - m_new)
                 for c in range(FOX_TK // LANES)], axis=1).astype(BF16)
            pv = jnp.dot(p, vx_ref[g, pl.ds(k0, FOX_TK), :], preferred_element_type=F32)
            acc_ref[g, rows, :] = jnp.concatenate([alpha, alpha], axis=1) * acc_ref[g, rows, :] + pv
            m_ref[g, rows, :] = m_new

    def body(j, c):
        tile(j, None)
        return c

    per_q = FOX_TQ // FOX_TK
    n_full = qi * per_q
    lax.fori_loop(0, n_full, body, 0)
    for r in range(per_q):
        tile(n_full + r, r)
    for g in range(FOX_HG):
        o_ref[0, :, head(g)] = (acc_ref[g, :, :FOX_DH] / acc_ref[g, :, FOX_DH:]).astype(BF16)


def fox_attention(z, negc, q_blk, k_blk, v_blk):
    b, s, _ = z.shape
    assert FOX_TQ % FOX_TK == 0 and FOX_DH == LANES and FOX_H % FOX_HG == 0
    gw = FOX_HG * FOX_DH
    negc4 = negc.reshape(b, FOX_H, 1, s)
    qg, kg, vg = q_blk // FOX_HG, k_blk // FOX_HG, v_blk // FOX_HG
    return pl.pallas_call(
        _fox_attn_kernel,
        grid=(b, FOX_H // FOX_HG, s // FOX_TQ),
        in_specs=[pl.BlockSpec((1, FOX_TQ, gw), lambda bi, h, qi: (bi, qi, qg + h)),
                  pl.BlockSpec((1, s, gw), lambda bi, h, qi: (bi, 0, kg + h)),
                  pl.BlockSpec((1, s, gw), lambda bi, h, qi: (bi, 0, vg + h)),
                  pl.BlockSpec((1, FOX_HG, 1, s), lambda bi, h, qi: (bi, h, 0, 0))],
        out_specs=pl.BlockSpec((1, FOX_TQ, gw), lambda bi, h, qi: (bi, qi, h)),
        out_shape=jax.ShapeDtypeStruct((b, s, FOX_H * FOX_DH), BF16),
        scratch_shapes=[pltpu.VMEM((FOX_HG, FOX_DH, s), BF16),
                        pltpu.VMEM((FOX_HG, s, 2 * FOX_DH), BF16),
                        pltpu.VMEM((FOX_HG, FOX_TQ, FOX_DH), BF16),
                        pltpu.VMEM((FOX_HG, FOX_TQ, LANES), F32),
                        pltpu.VMEM((FOX_HG, FOX_TQ, 2 * FOX_DH), F32)],
        compiler_params=_params("parallel", "parallel", "arbitrary"),
        name="fox_attn",
    )(z, z, z, negc4)


def _conv_kernel(a_ref, g_ref, ap_ref, gp_ref, w_ref, cb_ref, lg_ref, lb_ref, o_ref, y_ref,
                 c_ref):
    ts = a_ref.shape[1]
    c_all = a_ref.shape[2]
    prev = ap_ref[0] * _sigmoid(gp_ref[0])
    y_ref[0:CONV_HALO, :] = jnp.where(pl.program_id(1) > 0, prev, 0.0)
    y_ref[CONV_HALO:CONV_HALO + ts, :] = a_ref[0] * _sigmoid(g_ref[0])
    lead = CONV_HALO - (CONV_W - 1)

    n_cb = c_all // CONV_CB
    win_rows = CONV_RB + CONV_HALO

    def block(idx, carry):
        r0 = pl.multiple_of((idx // n_cb) * CONV_RB, CONV_RB)
        c0 = pl.multiple_of((idx % n_cb) * CONV_CB, CONV_CB)
        win = y_ref[pl.ds(r0, win_rows), pl.ds(c0, CONV_CB)]
        acc = jnp.zeros((CONV_RB, CONV_CB), F32)
        for ph in range(SUBLANES):
            rot = win if ph == 0 else pltpu.roll(win, win_rows - ph, 0)
            for w in range(CONV_W):
                if (w + lead) % SUBLANES == ph:
                    a0 = (w + lead) - ph
                    acc = acc + w_ref[w:w + 1, pl.ds(c0, CONV_CB)] * rot[a0:a0 + CONV_RB]
        c_ref[pl.ds(r0, CONV_RB), pl.ds(c0, CONV_CB)] = acc + cb_ref[:, pl.ds(c0, CONV_CB)]
        return carry

    lax.fori_loop(0, (ts // CONV_RB) * n_cb, block, 0)
    y = _layernorm_rows(c_ref[...], lg_ref[...], lb_ref[...])
    o_ref[0] = (y * _sigmoid(y)).astype(BF16)


def conformer_conv(z, a_blk, g_blk, conv_w, conv_b, ln_g, ln_b):
    b, s, _ = z.shape
    c = conv_w.shape[1]
    ts = min(CONV_TS, s)
    hb = ts // CONV_HALO

    def halo(col):
        return lambda bi, i: (bi, jnp.maximum(i * hb - 1, 0), col)

    return pl.pallas_call(
        _conv_kernel,
        grid=(b, s // ts),
        in_specs=[pl.BlockSpec((1, ts, c), lambda bi, i: (bi, i, a_blk)),
                  pl.BlockSpec((1, ts, c), lambda bi, i: (bi, i, g_blk)),
                  pl.BlockSpec((1, CONV_HALO, c), halo(a_blk)),
                  pl.BlockSpec((1, CONV_HALO, c), halo(g_blk)),
                  pl.BlockSpec((CONV_W, c), lambda bi, i: (0, 0)),
                  pl.BlockSpec((1, c), lambda bi, i: (0, 0)),
                  pl.BlockSpec((1, c), lambda bi, i: (0, 0)),
                  pl.BlockSpec((1, c), lambda bi, i: (0, 0))],
        out_specs=pl.BlockSpec((1, ts, c), lambda bi, i: (bi, i, 0)),
        out_shape=jax.ShapeDtypeStruct((b, s, c), BF16),
        scratch_shapes=[pltpu.VMEM((CONV_HALO + ts, c), F32),
                        pltpu.VMEM((ts, c), F32)],
        compiler_params=_params("parallel", "parallel"),
        name="conv",
    )(z, z, z, z, conv_w, conv_b, ln_g, ln_b)


def _merge_kernel(a_ref, f_ref, c_ref, wa_ref, wf_ref, wc_ref, g0_ref, g1_ref, g2_ref, o_ref):
    ya = jnp.dot(a_ref[...], wa_ref[...], preferred_element_type=F32)
    m = _sigmoid(g0_ref[...]) * ya
    yf = jnp.dot(f_ref[...], wf_ref[...], preferred_element_type=F32)
    m = m + _sigmoid(g1_ref[...]) * yf
    yc = jnp.dot(c_ref[...], wc_ref[...], preferred_element_type=F32)
    m = m + _sigmoid(g2_ref[...]) * yc
    o_ref[...] = m.astype(BF16)


def merge(ret_o, fox_o, conv_o, w_ret, w_fox, w_conv, z2, gl_blk, tm=1024, tn=256):
    m, k = ret_o.shape
    n = w_ret.shape[1]
    tm = min(tm, m)
    nb = n // tn
    act = pl.BlockSpec((tm, k), lambda i, j: (i, 0))
    wsp = pl.BlockSpec((k, tn), lambda i, j: (0, j))

    def gate(br):
        return pl.BlockSpec((tm, tn), lambda i, j: (i, gl_blk * nb + br * nb + j))

    return pl.pallas_call(
        _merge_kernel,
        grid=(m // tm, nb),
        in_specs=[act, act, act, wsp, wsp, wsp, gate(0), gate(1), gate(2)],
        out_specs=pl.BlockSpec((tm, tn), lambda i, j: (i, j)),
        out_shape=jax.ShapeDtypeStruct((m, n), BF16),
        compiler_params=_params("parallel", "parallel"),
        name="merge",
    )(ret_o, fox_o, conv_o, w_ret, w_fox, w_conv, z2, z2, z2)


ROUTE_COLS = 8


def _route_tile(h, w_ref, b_ref, tri_ref, idx_ref, wt_ref, rank_ref, cnt_ref, carry_ref):
    @pl.when(pl.program_id(0) == 0)
    def _():
        carry_ref[...] = jnp.zeros_like(carry_ref)

    hh = h.astype(BF16)
    hl = (h - hh.astype(F32)).astype(BF16)
    w = w_ref[...]
    wh = w.astype(BF16)
    wl = (w - wh.astype(F32)).astype(BF16)
    logits = (jnp.dot(hh, wh, preferred_element_type=F32)
              + jnp.dot(hl, wh, preferred_element_type=F32)
              + jnp.dot(hh, wl, preferred_element_type=F32))
    scores = _sigmoid(logits)
    tm, ne = scores.shape
    sel = scores + b_ref[...]
    lane = lax.broadcasted_iota(jnp.int32, (tm, ne), 1).astype(F32)
    mask = jnp.zeros((tm, ne), F32)
    onehots, idxs, wts = [], [], []
    for _ in range(TOP_K):
        mx = jnp.max(sel, axis=-1, keepdims=True)
        ik = jnp.min(jnp.where(sel == mx, lane, float(ne)), axis=-1, keepdims=True)
        oh = lane == ik
        wts.append(jnp.sum(jnp.where(oh, scores, 0.0), axis=-1, keepdims=True))
        idxs.append(ik)
        onehots.append(oh)
        sel = jnp.where(oh, -jnp.inf, sel)
        mask = jnp.where(oh, 1.0, mask)
    wsum = wts[0]
    for t in wts[1:]:
        wsum = wsum + t
    cnt = jnp.dot(tri_ref[...], mask.astype(BF16), preferred_element_type=F32) + carry_ref[...]
    col = lax.broadcasted_iota(jnp.int32, (tm, LANES), 1)
    idx_o = jnp.zeros((tm, LANES), F32)
    wt_o = jnp.zeros((tm, LANES), F32)
    rank_o = jnp.zeros((tm, LANES), F32)
    for kk in range(TOP_K):
        rk = jnp.sum(jnp.where(onehots[kk], cnt, 0.0), axis=-1, keepdims=True)
        idx_o = jnp.where(col == kk, idxs[kk], idx_o)
        wt_o = jnp.where(col == kk, wts[kk] / wsum * ROUTED_SCALE, wt_o)
        rank_o = jnp.where(col == kk, rk, rank_o)
    idx_ref[...] = idx_o.T[:ROUTE_COLS].astype(jnp.int32)
    wt_ref[...] = wt_o[:, :ROUTE_COLS]
    rank_ref[...] = rank_o.T[:ROUTE_COLS].astype(jnp.int32)
    carry_ref[...] = carry_ref[...] + jnp.sum(mask, axis=0, keepdims=True)
    cnt_ref[...] = carry_ref[...]


def _out_ln_kernel(x_ref, w_ref, h_ref, g_ref, b_ref, wr_ref, br_ref, tri_ref, o_ref, ob_ref,
                   op_ref, idx_ref, wt_ref, rank_ref, cnt_ref, carry_ref):
    y = jnp.dot(x_ref[...], w_ref[...], preferred_element_type=F32)
    out = _layernorm_rows(ALPHA * h_ref[...] + y, g_ref[...], b_ref[...])
    o_ref[...] = out
    ob_ref[...] = out.astype(BF16)
    op_ref[...] = _pack_halves(out)
    _route_tile(out, wr_ref, br_ref, tri_ref, idx_ref, wt_ref, rank_ref, cnt_ref, carry_ref)


def out_ln_route(merged, w_out, h, g, b, w_router, b_router, tm=512):
    m, k = merged.shape
    n = w_out.shape[1]
    ne = w_router.shape[1]
    tm = min(tm, m)
    r = jnp.arange(tm)
    tri = (r[:, None] > r[None, :]).astype(BF16)
    rows = pl.BlockSpec((tm, n), lambda i: (i, 0))
    vec = pl.BlockSpec((1, n), lambda i: (0, 0))
    small = pl.BlockSpec((tm, ROUTE_COLS), lambda i: (i, 0))
    small_t = pl.BlockSpec((ROUTE_COLS, tm), lambda i: (0, i))
    return pl.pallas_call(
        _out_ln_kernel,
        grid=(m // tm,),
        in_specs=[pl.BlockSpec((tm, k), lambda i: (i, 0)),
                  pl.BlockSpec((k, n), lambda i: (0, 0)),
                  rows, vec, vec,
                  pl.BlockSpec((n, ne), lambda i: (0, 0)),
                  pl.BlockSpec((1, ne), lambda i: (0, 0)),
                  pl.BlockSpec((tm, tm), lambda i: (0, 0))],
        out_specs=[rows, rows, pl.BlockSpec((tm, n // 2), lambda i: (i, 0)),
                   small_t, small, small_t, pl.BlockSpec((1, ne), lambda i: (0, 0))],
        out_shape=[jax.ShapeDtypeStruct((m, n), F32), jax.ShapeDtypeStruct((m, n), BF16),
                   jax.ShapeDtypeStruct((m, n // 2), jnp.uint32),
                   jax.ShapeDtypeStruct((ROUTE_COLS, m), jnp.int32),
                   jax.ShapeDtypeStruct((m, ROUTE_COLS), F32),
                   jax.ShapeDtypeStruct((ROUTE_COLS, m), jnp.int32),
                   jax.ShapeDtypeStruct((1, ne), F32)],
        scratch_shapes=[pltpu.VMEM((1, ne), F32)],
        compiler_params=_params("arbitrary"),
        name="out_ln_route",
    )(merged, w_out, h, g, b, w_router, b_router, tri)


def _experts_kernel(be_ref, bv_ref, li_ref, x_ref, wg_ref, wu_ref, wd_ref, o_ref, wgb_ref,
                    wub_ref, wdb_ref):
    i = pl.program_id(0)

    @pl.when(jnp.logical_or(i == 0, be_ref[i] != be_ref[jnp.maximum(i - 1, 0)]))
    def _():
        wgb_ref[...] = wg_ref[0, 0].astype(BF16)
        wub_ref[...] = wu_ref[0, 0].astype(BF16)
        wdb_ref[...] = wd_ref[0, 0].astype(BF16)

    @pl.when(bv_ref[i] > 0)
    def _():
        half = x_ref.shape[1]
        xa, xb = _unpack_halves(x_ref[...])
        xa = xa.astype(BF16)
        xb = xb.astype(BF16)
        g = (jnp.dot(xa, wgb_ref[:half, :], preferred_element_type=F32)
             + jnp.dot(xb, wgb_ref[half:, :], preferred_element_type=F32))
        u = (jnp.dot(xa, wub_ref[:half, :], preferred_element_type=F32)
             + jnp.dot(xb, wub_ref[half:, :], preferred_element_type=F32))
        a = (g * _sigmoid(g) * u).astype(BF16)
        o_ref[...] = _pack_halves(jnp.dot(a, wdb_ref[...], preferred_element_type=F32))

    @pl.when(bv_ref[i] == 0)
    def _():
        o_ref[...] = jnp.zeros_like(o_ref)


def experts(xs, block_e, block_valid, layer, w_gate, w_up, w_down):
    r, half = xs.shape
    d = 2 * half
    ff = w_gate.shape[3]
    nblk = r // MOE_BLOCK
    return pl.pallas_call(
        _experts_kernel,
        grid_spec=pltpu.PrefetchScalarGridSpec(
            num_scalar_prefetch=3,
            grid=(nblk,),
            in_specs=[pl.BlockSpec((MOE_BLOCK, half), lambda i, be, bv, li: (i, 0)),
                      pl.BlockSpec((1, 1, d, ff), lambda i, be, bv, li: (li[0], be[i], 0, 0)),
                      pl.BlockSpec((1, 1, d, ff), lambda i, be, bv, li: (li[0], be[i], 0, 0)),
                      pl.BlockSpec((1, 1, ff, d), lambda i, be, bv, li: (li[0], be[i], 0, 0))],
            out_specs=pl.BlockSpec((MOE_BLOCK, half), lambda i, be, bv, li: (i, 0)),
            scratch_shapes=[pltpu.VMEM((d, ff), BF16), pltpu.VMEM((d, ff), BF16),
                            pltpu.VMEM((ff, d), BF16)]),
        out_shape=jax.ShapeDtypeStruct((r, half), jnp.uint32),
        compiler_params=_params("arbitrary"),
        name="experts",
    )(block_e, block_valid, layer, xs, w_gate, w_up, w_down)


def _shared_ln_kernel(xb_ref, h_ref, yg_ref, wt_ref, wgu_ref, wd_ref, g_ref, b_ref, o_ref,
                      ob_ref):
    ff = wd_ref.shape[0]
    gu = jnp.dot(xb_ref[...], wgu_ref[...], preferred_element_type=F32)
    g = gu[:, :ff]
    a = (g * _sigmoid(g) * gu[:, ff:]).astype(BF16)
    shared = jnp.dot(a, wd_ref[...], preferred_element_type=F32)
    wt = wt_ref[...]
    r_lo = r_hi = None
    for kk in range(TOP_K):
        lo, hi = _unpack_halves(yg_ref[kk])
        wk = wt[:, kk:kk + 1]
        r_lo = wk * lo if r_lo is None else r_lo + wk * lo
        r_hi = wk * hi if r_hi is None else r_hi + wk * hi
    routed = jnp.concatenate([r_lo, r_hi], axis=1)
    out = _layernorm_rows(ALPHA * h_ref[...] + (routed + shared), g_ref[...], b_ref[...])
    o_ref[...] = out
    ob_ref[...] = out.astype(BF16)


def shared_ln(h_bf, h, yg, wt, w_gu, w_down, g, b, tm=256):
    m, d = h.shape
    ff2 = w_gu.shape[1]
    tm = min(tm, m)
    row = pl.BlockSpec((tm, d), lambda i: (i, 0))
    vec = pl.BlockSpec((1, d), lambda i: (0, 0))
    return pl.pallas_call(
        _shared_ln_kernel,
        grid=(m // tm,),
        in_specs=[row, row,
                  pl.BlockSpec((TOP_K, tm, d // 2), lambda i: (0, i, 0)),
                  pl.BlockSpec((tm, ROUTE_COLS), lambda i: (i, 0)),
                  pl.BlockSpec((d, ff2), lambda i: (0, 0)),
                  pl.BlockSpec((ff2 // 2, d), lambda i: (0, 0)),
                  vec, vec],
        out_specs=[row, row],
        out_shape=[jax.ShapeDtypeStruct((m, d), F32), jax.ShapeDtypeStruct((m, d), BF16)],
        compiler_params=_params("parallel"),
        name="shared_ln",
    )(h_bf, h, yg, wt, w_gu, w_down, g, b)


def _mixer_layout(d):
    rq = RET_H * RET_DK
    rv = RET_H * RET_DV
    fq = FOX_H * FOX_DH
    main = 2 * rq + 2 * rv + 3 * fq
    ff0 = main
    cu0 = ff0 + FOX_H
    gl0 = cu0 + 2 * d
    end = gl0 + N_BRANCH * d
    return dict(rq=rq, rv=rv, fq=fq, main=main, ff0=ff0, cu0=cu0, gl0=gl0, end=end)


def _prep_in_weights(w_in, b_in, d):
    lay = _mixer_layout(d)
    pad = LANES - FOX_H
    w = jnp.concatenate([w_in[:, :lay["main"]], w_in[:, lay["cu0"]:lay["end"]]],
                        axis=1).astype(BF16)
    bb = jnp.concatenate([b_in[:lay["main"]], b_in[lay["cu0"]:lay["end"]]])
    w_ff = jnp.pad(w_in[:, lay["ff0"]:lay["cu0"]], ((0, 0), (0, pad))).astype(BF16)
    b_ff = jnp.pad(b_in[lay["ff0"]:lay["cu0"]], (0, pad))
    return w, bb.reshape(1, -1).astype(F32), w_ff, b_ff.reshape(1, -1).astype(F32)


def _moe_sublayer(h, h_bf, h_pk, routing, layer, w_gate, w_up, w_down, w_sh_gu, w_sh_down, g, b):
    t, d = h.shape
    idx8, wt8, rank8, cnt = routing
    ne = cnt.shape[1]
    idx = idx8[:TOP_K]
    rank = rank8[:TOP_K]
    sizes = cnt[0].astype(jnp.int32)
    a = t * TOP_K
    n_blocks = -(-a // MOE_BLOCK) + ne
    r = n_blocks * MOE_BLOCK
    padded = (sizes + MOE_BLOCK - 1) // MOE_BLOCK * MOE_BLOCK
    pad_end = jnp.cumsum(padded)
    pad_start = pad_end - padded
    start_of = jnp.sum(jnp.where(idx[..., None] == jnp.arange(ne, dtype=jnp.int32), pad_start, 0),
                       axis=-1)
    dest = lax.optimization_barrier(start_of + rank)
    shift = (t + MOE_BLOCK).bit_length()
    tok = jnp.arange(t, dtype=jnp.int32)[None, :]
    real_keys = ((idx << shift) | tok).reshape(a)
    slot = jnp.arange(r - a, dtype=jnp.int32)
    slot_e = jnp.minimum(slot // MOE_BLOCK, ne - 1)
    slot_q = slot % MOE_BLOCK
    needed = (slot < ne * MOE_BLOCK) & (slot_q < (padded - sizes)[slot_e])
    pad_keys = jnp.where(needed, (slot_e << shift) | (t + slot_q), (ne << shift) + slot)
    order = jnp.sort(jnp.concatenate([real_keys, pad_keys]))
    row_tok = (order & ((1 << shift) - 1)) % t
    blk0 = jnp.arange(n_blocks, dtype=jnp.int32) * MOE_BLOCK
    block_e = jnp.minimum(jnp.sum((pad_end[None, :] <= blk0[:, None]).astype(jnp.int32), axis=1),
                          ne - 1)
    block_valid = (blk0 < pad_end[-1]).astype(jnp.int32)
    xs = h_pk[row_tok]
    ys = experts(xs, block_e, block_valid, layer, w_gate, w_up, w_down)
    yg = ys[dest.reshape(a)].reshape(TOP_K, t, d // 2)
    return shared_ln(h_bf, h, yg, wt8, w_sh_gu, w_sh_down, g, b)


def kernel(x, w_in, b_in, ret_gn_g, conv_w, conv_b, conv_ln_g, conv_ln_b, w_ret_o, w_fox_o,
           w_conv_o, w_out, ln1_g, ln1_b, w_router, b_router, w_exp_gate, w_exp_up, w_exp_down,
           w_sh_gate, w_sh_up, w_sh_down, ln2_g, ln2_b):
    bsz, seq, d = x.shape
    t = bsz * seq
    lay = _mixer_layout(d)
    chunk = min(RET_CHUNK, seq)
    tables = _retention_tables(seq, chunk)
    fq0 = 2 * lay["rq"] + 2 * lay["rv"]

    def row(v):
        return v.reshape(1, -1)

    def layer(carry, p):
        h, h_bf = carry
        w_cat, b_cat, w_ff, b_ff = _prep_in_weights(p["w_in"], p["b_in"], d)
        z2 = in_proj(h_bf, w_cat, b_cat)
        z = z2.reshape(bsz, seq, -1)
        ret_o = retention(z, row(p["ret_gn_g"]), tables)
        negc = fox_cumsum(h_bf.reshape(bsz, seq, d), w_ff, b_ff)
        fox_o = fox_attention(z, negc, fq0 // FOX_DH, (fq0 + lay["fq"]) // FOX_DH,
                              (fq0 + 2 * lay["fq"]) // FOX_DH)
        conv_o = conformer_conv(z, lay["main"] // d, lay["main"] // d + 1, p["conv_w"],
                                row(p["conv_b"]), row(p["conv_ln_g"]), row(p["conv_ln_b"]))
        merged = merge(ret_o.reshape(t, -1), fox_o.reshape(t, -1), conv_o.reshape(t, -1),
                       p["w_ret_o"].astype(BF16), p["w_fox_o"].astype(BF16),
                       p["w_conv_o"].astype(BF16), z2, (lay["main"] + 2 * d) // d)
        h, h_bf, h_pk, *routing = out_ln_route(
            merged, p["w_out"].astype(BF16), h, row(p["ln1_g"]), row(p["ln1_b"]),
            p["w_router"], row(p["b_router"]))
        w_sh_gu = jnp.concatenate([p["w_sh_gate"], p["w_sh_up"]], axis=-1).astype(BF16)
        h, h_bf = _moe_sublayer(h, h_bf, h_pk, routing, p["layer"],
                                w_exp_gate, w_exp_up, w_exp_down, w_sh_gu,
                                p["w_sh_down"].astype(BF16), row(p["ln2_g"]), row(p["ln2_b"]))
        return (h, h_bf), None

    depth = w_in.shape[0]
    params = dict(w_in=w_in, b_in=b_in, ret_gn_g=ret_gn_g, conv_w=conv_w, conv_b=conv_b,
                  conv_ln_g=conv_ln_g, conv_ln_b=conv_ln_b, w_ret_o=w_ret_o, w_fox_o=w_fox_o,
                  w_conv_o=w_conv_o, w_out=w_out, ln1_g=ln1_g, ln1_b=ln1_b, w_router=w_router,
                  b_router=b_router, w_sh_gate=w_sh_gate, w_sh_up=w_sh_up,
                  w_sh_down=w_sh_down, ln2_g=ln2_g, ln2_b=ln2_b,
                  layer=jnp.arange(depth, dtype=jnp.int32).reshape(depth, 1))
    h0 = x.reshape(t, d)
    (h, _), _ = lax.scan(layer, (h0, h0.astype(BF16)), params)
    return h.reshape(bsz, seq, d)
```

```python
import functools

import jax
import jax.numpy as jnp
from jax import lax
from jax.experimental import pallas as pl
from jax.experimental.pallas import tpu as pltpu

F32 = jnp.float32
BF16 = jnp.bfloat16

RET_H = 8
RET_DK = 128
RET_DV = 256
ROPE_BASE = 10000.0
FOX_H = 16
FOX_DH = 128
CONV_W = 31
N_BRANCH = 3
TOP_K = 6
ROUTED_SCALE = 2.5
DEPTH_FOR_NORM = 4
ALPHA = (2 * DEPTH_FOR_NORM) ** 0.25
LN_EPS = 1e-5

LANES = 128
SUBLANES = 8
VMEM_LIMIT = 56 * 1024 * 1024

IN_TM = 1024
IN_TN = 2048
RET_CHUNK = 256
FOX_TQ = 2048
FOX_TK = 512
FOX_HG = 1
CUM_BLK = 256
CONV_TS = 256
CONV_HALO = 32
CONV_RB = 256
CONV_CB = 128
MOE_BLOCK = 512
NEG_BIG = -1e30
LOG2E = 1.4426950408889634


def _params(*sem):
    return pltpu.CompilerParams(dimension_semantics=sem, vmem_limit_bytes=VMEM_LIMIT)


def _sigmoid(x):
    return 1.0 / (1.0 + jnp.exp(-x))


def _pack_halves(x):
    c = x.shape[1] // 2
    xb = x.astype(BF16).astype(F32)
    lo = lax.bitcast_convert_type(xb[:, :c], jnp.uint32) >> 16
    hi = lax.bitcast_convert_type(xb[:, c:], jnp.uint32) & jnp.uint32(0xFFFF0000)
    return lo | hi


def _unpack_halves(u):
    lo = lax.bitcast_convert_type(u << 16, F32)
    hi = lax.bitcast_convert_type(u & jnp.uint32(0xFFFF0000), F32)
    return lo, hi


def _layernorm_rows(x, g, b):
    mu = jnp.mean(x, axis=-1, keepdims=True)
    xc = x - mu
    var = jnp.mean(xc * xc, axis=-1, keepdims=True)
    return xc * lax.rsqrt(var + LN_EPS) * g + b


def _in_proj_kernel(x_ref, w_ref, b_ref, o_ref):
    acc = jnp.dot(x_ref[...], w_ref[...], preferred_element_type=F32)
    o_ref[...] = acc + b_ref[...]


def in_proj(x_bf, w_bf, b, tm=IN_TM, tn=IN_TN):
    m, k = x_bf.shape
    n = w_bf.shape[1]
    tm = min(tm, m)
    assert m % tm == 0 and n % tn == 0
    return pl.pallas_call(
        _in_proj_kernel,
        grid=(m // tm, n // tn),
        in_specs=[pl.BlockSpec((tm, k), lambda i, j: (i, 0)),
                  pl.BlockSpec((k, tn), lambda i, j: (0, j)),
                  pl.BlockSpec((1, tn), lambda i, j: (0, j))],
        out_specs=pl.BlockSpec((tm, tn), lambda i, j: (i, j)),
        out_shape=jax.ShapeDtypeStruct((m, n), F32),
        compiler_params=_params("parallel", "parallel"),
        name="in_proj",
    )(x_bf, w_bf, b)


def _retention_kernel(cd_ref, q_ref, k_ref, v_ref, g_ref, cos_ref, sin_ref, decay_ref,
                      xi_ref, zeta_ref, gn_ref, o_ref, state_ref):
    @pl.when(pl.program_id(1) == 0)
    def _():
        state_ref[...] = jnp.zeros_like(state_ref)

    cos = cos_ref[...]
    sin = sin_ref[...]
    for h in range(RET_H):
        qs = slice(h * RET_DK, (h + 1) * RET_DK)
        vs = slice(h * RET_DV, (h + 1) * RET_DV)
        q = q_ref[0, :, qs]
        k = k_ref[0, :, qs]
        qr = q * cos + pltpu.roll(q, RET_DK // 2, 1) * sin
        kr = (k * cos + pltpu.roll(k, RET_DK // 2, 1) * sin) * (RET_DK ** -0.5)
        vb = v_ref[0, :, vs].astype(BF16)
        s = lax.dot_general(qr.astype(BF16), kr.astype(BF16), (((1,), (1,)), ((), ())),
                            preferred_element_type=F32)
        inner = (s * decay_ref[h]).astype(BF16)
        o = jnp.dot(inner, vb, preferred_element_type=F32)
        st = state_ref[h]
        o = o + jnp.dot((qr * xi_ref[h]).astype(BF16), st.astype(BF16),
                        preferred_element_type=F32)
        kz = (kr * zeta_ref[h]).astype(BF16)
        state_ref[h] = st * cd_ref[h] + lax.dot_general(
            kz, vb, (((0,), (0,)), ((), ())), preferred_element_type=F32)
        mu = jnp.mean(o, axis=-1, keepdims=True)
        oc = o - mu
        var = jnp.mean(oc * oc, axis=-1, keepdims=True)
        on = oc * lax.rsqrt(var + LN_EPS) * gn_ref[:, vs]
        g = g_ref[0, :, vs]
        o_ref[0, :, vs] = (g * _sigmoid(g) * on).astype(BF16)


def _retention_tables(seq, chunk):
    h = jnp.arange(RET_H, dtype=F32)
    log_g = jnp.log1p(-jnp.exp2(-5.0 - h))
    i = jnp.arange(chunk, dtype=F32)
    diff = i[:, None] - i[None, :]
    decay = jnp.where(diff >= 0, jnp.exp(log_g[:, None, None] * jnp.maximum(diff, 0.0)), 0.0)
    xi = jnp.exp(log_g[:, None] * (i + 1.0))
    zeta = jnp.exp(log_g[:, None] * (chunk - 1.0 - i))
    xi = jnp.broadcast_to(xi[:, :, None], (RET_H, chunk, RET_DK))
    zeta = jnp.broadcast_to(zeta[:, :, None], (RET_H, chunk, RET_DK))
    cd = jnp.exp(log_g * chunk)
    half = RET_DK // 2
    inv = 1.0 / (ROPE_BASE ** (jnp.arange(half, dtype=F32) / half))
    ang = jnp.arange(seq, dtype=F32)[:, None] * inv[None, :]
    cos = jnp.cos(ang)
    sin = jnp.sin(ang)
    cos2 = jnp.concatenate([cos, cos], axis=-1)
    sin2 = jnp.concatenate([-sin, sin], axis=-1)
    return cd, cos2, sin2, decay, xi, zeta


def retention(z, gn_g, tables):
    b, s, _ = z.shape
    cd, cos2, sin2, decay, xi, zeta = tables
    L = decay.shape[1]
    qw = RET_H * RET_DK
    vw = RET_H * RET_DV
    return pl.pallas_call(
        _retention_kernel,
        grid=(b, s // L),
        in_specs=[pl.BlockSpec(memory_space=pltpu.SMEM),
                  pl.BlockSpec((1, L, qw), lambda bi, n: (bi, n, 0)),
                  pl.BlockSpec((1, L, qw), lambda bi, n: (bi, n, 1)),
                  pl.BlockSpec((1, L, vw), lambda bi, n: (bi, n, 1)),
                  pl.BlockSpec((1, L, vw), lambda bi, n: (bi, n, 2)),
                  pl.BlockSpec((L, RET_DK), lambda bi, n: (n, 0)),
                  pl.BlockSpec((L, RET_DK), lambda bi, n: (n, 0)),
                  pl.BlockSpec((RET_H, L, L), lambda bi, n: (0, 0, 0)),
                  pl.BlockSpec((RET_H, L, RET_DK), lambda bi, n: (0, 0, 0)),
                  pl.BlockSpec((RET_H, L, RET_DK), lambda bi, n: (0, 0, 0)),
                  pl.BlockSpec((1, vw), lambda bi, n: (0, 0))],
        out_specs=pl.BlockSpec((1, L, vw), lambda bi, n: (bi, n, 0)),
        out_shape=jax.ShapeDtypeStruct((b, s, vw), BF16),
        scratch_shapes=[pltpu.VMEM((RET_H, RET_DK, RET_DV), F32)],
        compiler_params=_params("parallel", "arbitrary"),
        name="retention",
    )(cd, z, z, z, z, cos2, sin2, decay, xi, zeta, gn_g)


def _fox_cumsum_kernel(h_ref, w_ref, b_ref, tri_ref, o_ref):
    s = h_ref.shape[1]
    tri = tri_ref[...]
    carry = jnp.zeros((1, LANES), F32)
    for i in range(s // CUM_BLK):
        f = jnp.dot(h_ref[0, i * CUM_BLK:(i + 1) * CUM_BLK, :], w_ref[...],
                    preferred_element_type=F32) + b_ref[...]
        ls = jnp.minimum(f, 0.0) - jnp.log1p(jnp.exp(-jnp.abs(f)))
        hi = ls.astype(BF16)
        r1 = ls - hi.astype(F32)
        mid = r1.astype(BF16)
        lo = (r1 - mid.astype(F32)).astype(BF16)
        c = (jnp.dot(tri, hi, preferred_element_type=F32)
             + jnp.dot(tri, mid, preferred_element_type=F32)
             + jnp.dot(tri, lo, preferred_element_type=F32)) + carry
        carry = c[CUM_BLK - 1:CUM_BLK, :]
        o_ref[0, :, i * CUM_BLK:(i + 1) * CUM_BLK] = (-c).T[:FOX_H, :]


def fox_cumsum(h_bf, w_ff, b_ff):
    b, s, d = h_bf.shape
    r = jnp.arange(CUM_BLK)
    tri = (r[:, None] >= r[None, :]).astype(BF16)
    return pl.pallas_call(
        _fox_cumsum_kernel,
        grid=(b,),
        in_specs=[pl.BlockSpec((1, s, d), lambda bi: (bi, 0, 0)),
                  pl.BlockSpec((d, LANES), lambda bi: (0, 0)),
                  pl.BlockSpec((1, LANES), lambda bi: (0, 0)),
                  pl.BlockSpec((CUM_BLK, CUM_BLK), lambda bi: (0, 0))],
        out_specs=pl.BlockSpec((1, FOX_H, s), lambda bi: (bi, 0, 0)),
        out_shape=jax.ShapeDtypeStruct((b, FOX_H, s), F32),
        compiler_params=_params("parallel"),
        name="fox_cumsum",
    )(h_bf, w_ff, b_ff, tri)


def _fox_attn_kernel(q_ref, k_ref, v_ref, nc_ref, o_ref, kt_ref, vx_ref, qb_ref, m_ref, acc_ref):
    qi = pl.program_id(2)
    s_len = k_ref.shape[1]

    def head(g):
        return slice(g * FOX_DH, (g + 1) * FOX_DH)

    @pl.when(qi == 0)
    def _():
        for g in range(FOX_HG):
            for c in range(s_len // FOX_TK):
                rows = slice(c * FOX_TK, (c + 1) * FOX_TK)
                kt_ref[g, :, rows] = k_ref[0, rows, head(g)].T.astype(BF16)
                vx_ref[g, rows, :FOX_DH] = v_ref[0, rows, head(g)].astype(BF16)
            vx_ref[g, :, FOX_DH:] = jnp.ones((s_len, FOX_DH), BF16)

    for g in range(FOX_HG):
        qb_ref[g] = (q_ref[0, :, head(g)] * (FOX_DH ** -0.5 * LOG2E)).astype(BF16)
    m_ref[...] = jnp.full_like(m_ref, NEG_BIG)
    acc_ref[...] = jnp.zeros_like(acc_ref)

    def tile(j, diag):
        k0 = pl.multiple_of(j * FOX_TK, FOX_TK)
        rows = slice(0 if diag is None else diag * FOX_TK, FOX_TQ)
        for g in range(FOX_HG):
            s = jnp.dot(qb_ref[g, rows, :], kt_ref[g, :, pl.ds(k0, FOX_TK)],
                        preferred_element_type=F32)
            s = s + nc_ref[0, g, :, pl.ds(k0, FOX_TK)] * LOG2E
            if diag is not None:
                row = lax.broadcasted_iota(jnp.int32, s.shape, 0)
                col = lax.broadcasted_iota(jnp.int32, s.shape, 1)
                s = jnp.where(col <= row, s, NEG_BIG)
            m_prev = m_ref[g, rows, :]
            m_new = jnp.maximum(m_prev, jnp.max(s, axis=-1, keepdims=True))
            alpha = jnp.exp2(m_prev - m_new)
            p = jnp.concatenate(
                [jnp.exp2(s[:, c * LANES:(c + 1) * LANES] - m_new)
                 for c in range(FOX_TK // LANES)], axis=1).astype(BF16)
            pv = jnp.dot(p, vx_ref[g, pl.ds(k0, FOX_TK), :], preferred_element_type=F32)
            acc_ref[g, rows, :] = jnp.concatenate([alpha, alpha], axis=1) * acc_ref[g, rows, :] + pv
            m_ref[g, rows, :] = m_new

    def body(j, c):
        tile(j, None)
        return c

    per_q = FOX_TQ // FOX_TK
    n_full = qi * per_q
    lax.fori_loop(0, n_full, body, 0)
    for r in range(per_q):
        tile(n_full + r, r)
    for g in range(FOX_HG):
        o_ref[0, :, head(g)] = (acc_ref[g, :, :FOX_DH] / acc_ref[g, :, FOX_DH:]).astype(BF16)


def fox_attention(z, negc, q_blk, k_blk, v_blk):
    b, s, _ = z.shape
    assert FOX_TQ % FOX_TK == 0 and FOX_DH == LANES and FOX_H % FOX_HG == 0
    gw = FOX_HG * FOX_DH
    negc4 = negc.reshape(b, FOX_H, 1, s)
    qg, kg, vg = q_blk // FOX_HG, k_blk // FOX_HG, v_blk // FOX_HG
    return pl.pallas_call(
        _fox_attn_kernel,
        grid=(b, FOX_H // FOX_HG, s // FOX_TQ),
        in_specs=[pl.BlockSpec((1, FOX_TQ, gw), lambda bi, h, qi: (bi, qi, qg + h)),
                  pl.BlockSpec((1, s, gw), lambda bi, h, qi: (bi, 0, kg + h)),
                  pl.BlockSpec((1, s, gw), lambda bi, h, qi: (bi, 0, vg + h)),
                  pl.BlockSpec((1, FOX_HG, 1, s), lambda bi, h, qi: (bi, h, 0, 0))],
        out_specs=pl.BlockSpec((1, FOX_TQ, gw), lambda bi, h, qi: (bi, qi, h)),
        out_shape=jax.ShapeDtypeStruct((b, s, FOX_H * FOX_DH), BF16),
        scratch_shapes=[pltpu.VMEM((FOX_HG, FOX_DH, s), BF16),
                        pltpu.VMEM((FOX_HG, s, 2 * FOX_DH), BF16),
                        pltpu.VMEM((FOX_HG, FOX_TQ, FOX_DH), BF16),
                        pltpu.VMEM((FOX_HG, FOX_TQ, LANES), F32),
                        pltpu.VMEM((FOX_HG, FOX_TQ, 2 * FOX_DH), F32)],
        compiler_params=_params("parallel", "parallel", "arbitrary"),
        name="fox_attn",
    )(z, z, z, negc4)


def _conv_kernel(a_ref, g_ref, ap_ref, gp_ref, w_ref, cb_ref, lg_ref, lb_ref, o_ref, y_ref,
                 c_ref):
    ts = a_ref.shape[1]
    c_all = a_ref.shape[2]
    prev = ap_ref[0] * _sigmoid(gp_ref[0])
    y_ref[0:CONV_HALO, :] = jnp.where(pl.program_id(1) > 0, prev, 0.0)
    y_ref[CONV_HALO:CONV_HALO + ts, :] = a_ref[0] * _sigmoid(g_ref[0])
    lead = CONV_HALO - (CONV_W - 1)

    n_cb = c_all // CONV_CB
    win_rows = CONV_RB + CONV_HALO

    def block(idx, carry):
        r0 = pl.multiple_of((idx // n_cb) * CONV_RB, CONV_RB)
        c0 = pl.multiple_of((idx % n_cb) * CONV_CB, CONV_CB)
        win = y_ref[pl.ds(r0, win_rows), pl.ds(c0, CONV_CB)]
        acc = jnp.zeros((CONV_RB, CONV_CB), F32)
        for ph in range(SUBLANES):
            rot = win if ph == 0 else pltpu.roll(win, win_rows - ph, 0)
            for w in range(CONV_W):
                if (w + lead) % SUBLANES == ph:
                    a0 = (w + lead) - ph
                    acc = acc + w_ref[w:w + 1, pl.ds(c0, CONV_CB)] * rot[a0:a0 + CONV_RB]
        c_ref[pl.ds(r0, CONV_RB), pl.ds(c0, CONV_CB)] = acc + cb_ref[:, pl.ds(c0, CONV_CB)]
        return carry

    lax.fori_loop(0, (ts // CONV_RB) * n_cb, block, 0)
    y = _layernorm_rows(c_ref[...], lg_ref[...], lb_ref[...])
    o_ref[0] = (y * _sigmoid(y)).astype(BF16)


def conformer_conv(z, a_blk, g_blk, conv_w, conv_b, ln_g, ln_b):
    b, s, _ = z.shape
    c = conv_w.shape[1]
    ts = min(CONV_TS, s)
    hb = ts // CONV_HALO

    def halo(col):
        return lambda bi, i: (bi, jnp.maximum(i * hb - 1, 0), col)

    return pl.pallas_call(
        _conv_kernel,
        grid=(b, s // ts),
        in_specs=[pl.BlockSpec((1, ts, c), lambda bi, i: (bi, i, a_blk)),
                  pl.BlockSpec((1, ts, c), lambda bi, i: (bi, i, g_blk)),
                  pl.BlockSpec((1, CONV_HALO, c), halo(a_blk)),
                  pl.BlockSpec((1, CONV_HALO, c), halo(g_blk)),
                  pl.BlockSpec((CONV_W, c), lambda bi, i: (0, 0)),
                  pl.BlockSpec((1, c), lambda bi, i: (0, 0)),
                  pl.BlockSpec((1, c), lambda bi, i: (0, 0)),
                  pl.BlockSpec((1, c), lambda bi, i: (0, 0))],
        out_specs=pl.BlockSpec((1, ts, c), lambda bi, i: (bi, i, 0)),
        out_shape=jax.ShapeDtypeStruct((b, s, c), BF16),
        scratch_shapes=[pltpu.VMEM((CONV_HALO + ts, c), F32),
                        pltpu.VMEM((ts, c), F32)],
        compiler_params=_params("parallel", "parallel"),
        name="conv",
    )(z, z, z, z, conv_w, conv_b, ln_g, ln_b)


def _merge_kernel(a_ref, f_ref, c_ref, wa_ref, wf_ref, wc_ref, g0_ref, g1_ref, g2_ref, o_ref):
    ya = jnp.dot(a_ref[...], wa_ref[...], preferred_element_type=F32)
    m = _sigmoid(g0_ref[...]) * ya
    yf = jnp.dot(f_ref[...], wf_ref[...], preferred_element_type=F32)
    m = m + _sigmoid(g1_ref[...]) * yf
    yc = jnp.dot(c_ref[...], wc_ref[...], preferred_element_type=F32)
    m = m + _sigmoid(g2_ref[...]) * yc
    o_ref[...] = m.astype(BF16)


def merge(ret_o, fox_o, conv_o, w_ret, w_fox, w_conv, z2, gl_blk, tm=1024, tn=256):
    m, k = ret_o.shape
    n = w_ret.shape[1]
    tm = min(tm, m)
    nb = n // tn
    act = pl.BlockSpec((tm, k), lambda i, j: (i, 0))
    wsp = pl.BlockSpec((k, tn), lambda i, j: (0, j))

    def gate(br):
        return pl.BlockSpec((tm, tn), lambda i, j: (i, gl_blk * nb + br * nb + j))

    return pl.pallas_call(
        _merge_kernel,
        grid=(m // tm, nb),
        in_specs=[act, act, act, wsp, wsp, wsp, gate(0), gate(1), gate(2)],
        out_specs=pl.BlockSpec((tm, tn), lambda i, j: (i, j)),
        out_shape=jax.ShapeDtypeStruct((m, n), BF16),
        compiler_params=_params("parallel", "parallel"),
        name="merge",
    )(ret_o, fox_o, conv_o, w_ret, w_fox, w_conv, z2, z2, z2)


def _out_ln_kernel(x_ref, w_ref, h_ref, g_ref, b_ref, o_ref, ob_ref, op_ref):
    y = jnp.dot(x_ref[...], w_ref[...], preferred_element_type=F32)
    out = _layernorm_rows(ALPHA * h_ref[...] + y, g_ref[...], b_ref[...])
    o_ref[...] = out
    ob_ref[...] = out.astype(BF16)
    op_ref[...] = _pack_halves(out)


def out_ln(merged, w_out, h, g, b, tm=512):
    m, k = merged.shape
    n = w_out.shape[1]
    tm = min(tm, m)
    return pl.pallas_call(
        _out_ln_kernel,
        grid=(m // tm,),
        in_specs=[pl.BlockSpec((tm, k), lambda i: (i, 0)),
                  pl.BlockSpec((k, n), lambda i: (0, 0)),
                  pl.BlockSpec((tm, n), lambda i: (i, 0)),
                  pl.BlockSpec((1, n), lambda i: (0, 0)),
                  pl.BlockSpec((1, n), lambda i: (0, 0))],
        out_specs=[pl.BlockSpec((tm, n), lambda i: (i, 0)),
                   pl.BlockSpec((tm, n), lambda i: (i, 0)),
                   pl.BlockSpec((tm, n // 2), lambda i: (i, 0))],
        out_shape=[jax.ShapeDtypeStruct((m, n), F32), jax.ShapeDtypeStruct((m, n), BF16),
                   jax.ShapeDtypeStruct((m, n // 2), jnp.uint32)],
        compiler_params=_params("parallel"),
        name="out_ln",
    )(merged, w_out, h, g, b)


ROUTE_COLS = 8


def _router_kernel(h_ref, w_ref, b_ref, tri_ref, idx_ref, wt_ref, rank_ref, cnt_ref, carry_ref):
    @pl.when(pl.program_id(0) == 0)
    def _():
        carry_ref[...] = jnp.zeros_like(carry_ref)

    h = h_ref[...]
    hh = h.astype(BF16)
    hl = (h - hh.astype(F32)).astype(BF16)
    w = w_ref[...]
    wh = w.astype(BF16)
    wl = (w - wh.astype(F32)).astype(BF16)
    logits = (jnp.dot(hh, wh, preferred_element_type=F32)
              + jnp.dot(hl, wh, preferred_element_type=F32)
              + jnp.dot(hh, wl, preferred_element_type=F32))
    scores = _sigmoid(logits)
    tm, ne = scores.shape
    sel = scores + b_ref[...]
    lane = lax.broadcasted_iota(jnp.int32, (tm, ne), 1).astype(F32)
    mask = jnp.zeros((tm, ne), F32)
    onehots, idxs, wts = [], [], []
    for _ in range(TOP_K):
        mx = jnp.max(sel, axis=-1, keepdims=True)
        ik = jnp.min(jnp.where(sel == mx, lane, float(ne)), axis=-1, keepdims=True)
        oh = lane == ik
        wts.append(jnp.sum(jnp.where(oh, scores, 0.0), axis=-1, keepdims=True))
        idxs.append(ik)
        onehots.append(oh)
        sel = jnp.where(oh, -jnp.inf, sel)
        mask = jnp.where(oh, 1.0, mask)
    wsum = wts[0]
    for t in wts[1:]:
        wsum = wsum + t
    cnt = jnp.dot(tri_ref[...], mask.astype(BF16), preferred_element_type=F32) + carry_ref[...]
    col = lax.broadcasted_iota(jnp.int32, (tm, LANES), 1)
    idx_o = jnp.zeros((tm, LANES), F32)
    wt_o = jnp.zeros((tm, LANES), F32)
    rank_o = jnp.zeros((tm, LANES), F32)
    for kk in range(TOP_K):
        rk = jnp.sum(jnp.where(onehots[kk], cnt, 0.0), axis=-1, keepdims=True)
        idx_o = jnp.where(col == kk, idxs[kk], idx_o)
        wt_o = jnp.where(col == kk, wts[kk] / wsum * ROUTED_SCALE, wt_o)
        rank_o = jnp.where(col == kk, rk, rank_o)
    idx_ref[...] = idx_o.T[:ROUTE_COLS].astype(jnp.int32)
    wt_ref[...] = wt_o[:, :ROUTE_COLS]
    rank_ref[...] = rank_o.T[:ROUTE_COLS].astype(jnp.int32)
    carry_ref[...] = carry_ref[...] + jnp.sum(mask, axis=0, keepdims=True)
    cnt_ref[...] = carry_ref[...]


def router(h, w_router, b_router, tm=512):
    m, k = h.shape
    ne = w_router.shape[1]
    tm = min(tm, m)
    r = jnp.arange(tm)
    tri = (r[:, None] > r[None, :]).astype(BF16)
    small = pl.BlockSpec((tm, ROUTE_COLS), lambda i: (i, 0))
    small_t = pl.BlockSpec((ROUTE_COLS, tm), lambda i: (0, i))
    return pl.pallas_call(
        _router_kernel,
        grid=(m // tm,),
        in_specs=[pl.BlockSpec((tm, k), lambda i: (i, 0)),
                  pl.BlockSpec((k, ne), lambda i: (0, 0)),
                  pl.BlockSpec((1, ne), lambda i: (0, 0)),
                  pl.BlockSpec((tm, tm), lambda i: (0, 0))],
        out_specs=[small_t, small, small_t, pl.BlockSpec((1, ne), lambda i: (0, 0))],
        out_shape=[jax.ShapeDtypeStruct((ROUTE_COLS, m), jnp.int32),
                   jax.ShapeDtypeStruct((m, ROUTE_COLS), F32),
                   jax.ShapeDtypeStruct((ROUTE_COLS, m), jnp.int32),
                   jax.ShapeDtypeStruct((1, ne), F32)],
        scratch_shapes=[pltpu.VMEM((1, ne), F32)],
        compiler_params=_params("arbitrary"),
        name="router",
    )(h, w_router, b_router, tri)


def _experts_kernel(be_ref, bv_ref, li_ref, x_ref, wg_ref, wu_ref, wd_ref, o_ref, wgb_ref,
                    wub_ref, wdb_ref):
    i = pl.program_id(0)

    @pl.when(jnp.logical_or(i == 0, be_ref[i] != be_ref[jnp.maximum(i - 1, 0)]))
    def _():
        wgb_ref[...] = wg_ref[0, 0].astype(BF16)
        wub_ref[...] = wu_ref[0, 0].astype(BF16)
        wdb_ref[...] = wd_ref[0, 0].astype(BF16)

    @pl.when(bv_ref[i] > 0)
    def _():
        half = x_ref.shape[1]
        xa, xb = _unpack_halves(x_ref[...])
        xa = xa.astype(BF16)
        xb = xb.astype(BF16)
        g = (jnp.dot(xa, wgb_ref[:half, :], preferred_element_type=F32)
             + jnp.dot(xb, wgb_ref[half:, :], preferred_element_type=F32))
        u = (jnp.dot(xa, wub_ref[:half, :], preferred_element_type=F32)
             + jnp.dot(xb, wub_ref[half:, :], preferred_element_type=F32))
        a = (g * _sigmoid(g) * u).astype(BF16)
        o_ref[...] = _pack_halves(jnp.dot(a, wdb_ref[...], preferred_element_type=F32))

    @pl.when(bv_ref[i] == 0)
    def _():
        o_ref[...] = jnp.zeros_like(o_ref)


def experts(xs, block_e, block_valid, layer, w_gate, w_up, w_down):
    r, half = xs.shape
    d = 2 * half
    ff = w_gate.shape[3]
    nblk = r // MOE_BLOCK
    return pl.pallas_call(
        _experts_kernel,
        grid_spec=pltpu.PrefetchScalarGridSpec(
            num_scalar_prefetch=3,
            grid=(nblk,),
            in_specs=[pl.BlockSpec((MOE_BLOCK, half), lambda i, be, bv, li: (i, 0)),
                      pl.BlockSpec((1, 1, d, ff), lambda i, be, bv, li: (li[0], be[i], 0, 0)),
                      pl.BlockSpec((1, 1, d, ff), lambda i, be, bv, li: (li[0], be[i], 0, 0)),
                      pl.BlockSpec((1, 1, ff, d), lambda i, be, bv, li: (li[0], be[i], 0, 0))],
            out_specs=pl.BlockSpec((MOE_BLOCK, half), lambda i, be, bv, li: (i, 0)),
            scratch_shapes=[pltpu.VMEM((d, ff), BF16), pltpu.VMEM((d, ff), BF16),
                            pltpu.VMEM((ff, d), BF16)]),
        out_shape=jax.ShapeDtypeStruct((r, half), jnp.uint32),
        compiler_params=_params("arbitrary"),
        name="experts",
    )(block_e, block_valid, layer, xs, w_gate, w_up, w_down)


def _shared_ln_kernel(xb_ref, h_ref, yg_ref, wt_ref, wgu_ref, wd_ref, g_ref, b_ref, o_ref,
                      ob_ref):
    ff = wd_ref.shape[0]
    gu = jnp.dot(xb_ref[...], wgu_ref[...], preferred_element_type=F32)
    g = gu[:, :ff]
    a = (g * _sigmoid(g) * gu[:, ff:]).astype(BF16)
    shared = jnp.dot(a, wd_ref[...], preferred_element_type=F32)
    wt = wt_ref[...]
    r_lo = r_hi = None
    for kk in range(TOP_K):
        lo, hi = _unpack_halves(yg_ref[kk])
        wk = wt[:, kk:kk + 1]
        r_lo = wk * lo if r_lo is None else r_lo + wk * lo
        r_hi = wk * hi if r_hi is None else r_hi + wk * hi
    routed = jnp.concatenate([r_lo, r_hi], axis=1)
    out = _layernorm_rows(ALPHA * h_ref[...] + (routed + shared), g_ref[...], b_ref[...])
    o_ref[...] = out
    ob_ref[...] = out.astype(BF16)


def shared_ln(h_bf, h, yg, wt, w_gu, w_down, g, b, tm=256):
    m, d = h.shape
    ff2 = w_gu.shape[1]
    tm = min(tm, m)
    row = pl.BlockSpec((tm, d), lambda i: (i, 0))
    vec = pl.BlockSpec((1, d), lambda i: (0, 0))
    return pl.pallas_call(
        _shared_ln_kernel,
        grid=(m // tm,),
        in_specs=[row, row,
                  pl.BlockSpec((TOP_K, tm, d // 2), lambda i: (0, i, 0)),
                  pl.BlockSpec((tm, ROUTE_COLS), lambda i: (i, 0)),
                  pl.BlockSpec((d, ff2), lambda i: (0, 0)),
                  pl.BlockSpec((ff2 // 2, d), lambda i: (0, 0)),
                  vec, vec],
        out_specs=[row, row],
        out_shape=[jax.ShapeDtypeStruct((m, d), F32), jax.ShapeDtypeStruct((m, d), BF16)],
        compiler_params=_params("parallel"),
        name="shared_ln",
    )(h_bf, h, yg, wt, w_gu, w_down, g, b)


def _mixer_layout(d):
    rq = RET_H * RET_DK
    rv = RET_H * RET_DV
    fq = FOX_H * FOX_DH
    main = 2 * rq + 2 * rv + 3 * fq
    ff0 = main
    cu0 = ff0 + FOX_H
    gl0 = cu0 + 2 * d
    end = gl0 + N_BRANCH * d
    return dict(rq=rq, rv=rv, fq=fq, main=main, ff0=ff0, cu0=cu0, gl0=gl0, end=end)


def _prep_in_weights(w_in, b_in, d):
    lay = _mixer_layout(d)
    pad = LANES - FOX_H
    w = jnp.concatenate([w_in[:, :lay["main"]], w_in[:, lay["cu0"]:lay["end"]]],
                        axis=1).astype(BF16)
    bb = jnp.concatenate([b_in[:lay["main"]], b_in[lay["cu0"]:lay["end"]]])
    w_ff = jnp.pad(w_in[:, lay["ff0"]:lay["cu0"]], ((0, 0), (0, pad))).astype(BF16)
    b_ff = jnp.pad(b_in[lay["ff0"]:lay["cu0"]], (0, pad))
    return w, bb.reshape(1, -1).astype(F32), w_ff, b_ff.reshape(1, -1).astype(F32)


def _moe_sublayer(h, h_bf, h_pk, routing, layer, w_gate, w_up, w_down, w_sh_gu, w_sh_down, g, b):
    t, d = h.shape
    idx8, wt8, rank8, cnt = routing
    ne = cnt.shape[1]
    idx = idx8[:TOP_K]
    rank = rank8[:TOP_K]
    sizes = cnt[0].astype(jnp.int32)
    a = t * TOP_K
    n_blocks = -(-a // MOE_BLOCK) + ne
    r = n_blocks * MOE_BLOCK
    padded = (sizes + MOE_BLOCK - 1) // MOE_BLOCK * MOE_BLOCK
    pad_end = jnp.cumsum(padded)
    pad_start = pad_end - padded
    start_of = jnp.sum(jnp.where(idx[..., None] == jnp.arange(ne, dtype=jnp.int32), pad_start, 0),
                       axis=-1)
    dest = lax.optimization_barrier(start_of + rank)
    shift = (t + MOE_BLOCK).bit_length()
    tok = jnp.arange(t, dtype=jnp.int32)[None, :]
    real_keys = ((idx << shift) | tok).reshape(a)
    slot = jnp.arange(r - a, dtype=jnp.int32)
    slot_e = jnp.minimum(slot // MOE_BLOCK, ne - 1)
    slot_q = slot % MOE_BLOCK
    needed = (slot < ne * MOE_BLOCK) & (slot_q < (padded - sizes)[slot_e])
    pad_keys = jnp.where(needed, (slot_e << shift) | (t + slot_q), (ne << shift) + slot)
    order = jnp.sort(jnp.concatenate([real_keys, pad_keys]))
    row_tok = (order & ((1 << shift) - 1)) % t
    blk0 = jnp.arange(n_blocks, dtype=jnp.int32) * MOE_BLOCK
    block_e = jnp.minimum(jnp.sum((pad_end[None, :] <= blk0[:, None]).astype(jnp.int32), axis=1),
                          ne - 1)
    block_valid = (blk0 < pad_end[-1]).astype(jnp.int32)
    xs = h_pk[row_tok]
    ys = experts(xs, block_e, block_valid, layer, w_gate, w_up, w_down)
    yg = ys[dest.reshape(a)].reshape(TOP_K, t, d // 2)
    return shared_ln(h_bf, h, yg, wt8, w_sh_gu, w_sh_down, g, b)


def kernel(x, w_in, b_in, ret_gn_g, conv_w, conv_b, conv_ln_g, conv_ln_b, w_ret_o, w_fox_o,
           w_conv_o, w_out, ln1_g, ln1_b, w_router, b_router, w_exp_gate, w_exp_up, w_exp_down,
           w_sh_gate, w_sh_up, w_sh_down, ln2_g, ln2_b):
    bsz, seq, d = x.shape
    t = bsz * seq
    lay = _mixer_layout(d)
    chunk = min(RET_CHUNK, seq)
    tables = _retention_tables(seq, chunk)
    fq0 = 2 * lay["rq"] + 2 * lay["rv"]

    def row(v):
        return v.reshape(1, -1)

    def layer(carry, p):
        h, h_bf = carry
        w_cat, b_cat, w_ff, b_ff = _prep_in_weights(p["w_in"], p["b_in"], d)
        z2 = in_proj(h_bf, w_cat, b_cat)
        z = z2.reshape(bsz, seq, -1)
        ret_o = retention(z, row(p["ret_gn_g"]), tables)
        negc = fox_cumsum(h_bf.reshape(bsz, seq, d), w_ff, b_ff)
        fox_o = fox_attention(z, negc, fq0 // FOX_DH, (fq0 + lay["fq"]) // FOX_DH,
                              (fq0 + 2 * lay["fq"]) // FOX_DH)
        conv_o = conformer_conv(z, lay["main"] // d, lay["main"] // d + 1, p["conv_w"],
                                row(p["conv_b"]), row(p["conv_ln_g"]), row(p["conv_ln_b"]))
        merged = merge(ret_o.reshape(t, -1), fox_o.reshape(t, -1), conv_o.reshape(t, -1),
                       p["w_ret_o"].astype(BF16), p["w_fox_o"].astype(BF16),
                       p["w_conv_o"].astype(BF16), z2, (lay["main"] + 2 * d) // d)
        h, h_bf, h_pk = out_ln(merged, p["w_out"].astype(BF16), h, row(p["ln1_g"]),
                               row(p["ln1_b"]))
        routing = router(h, p["w_router"], row(p["b_router"]))
        w_sh_gu = jnp.concatenate([p["w_sh_gate"], p["w_sh_up"]], axis=-1).astype(BF16)
        h, h_bf = _moe_sublayer(h, h_bf, h_pk, routing, p["layer"],
                                w_exp_gate, w_exp_up, w_exp_down, w_sh_gu,
                                p["w_sh_down"].astype(BF16), row(p["ln2_g"]), row(p["ln2_b"]))
        return (h, h_bf), None

    depth = w_in.shape[0]
    params = dict(w_in=w_in, b_in=b_in, ret_gn_g=ret_gn_g, conv_w=conv_w, conv_b=conv_b,
                  conv_ln_g=conv_ln_g, conv_ln_b=conv_ln_b, w_ret_o=w_ret_o, w_fox_o=w_fox_o,
                  w_conv_o=w_conv_o, w_out=w_out, ln1_g=ln1_g, ln1_b=ln1_b, w_router=w_router,
                  b_router=b_router, w_sh_gate=w_sh_gate, w_sh_up=w_sh_up,
                  w_sh_down=w_sh_down, ln2_g=ln2_g, ln2_b=ln2_b,
                  layer=jnp.arange(depth, dtype=jnp.int32).reshape(depth, 1))
    h0 = x.reshape(t, d)
    (h, _), _ = lax.scan(layer, (h0, h0.astype(BF16)), params)
    return h.reshape(bsz, seq, d)
```

```python
import jax
import jax.numpy as jnp
from jax import lax
from jax.experimental import pallas as pl
from jax.experimental.pallas import tpu as pltpu

F32 = jnp.float32
BF16 = jnp.bfloat16

RET_H = 8
RET_DK = 128
RET_DV = 256
ROPE_BASE = 10000.0
FOX_H = 16
FOX_DH = 128
CONV_W = 31
N_BRANCH = 3
TOP_K = 6
ROUTED_SCALE = 2.5
DEPTH_FOR_NORM = 4
ALPHA = (2 * DEPTH_FOR_NORM) ** 0.25
LN_EPS = 1e-5

LANES = 128
SUBLANES = 8
VMEM_LIMIT = 56 * 1024 * 1024

IN_TM = 1024
IN_TN = 2048
RET_CHUNK = 256
FOX_TQ = 2048
FOX_TK = 512
FOX_HG = 1
CUM_BLK = 256
CONV_TS = 256
CONV_HALO = 32
CONV_RB = 256
CONV_CB = 128
MOE_BLOCK = 512
NEG_BIG = -1e30
LOG2E = 1.4426950408889634


def _params(*sem):
    return pltpu.CompilerParams(dimension_semantics=sem, vmem_limit_bytes=VMEM_LIMIT)


def _sigmoid(x):
    return 1.0 / (1.0 + jnp.exp(-x))


def _pack_halves(x):
    c = x.shape[1] // 2
    xb = x.astype(BF16).astype(F32)
    lo = lax.bitcast_convert_type(xb[:, :c], jnp.uint32) >> 16
    hi = lax.bitcast_convert_type(xb[:, c:], jnp.uint32) & jnp.uint32(0xFFFF0000)
    return lo | hi


def _unpack_halves(u):
    lo = lax.bitcast_convert_type(u << 16, F32)
    hi = lax.bitcast_convert_type(u & jnp.uint32(0xFFFF0000), F32)
    return lo, hi


def _layernorm_rows(x, g, b):
    mu = jnp.mean(x, axis=-1, keepdims=True)
    xc = x - mu
    var = jnp.mean(xc * xc, axis=-1, keepdims=True)
    return xc * lax.rsqrt(var + LN_EPS) * g + b


def _in_proj_kernel(x_ref, w_ref, b_ref, o_ref):
    acc = jnp.dot(x_ref[...], w_ref[...], preferred_element_type=F32)
    o_ref[...] = acc + b_ref[...]


def in_proj(x_bf, w_bf, b, tm=IN_TM, tn=IN_TN):
    m, k = x_bf.shape
    n = w_bf.shape[1]
    tm = min(tm, m)
    assert m % tm == 0 and n % tn == 0
    return pl.pallas_call(
        _in_proj_kernel,
        grid=(m // tm, n // tn),
        in_specs=[pl.BlockSpec((tm, k), lambda i, j: (i, 0)),
                  pl.BlockSpec((k, tn), lambda i, j: (0, j)),
                  pl.BlockSpec((1, tn), lambda i, j: (0, j))],
        out_specs=pl.BlockSpec((tm, tn), lambda i, j: (i, j)),
        out_shape=jax.ShapeDtypeStruct((m, n), F32),
        compiler_params=_params("parallel", "parallel"),
        name="in_proj",
    )(x_bf, w_bf, b)


def _retention_kernel(cd_ref, q_ref, k_ref, v_ref, g_ref, cos_ref, sin_ref, decay_ref,
                      xi_ref, zeta_ref, gn_ref, o_ref, state_ref):
    @pl.when(pl.program_id(1) == 0)
    def _():
        state_ref[...] = jnp.zeros_like(state_ref)

    cos = cos_ref[...]
    sin = sin_ref[...]
    for h in range(RET_H):
        qs = slice(h * RET_DK, (h + 1) * RET_DK)
        vs = slice(h * RET_DV, (h + 1) * RET_DV)
        q = q_ref[0, :, qs]
        k = k_ref[0, :, qs]
        qr = q * cos + pltpu.roll(q, RET_DK // 2, 1) * sin
        kr = (k * cos + pltpu.roll(k, RET_DK // 2, 1) * sin) * (RET_DK ** -0.5)
        vb = v_ref[0, :, vs].astype(BF16)
        s = lax.dot_general(qr.astype(BF16), kr.astype(BF16), (((1,), (1,)), ((), ())),
                            preferred_element_type=F32)
        inner = (s * decay_ref[h]).astype(BF16)
        o = jnp.dot(inner, vb, preferred_element_type=F32)
        st = state_ref[h]
        o = o + jnp.dot((qr * xi_ref[h]).astype(BF16), st.astype(BF16),
                        preferred_element_type=F32)
        kz = (kr * zeta_ref[h]).astype(BF16)
        state_ref[h] = st * cd_ref[h] + lax.dot_general(
            kz, vb, (((0,), (0,)), ((), ())), preferred_element_type=F32)
        mu = jnp.mean(o, axis=-1, keepdims=True)
        oc = o - mu
        var = jnp.mean(oc * oc, axis=-1, keepdims=True)
        on = oc * lax.rsqrt(var + LN_EPS) * gn_ref[:, vs]
        g = g_ref[0, :, vs]
        o_ref[0, :, vs] = (g * _sigmoid(g) * on).astype(BF16)


def _retention_tables(seq, chunk):
    h = jnp.arange(RET_H, dtype=F32)
    log_g = jnp.log1p(-jnp.exp2(-5.0 - h))
    i = jnp.arange(chunk, dtype=F32)
    diff = i[:, None] - i[None, :]
    decay = jnp.where(diff >= 0, jnp.exp(log_g[:, None, None] * jnp.maximum(diff, 0.0)), 0.0)
    xi = jnp.exp(log_g[:, None] * (i + 1.0))
    zeta = jnp.exp(log_g[:, None] * (chunk - 1.0 - i))
    xi = jnp.broadcast_to(xi[:, :, None], (RET_H, chunk, RET_DK))
    zeta = jnp.broadcast_to(zeta[:, :, None], (RET_H, chunk, RET_DK))
    cd = jnp.exp(log_g * chunk)
    half = RET_DK // 2
    inv = 1.0 / (ROPE_BASE ** (jnp.arange(half, dtype=F32) / half))
    ang = jnp.arange(seq, dtype=F32)[:, None] * inv[None, :]
    cos = jnp.cos(ang)
    sin = jnp.sin(ang)
    cos2 = jnp.concatenate([cos, cos], axis=-1)
    sin2 = jnp.concatenate([-sin, sin], axis=-1)
    return cd, cos2, sin2, decay, xi, zeta


def retention(z, gn_g, tables):
    b, s, _ = z.shape
    cd, cos2, sin2, decay, xi, zeta = tables
    L = decay.shape[1]
    qw = RET_H * RET_DK
    vw = RET_H * RET_DV
    return pl.pallas_call(
        _retention_kernel,
        grid=(b, s // L),
        in_specs=[pl.BlockSpec(memory_space=pltpu.SMEM),
                  pl.BlockSpec((1, L, qw), lambda bi, n: (bi, n, 0)),
                  pl.BlockSpec((1, L, qw), lambda bi, n: (bi, n, 1)),
                  pl.BlockSpec((1, L, vw), lambda bi, n: (bi, n, 1)),
                  pl.BlockSpec((1, L, vw), lambda bi, n: (bi, n, 2)),
                  pl.BlockSpec((L, RET_DK), lambda bi, n: (n, 0)),
                  pl.BlockSpec((L, RET_DK), lambda bi, n: (n, 0)),
                  pl.BlockSpec((RET_H, L, L), lambda bi, n: (0, 0, 0)),
                  pl.BlockSpec((RET_H, L, RET_DK), lambda bi, n: (0, 0, 0)),
                  pl.BlockSpec((RET_H, L, RET_DK), lambda bi, n: (0, 0, 0)),
                  pl.BlockSpec((1, vw), lambda bi, n: (0, 0))],
        out_specs=pl.BlockSpec((1, L, vw), lambda bi, n: (bi, n, 0)),
        out_shape=jax.ShapeDtypeStruct((b, s, vw), BF16),
        scratch_shapes=[pltpu.VMEM((RET_H, RET_DK, RET_DV), F32)],
        compiler_params=_params("parallel", "arbitrary"),
        name="retention",
    )(cd, z, z, z, z, cos2, sin2, decay, xi, zeta, gn_g)


def _fox_cumsum_kernel(h_ref, w_ref, b_ref, tri_ref, o_ref):
    s = h_ref.shape[1]
    tri = tri_ref[...]
    carry = jnp.zeros((1, LANES), F32)
    for i in range(s // CUM_BLK):
        f = jnp.dot(h_ref[0, i * CUM_BLK:(i + 1) * CUM_BLK, :], w_ref[...],
                    preferred_element_type=F32) + b_ref[...]
        ls = jnp.minimum(f, 0.0) - jnp.log1p(jnp.exp(-jnp.abs(f)))
        hi = ls.astype(BF16)
        r1 = ls - hi.astype(F32)
        mid = r1.astype(BF16)
        lo = (r1 - mid.astype(F32)).astype(BF16)
        c = (jnp.dot(tri, hi, preferred_element_type=F32)
             + jnp.dot(tri, mid, preferred_element_type=F32)
             + jnp.dot(tri, lo, preferred_element_type=F32)) + carry
        carry = c[CUM_BLK - 1:CUM_BLK, :]
        o_ref[0, :, i * CUM_BLK:(i + 1) * CUM_BLK] = (-c).T[:FOX_H, :]


def fox_cumsum(h_bf, w_ff, b_ff):
    b, s, d = h_bf.shape
    r = jnp.arange(CUM_BLK)
    tri = (r[:, None] >= r[None, :]).astype(BF16)
    return pl.pallas_call(
        _fox_cumsum_kernel,
        grid=(b,),
        in_specs=[pl.BlockSpec((1, s, d), lambda bi: (bi, 0, 0)),
                  pl.BlockSpec((d, LANES), lambda bi: (0, 0)),
                  pl.BlockSpec((1, LANES), lambda bi: (0, 0)),
                  pl.BlockSpec((CUM_BLK, CUM_BLK), lambda bi: (0, 0))],
        out_specs=pl.BlockSpec((1, FOX_H, s), lambda bi: (bi, 0, 0)),
        out_shape=jax.ShapeDtypeStruct((b, FOX_H, s), F32),
        compiler_params=_params("parallel"),
        name="fox_cumsum",
    )(h_bf, w_ff, b_ff, tri)


def _fox_attn_kernel(q_ref, k_ref, v_ref, nc_ref, o_ref, kt_ref, vx_ref, qb_ref, m_ref, acc_ref):
    qi = pl.program_id(2)
    s_len = k_ref.shape[1]

    def head(g):
        return slice(g * FOX_DH, (g + 1) * FOX_DH)

    @pl.when(qi == 0)
    def _():
        for g in range(FOX_HG):
            for c in range(s_len // FOX_TK):
                rows = slice(c * FOX_TK, (c + 1) * FOX_TK)
                kt_ref[g, :, rows] = k_ref[0, rows, head(g)].T.astype(BF16)
                vx_ref[g, rows, :FOX_DH] = v_ref[0, rows, head(g)].astype(BF16)
            vx_ref[g, :, FOX_DH:] = jnp.ones((s_len, FOX_DH), BF16)

    for g in range(FOX_HG):
        qb_ref[g] = (q_ref[0, :, head(g)] * (FOX_DH ** -0.5 * LOG2E)).astype(BF16)
    m_ref[...] = jnp.full_like(m_ref, NEG_BIG)
    acc_ref[...] = jnp.zeros_like(acc_ref)

    def tile(j, diag):
        k0 = pl.multiple_of(j * FOX_TK, FOX_TK)
        rows = slice(0 if diag is None else diag * FOX_TK, FOX_TQ)
        for g in range(FOX_HG):
            s = jnp.dot(qb_ref[g, rows, :], kt_ref[g, :, pl.ds(k0, FOX_TK)],
                        preferred_element_type=F32)
            s = s + nc_ref[0, g, :, pl.ds(k0, FOX_TK)] * LOG2E
            if diag is not None:
                row = lax.broadcasted_iota(jnp.int32, s.shape, 0)
                col = lax.broadcasted_iota(jnp.int32, s.shape, 1)
                s = jnp.where(col <= row, s, NEG_BIG)
            m_prev = m_ref[g, rows, :]
            m_new = jnp.maximum(m_prev, jnp.max(s, axis=-1, keepdims=True))
            alpha = jnp.exp2(m_prev - m_new)
            p = jnp.concatenate(
                [jnp.exp2(s[:, c * LANES:(c + 1) * LANES] - m_new)
                 for c in range(FOX_TK // LANES)], axis=1).astype(BF16)
            pv = jnp.dot(p, vx_ref[g, pl.ds(k0, FOX_TK), :], preferred_element_type=F32)
            acc_ref[g, rows, :] = jnp.concatenate([alpha, alpha], axis=1) * acc_ref[g, rows, :] + pv
            m_ref[g, rows, :] = m_new

    def body(j, c):
        tile(j, None)
        return c

    per_q = FOX_TQ // FOX_TK
    n_full = qi * per_q
    lax.fori_loop(0, n_full, body, 0)
    for r in range(per_q):
        tile(n_full + r, r)
    for g in range(FOX_HG):
        o_ref[0, :, head(g)] = (acc_ref[g, :, :FOX_DH] / acc_ref[g, :, FOX_DH:]).astype(BF16)


def fox_attention(z, negc, q_blk, k_blk, v_blk):
    b, s, _ = z.shape
    assert FOX_TQ % FOX_TK == 0 and FOX_DH == LANES and FOX_H % FOX_HG == 0
    gw = FOX_HG * FOX_DH
    negc4 = negc.reshape(b, FOX_H, 1, s)
    qg, kg, vg = q_blk // FOX_HG, k_blk // FOX_HG, v_blk // FOX_HG
    return pl.pallas_call(
        _fox_attn_kernel,
        grid=(b, FOX_H // FOX_HG, s // FOX_TQ),
        in_specs=[pl.BlockSpec((1, FOX_TQ, gw), lambda bi, h, qi: (bi, qi, qg + h)),
                  pl.BlockSpec((1, s, gw), lambda bi, h, qi: (bi, 0, kg + h)),
                  pl.BlockSpec((1, s, gw), lambda bi, h, qi: (bi, 0, vg + h)),
                  pl.BlockSpec((1, FOX_HG, 1, s), lambda bi, h, qi: (bi, h, 0, 0))],
        out_specs=pl.BlockSpec((1, FOX_TQ, gw), lambda bi, h, qi: (bi, qi, h)),
        out_shape=jax.ShapeDtypeStruct((b, s, FOX_H * FOX_DH), BF16),
        scratch_shapes=[pltpu.VMEM((FOX_HG, FOX_DH, s), BF16),
                        pltpu.VMEM((FOX_HG, s, 2 * FOX_DH), BF16),
                        pltpu.VMEM((FOX_HG, FOX_TQ, FOX_DH), BF16),
                        pltpu.VMEM((FOX_HG, FOX_TQ, LANES), F32),
                        pltpu.VMEM((FOX_HG, FOX_TQ, 2 * FOX_DH), F32)],
        compiler_params=_params("parallel", "parallel", "arbitrary"),
        name="fox_attn",
    )(z, z, z, negc4)


def _conv_kernel(a_ref, g_ref, ap_ref, gp_ref, w_ref, cb_ref, lg_ref, lb_ref, o_ref, y_ref,
                 c_ref):
    ts = a_ref.shape[1]
    c_all = a_ref.shape[2]
    prev = ap_ref[0] * _sigmoid(gp_ref[0])
    y_ref[0:CONV_HALO, :] = jnp.where(pl.program_id(1) > 0, prev, 0.0)
    y_ref[CONV_HALO:CONV_HALO + ts, :] = a_ref[0] * _sigmoid(g_ref[0])
    lead = CONV_HALO - (CONV_W - 1)

    n_cb = c_all // CONV_CB
    win_rows = CONV_RB + CONV_HALO

    def block(idx, carry):
        r0 = pl.multiple_of((idx // n_cb) * CONV_RB, CONV_RB)
        c0 = pl.multiple_of((idx % n_cb) * CONV_CB, CONV_CB)
        win = y_ref[pl.ds(r0, win_rows), pl.ds(c0, CONV_CB)]
        acc = jnp.zeros((CONV_RB, CONV_CB), F32)
        for ph in range(SUBLANES):
            rot = win if ph == 0 else pltpu.roll(win, win_rows - ph, 0)
            for w in range(CONV_W):
                if (w + lead) % SUBLANES == ph:
                    a0 = (w + lead) - ph
                    acc = acc + w_ref[w:w + 1, pl.ds(c0, CONV_CB)] * rot[a0:a0 + CONV_RB]
        c_ref[pl.ds(r0, CONV_RB), pl.ds(c0, CONV_CB)] = acc + cb_ref[:, pl.ds(c0, CONV_CB)]
        return carry

    lax.fori_loop(0, (ts // CONV_RB) * n_cb, block, 0)
    y = _layernorm_rows(c_ref[...], lg_ref[...], lb_ref[...])
    o_ref[0] = (y * _sigmoid(y)).astype(BF16)


def conformer_conv(z, a_blk, g_blk, conv_w, conv_b, ln_g, ln_b):
    b, s, _ = z.shape
    c = conv_w.shape[1]
    ts = min(CONV_TS, s)
    hb = ts // CONV_HALO

    def halo(col):
        return lambda bi, i: (bi, jnp.maximum(i * hb - 1, 0), col)

    return pl.pallas_call(
        _conv_kernel,
        grid=(b, s // ts),
        in_specs=[pl.BlockSpec((1, ts, c), lambda bi, i: (bi, i, a_blk)),
                  pl.BlockSpec((1, ts, c), lambda bi, i: (bi, i, g_blk)),
                  pl.BlockSpec((1, CONV_HALO, c), halo(a_blk)),
                  pl.BlockSpec((1, CONV_HALO, c), halo(g_blk)),
                  pl.BlockSpec((CONV_W, c), lambda bi, i: (0, 0)),
                  pl.BlockSpec((1, c), lambda bi, i: (0, 0)),
                  pl.BlockSpec((1, c), lambda bi, i: (0, 0)),
                  pl.BlockSpec((1, c), lambda bi, i: (0, 0))],
        out_specs=pl.BlockSpec((1, ts, c), lambda bi, i: (bi, i, 0)),
        out_shape=jax.ShapeDtypeStruct((b, s, c), BF16),
        scratch_shapes=[pltpu.VMEM((CONV_HALO + ts, c), F32),
                        pltpu.VMEM((ts, c), F32)],
        compiler_params=_params("parallel", "parallel"),
        name="conv",
    )(z, z, z, z, conv_w, conv_b, ln_g, ln_b)


def _merge_kernel(a_ref, f_ref, c_ref, wa_ref, wf_ref, wc_ref, g0_ref, g1_ref, g2_ref, o_ref):
    ya = jnp.dot(a_ref[...], wa_ref[...], preferred_element_type=F32)
    m = _sigmoid(g0_ref[...]) * ya
    yf = jnp.dot(f_ref[...], wf_ref[...], preferred_element_type=F32)
    m = m + _sigmoid(g1_ref[...]) * yf
    yc = jnp.dot(c_ref[...], wc_ref[...], preferred_element_type=F32)
    m = m + _sigmoid(g2_ref[...]) * yc
    o_ref[...] = m.astype(BF16)


def merge(ret_o, fox_o, conv_o, w_ret, w_fox, w_conv, z2, gl_blk, tm=1024, tn=256):
    m, k = ret_o.shape
    n = w_ret.shape[1]
    tm = min(tm, m)
    nb = n // tn
    act = pl.BlockSpec((tm, k), lambda i, j: (i, 0))
    wsp = pl.BlockSpec((k, tn), lambda i, j: (0, j))

    def gate(br):
        return pl.BlockSpec((tm, tn), lambda i, j: (i, gl_blk * nb + br * nb + j))

    return pl.pallas_call(
        _merge_kernel,
        grid=(m // tm, nb),
        in_specs=[act, act, act, wsp, wsp, wsp, gate(0), gate(1), gate(2)],
        out_specs=pl.BlockSpec((tm, tn), lambda i, j: (i, j)),
        out_shape=jax.ShapeDtypeStruct((m, n), BF16),
        compiler_params=_params("parallel", "parallel"),
        name="merge",
    )(ret_o, fox_o, conv_o, w_ret, w_fox, w_conv, z2, z2, z2)


def _out_ln_kernel(x_ref, w_ref, h_ref, g_ref, b_ref, o_ref, op_ref):
    y = jnp.dot(x_ref[...], w_ref[...], preferred_element_type=F32)
    out = _layernorm_rows(ALPHA * h_ref[...] + y, g_ref[...], b_ref[...])
    o_ref[...] = out
    op_ref[...] = _pack_halves(out)


def out_ln(merged, w_out, h, g, b, tm=512):
    m, k = merged.shape
    n = w_out.shape[1]
    tm = min(tm, m)
    return pl.pallas_call(
        _out_ln_kernel,
        grid=(m // tm,),
        in_specs=[pl.BlockSpec((tm, k), lambda i: (i, 0)),
                  pl.BlockSpec((k, n), lambda i: (0, 0)),
                  pl.BlockSpec((tm, n), lambda i: (i, 0)),
                  pl.BlockSpec((1, n), lambda i: (0, 0)),
                  pl.BlockSpec((1, n), lambda i: (0, 0))],
        out_specs=[pl.BlockSpec((tm, n), lambda i: (i, 0)),
                   pl.BlockSpec((tm, n // 2), lambda i: (i, 0))],
        out_shape=[jax.ShapeDtypeStruct((m, n), F32),
                   jax.ShapeDtypeStruct((m, n // 2), jnp.uint32)],
        compiler_params=_params("parallel"),
        name="out_ln",
    )(merged, w_out, h, g, b)


ROUTE_COLS = 8


def _router_kernel(h_ref, w_ref, b_ref, tri_ref, idx_ref, wt_ref, rank_ref, cnt_ref, carry_ref):
    @pl.when(pl.program_id(0) == 0)
    def _():
        carry_ref[...] = jnp.zeros_like(carry_ref)

    h = h_ref[...]
    hh = h.astype(BF16)
    hl = (h - hh.astype(F32)).astype(BF16)
    w = w_ref[...]
    wh = w.astype(BF16)
    wl = (w - wh.astype(F32)).astype(BF16)
    logits = (jnp.dot(hh, wh, preferred_element_type=F32)
              + jnp.dot(hl, wh, preferred_element_type=F32)
              + jnp.dot(hh, wl, preferred_element_type=F32))
    scores = _sigmoid(logits)
    tm, ne = scores.shape
    sel = scores + b_ref[...]
    lane = lax.broadcasted_iota(jnp.int32, (tm, ne), 1).astype(F32)
    mask = jnp.zeros((tm, ne), F32)
    onehots, idxs, wts = [], [], []
    for _ in range(TOP_K):
        mx = jnp.max(sel, axis=-1, keepdims=True)
        ik = jnp.min(jnp.where(sel == mx, lane, float(ne)), axis=-1, keepdims=True)
        oh = lane == ik
        wts.append(jnp.sum(jnp.where(oh, scores, 0.0), axis=-1, keepdims=True))
        idxs.append(ik)
        onehots.append(oh)
        sel = jnp.where(oh, -jnp.inf, sel)
        mask = jnp.where(oh, 1.0, mask)
    wsum = wts[0]
    for t in wts[1:]:
        wsum = wsum + t
    cnt = jnp.dot(tri_ref[...], mask.astype(BF16), preferred_element_type=F32) + carry_ref[...]
    col = lax.broadcasted_iota(jnp.int32, (tm, LANES), 1)
    idx_o = jnp.zeros((tm, LANES), F32)
    wt_o = jnp.zeros((tm, LANES), F32)
    rank_o = jnp.zeros((tm, LANES), F32)
    for kk in range(TOP_K):
        rk = jnp.sum(jnp.where(onehots[kk], cnt, 0.0), axis=-1, keepdims=True)
        idx_o = jnp.where(col == kk, idxs[kk], idx_o)
        wt_o = jnp.where(col == kk, wts[kk] / wsum * ROUTED_SCALE, wt_o)
        rank_o = jnp.where(col == kk, rk, rank_o)
    idx_ref[...] = idx_o.T[:ROUTE_COLS].astype(jnp.int32)
    wt_ref[...] = wt_o[:, :ROUTE_COLS]
    rank_ref[...] = rank_o.T[:ROUTE_COLS].astype(jnp.int32)
    carry_ref[...] = carry_ref[...] + jnp.sum(mask, axis=0, keepdims=True)
    cnt_ref[...] = carry_ref[...]


def router(h, w_router, b_router, tm=512):
    m, k = h.shape
    ne = w_router.shape[1]
    tm = min(tm, m)
    r = jnp.arange(tm)
    tri = (r[:, None] > r[None, :]).astype(BF16)
    small = pl.BlockSpec((tm, ROUTE_COLS), lambda i: (i, 0))
    small_t = pl.BlockSpec((ROUTE_COLS, tm), lambda i: (0, i))
    return pl.pallas_call(
        _router_kernel,
        grid=(m // tm,),
        in_specs=[pl.BlockSpec((tm, k), lambda i: (i, 0)),
                  pl.BlockSpec((k, ne), lambda i: (0, 0)),
                  pl.BlockSpec((1, ne), lambda i: (0, 0)),
                  pl.BlockSpec((tm, tm), lambda i: (0, 0))],
        out_specs=[small_t, small, small_t, pl.BlockSpec((1, ne), lambda i: (0, 0))],
        out_shape=[jax.ShapeDtypeStruct((ROUTE_COLS, m), jnp.int32),
                   jax.ShapeDtypeStruct((m, ROUTE_COLS), F32),
                   jax.ShapeDtypeStruct((ROUTE_COLS, m), jnp.int32),
                   jax.ShapeDtypeStruct((1, ne), F32)],
        scratch_shapes=[pltpu.VMEM((1, ne), F32)],
        compiler_params=_params("arbitrary"),
        name="router",
    )(h, w_router, b_router, tri)


def _experts_kernel(be_ref, bv_ref, li_ref, x_ref, wg_ref, wu_ref, wd_ref, o_ref, wgb_ref,
                    wub_ref, wdb_ref):
    i = pl.program_id(0)

    @pl.when(jnp.logical_or(i == 0, be_ref[i] != be_ref[jnp.maximum(i - 1, 0)]))
    def _():
        wgb_ref[...] = wg_ref[0, 0].astype(BF16)
        wub_ref[...] = wu_ref[0, 0].astype(BF16)
        wdb_ref[...] = wd_ref[0, 0].astype(BF16)

    @pl.when(bv_ref[i] > 0)
    def _():
        half = x_ref.shape[1]
        xa, xb = _unpack_halves(x_ref[...])
        xa = xa.astype(BF16)
        xb = xb.astype(BF16)
        g = (jnp.dot(xa, wgb_ref[:half, :], preferred_element_type=F32)
             + jnp.dot(xb, wgb_ref[half:, :], preferred_element_type=F32))
        u = (jnp.dot(xa, wub_ref[:half, :], preferred_element_type=F32)
             + jnp.dot(xb, wub_ref[half:, :], preferred_element_type=F32))
        a = (g * _sigmoid(g) * u).astype(BF16)
        o_ref[...] = _pack_halves(jnp.dot(a, wdb_ref[...], preferred_element_type=F32))

    @pl.when(bv_ref[i] == 0)
    def _():
        o_ref[...] = jnp.zeros_like(o_ref)


def experts(xs, block_e, block_valid, layer, w_gate, w_up, w_down):
    r, half = xs.shape
    d = 2 * half
    ff = w_gate.shape[3]
    nblk = r // MOE_BLOCK
    return pl.pallas_call(
        _experts_kernel,
        grid_spec=pltpu.PrefetchScalarGridSpec(
            num_scalar_prefetch=3,
            grid=(nblk,),
            in_specs=[pl.BlockSpec((MOE_BLOCK, half), lambda i, be, bv, li: (i, 0)),
                      pl.BlockSpec((1, 1, d, ff), lambda i, be, bv, li: (li[0], be[i], 0, 0)),
                      pl.BlockSpec((1, 1, d, ff), lambda i, be, bv, li: (li[0], be[i], 0, 0)),
                      pl.BlockSpec((1, 1, ff, d), lambda i, be, bv, li: (li[0], be[i], 0, 0))],
            out_specs=pl.BlockSpec((MOE_BLOCK, half), lambda i, be, bv, li: (i, 0)),
            scratch_shapes=[pltpu.VMEM((d, ff), BF16), pltpu.VMEM((d, ff), BF16),
                            pltpu.VMEM((ff, d), BF16)]),
        out_shape=jax.ShapeDtypeStruct((r, half), jnp.uint32),
        compiler_params=_params("arbitrary"),
        name="experts",
    )(block_e, block_valid, layer, xs, w_gate, w_up, w_down)


def _shared_ln_kernel(h_ref, yg_ref, wt_ref, wgu_ref, wd_ref, g_ref, b_ref, o_ref, ob_ref):
    ff = wd_ref.shape[0]
    gu = jnp.dot(h_ref[...].astype(BF16), wgu_ref[...], preferred_element_type=F32)
    g = gu[:, :ff]
    a = (g * _sigmoid(g) * gu[:, ff:]).astype(BF16)
    shared = jnp.dot(a, wd_ref[...], preferred_element_type=F32)
    wt = wt_ref[...]
    r_lo = r_hi = None
    for kk in range(TOP_K):
        lo, hi = _unpack_halves(yg_ref[kk])
        wk = wt[:, kk:kk + 1]
        r_lo = wk * lo if r_lo is None else r_lo + wk * lo
        r_hi = wk * hi if r_hi is None else r_hi + wk * hi
    routed = jnp.concatenate([r_lo, r_hi], axis=1)
    out = _layernorm_rows(ALPHA * h_ref[...] + (routed + shared), g_ref[...], b_ref[...])
    o_ref[...] = out
    ob_ref[...] = out.astype(BF16)


def shared_ln(h, yg, wt, w_gu, w_down, g, b, tm=256):
    m, d = h.shape
    ff2 = w_gu.shape[1]
    tm = min(tm, m)
    row = pl.BlockSpec((tm, d), lambda i: (i, 0))
    vec = pl.BlockSpec((1, d), lambda i: (0, 0))
    return pl.pallas_call(
        _shared_ln_kernel,
        grid=(m // tm,),
        in_specs=[row,
                  pl.BlockSpec((TOP_K, tm, d // 2), lambda i: (0, i, 0)),
                  pl.BlockSpec((tm, ROUTE_COLS), lambda i: (i, 0)),
                  pl.BlockSpec((d, ff2), lambda i: (0, 0)),
                  pl.BlockSpec((ff2 // 2, d), lambda i: (0, 0)),
                  vec, vec],
        out_specs=[row, row],
        out_shape=[jax.ShapeDtypeStruct((m, d), F32), jax.ShapeDtypeStruct((m, d), BF16)],
        compiler_params=_params("parallel"),
        name="shared_ln",
    )(h, yg, wt, w_gu, w_down, g, b)


def _mixer_layout(d):
    rq = RET_H * RET_DK
    rv = RET_H * RET_DV
    fq = FOX_H * FOX_DH
    main = 2 * rq + 2 * rv + 3 * fq
    ff0 = main
    cu0 = ff0 + FOX_H
    gl0 = cu0 + 2 * d
    end = gl0 + N_BRANCH * d
    return dict(rq=rq, rv=rv, fq=fq, main=main, ff0=ff0, cu0=cu0, gl0=gl0, end=end)


def _prep_in_weights(w_in, b_in, d):
    lay = _mixer_layout(d)
    pad = LANES - FOX_H
    w = jnp.concatenate([w_in[:, :lay["main"]], w_in[:, lay["cu0"]:lay["end"]]],
                        axis=1).astype(BF16)
    bb = jnp.concatenate([b_in[:lay["main"]], b_in[lay["cu0"]:lay["end"]]])
    w_ff = jnp.pad(w_in[:, lay["ff0"]:lay["cu0"]], ((0, 0), (0, pad))).astype(BF16)
    b_ff = jnp.pad(b_in[lay["ff0"]:lay["cu0"]], (0, pad))
    return w, bb.reshape(1, -1).astype(F32), w_ff, b_ff.reshape(1, -1).astype(F32)


def _moe_sublayer(h, h_pk, routing, layer, w_gate, w_up, w_down, w_sh_gu, w_sh_down, g, b):
    t, d = h.shape
    idx8, wt8, rank8, cnt = routing
    ne = cnt.shape[1]
    idx = idx8[:TOP_K]
    rank = rank8[:TOP_K]
    sizes = cnt[0].astype(jnp.int32)
    a = t * TOP_K
    n_blocks = -(-a // MOE_BLOCK) + ne
    r = n_blocks * MOE_BLOCK
    padded = (sizes + MOE_BLOCK - 1) // MOE_BLOCK * MOE_BLOCK
    pad_end = jnp.cumsum(padded)
    pad_start = pad_end - padded
    start_of = jnp.sum(jnp.where(idx[..., None] == jnp.arange(ne, dtype=jnp.int32), pad_start, 0),
                       axis=-1)
    dest = lax.optimization_barrier(start_of + rank)
    shift = (t + MOE_BLOCK).bit_length()
    tok = jnp.arange(t, dtype=jnp.int32)[None, :]
    real_keys = ((idx << shift) | tok).reshape(a)
    slot = jnp.arange(r - a, dtype=jnp.int32)
    slot_e = jnp.minimum(slot // MOE_BLOCK, ne - 1)
    slot_q = slot % MOE_BLOCK
    needed = (slot < ne * MOE_BLOCK) & (slot_q < (padded - sizes)[slot_e])
    pad_keys = jnp.where(needed, (slot_e << shift) | (t + slot_q), (ne << shift) + slot)
    order = jnp.sort(jnp.concatenate([real_keys, pad_keys]))
    row_tok = (order & ((1 << shift) - 1)) % t
    blk0 = jnp.arange(n_blocks, dtype=jnp.int32) * MOE_BLOCK
    block_e = jnp.minimum(jnp.sum((pad_end[None, :] <= blk0[:, None]).astype(jnp.int32), axis=1),
                          ne - 1)
    block_valid = (blk0 < pad_end[-1]).astype(jnp.int32)
    xs = h_pk[row_tok]
    ys = experts(xs, block_e, block_valid, layer, w_gate, w_up, w_down)
    yg = ys[dest.reshape(a)].reshape(TOP_K, t, d // 2)
    return shared_ln(h, yg, wt8, w_sh_gu, w_sh_down, g, b)


def kernel(x, w_in, b_in, ret_gn_g, conv_w, conv_b, conv_ln_g, conv_ln_b, w_ret_o, w_fox_o,
           w_conv_o, w_out, ln1_g, ln1_b, w_router, b_router, w_exp_gate, w_exp_up, w_exp_down,
           w_sh_gate, w_sh_up, w_sh_down, ln2_g, ln2_b):
    bsz, seq, d = x.shape
    t = bsz * seq
    lay = _mixer_layout(d)
    chunk = min(RET_CHUNK, seq)
    tables = _retention_tables(seq, chunk)
    fq0 = 2 * lay["rq"] + 2 * lay["rv"]

    def row(v):
        return v.reshape(1, -1)

    def layer(carry, p):
        h, h_bf = carry
        w_cat, b_cat, w_ff, b_ff = _prep_in_weights(p["w_in"], p["b_in"], d)
        z2 = in_proj(h_bf, w_cat, b_cat)
        z = z2.reshape(bsz, seq, -1)
        ret_o = retention(z, row(p["ret_gn_g"]), tables)
        negc = fox_cumsum(h_bf.reshape(bsz, seq, d), w_ff, b_ff)
        fox_o = fox_attention(z, negc, fq0 // FOX_DH, (fq0 + lay["fq"]) // FOX_DH,
                              (fq0 + 2 * lay["fq"]) // FOX_DH)
        conv_o = conformer_conv(z, lay["main"] // d, lay["main"] // d + 1, p["conv_w"],
                                row(p["conv_b"]), row(p["conv_ln_g"]), row(p["conv_ln_b"]))
        merged = merge(ret_o.reshape(t, -1), fox_o.reshape(t, -1), conv_o.reshape(t, -1),
                       p["w_ret_o"].astype(BF16), p["w_fox_o"].astype(BF16),
                       p["w_conv_o"].astype(BF16), z2, (lay["main"] + 2 * d) // d)
        h, h_pk = out_ln(merged, p["w_out"].astype(BF16), h, row(p["ln1_g"]), row(p["ln1_b"]))
        routing = router(h, p["w_router"], row(p["b_router"]))
        w_sh_gu = jnp.concatenate([p["w_sh_gate"], p["w_sh_up"]], axis=-1).astype(BF16)
        h, h_bf = _moe_sublayer(h, h_pk, routing, p["layer"],
                                w_exp_gate, w_exp_up, w_exp_down, w_sh_gu,
                                p["w_sh_down"].astype(BF16), row(p["ln2_g"]), row(p["ln2_b"]))
        return (h, h_bf), None

    depth = w_in.shape[0]
    params = dict(w_in=w_in, b_in=b_in, ret_gn_g=ret_gn_g, conv_w=conv_w, conv_b=conv_b,
                  conv_ln_g=conv_ln_g, conv_ln_b=conv_ln_b, w_ret_o=w_ret_o, w_fox_o=w_fox_o,
                  w_conv_o=w_conv_o, w_out=w_out, ln1_g=ln1_g, ln1_b=ln1_b, w_router=w_router,
                  b_router=b_router, w_sh_gate=w_sh_gate, w_sh_up=w_sh_up,
                  w_sh_down=w_sh_down, ln2_g=ln2_g, ln2_b=ln2_b,
                  layer=jnp.arange(depth, dtype=jnp.int32).reshape(depth, 1))
    h0 = x.reshape(t, d)
    (h, _), _ = lax.scan(layer, (h0, h0.astype(BF16)), params)
    return h.reshape(bsz, seq, d)
```

```python
import jax
import jax.numpy as jnp
from jax import lax
from jax.experimental import pallas as pl
from jax.experimental.pallas import tpu as pltpu

F32 = jnp.float32
BF16 = jnp.bfloat16

RET_H = 8
RET_DK = 128
RET_DV = 256
ROPE_BASE = 10000.0
FOX_H = 16
FOX_DH = 128
CONV_W = 31
N_BRANCH = 3
TOP_K = 6
ROUTED_SCALE = 2.5
DEPTH_FOR_NORM = 4
ALPHA = (2 * DEPTH_FOR_NORM) ** 0.25
LN_EPS = 1e-5

LANES = 128
SUBLANES = 8
VMEM_LIMIT = 56 * 1024 * 1024

IN_TM = 1024
IN_TN = 2048
RET_CHUNK = 256
FOX_TQ = 2048
FOX_TK = 512
FOX_HG = 1
CUM_BLK = 256
CONV_TS = 256
CONV_HALO = 32
CONV_RB = 256
CONV_CB = 128
MOE_BLOCK = 512
NEG_BIG = -1e30
LOG2E = 1.4426950408889634


def _params(*sem):
    return pltpu.CompilerParams(dimension_semantics=sem, vmem_limit_bytes=VMEM_LIMIT)


def _sigmoid(x):
    return 1.0 / (1.0 + jnp.exp(-x))


def _pack_halves(x):
    c = x.shape[1] // 2
    xb = x.astype(BF16).astype(F32)
    lo = lax.bitcast_convert_type(xb[:, :c], jnp.uint32) >> 16
    hi = lax.bitcast_convert_type(xb[:, c:], jnp.uint32) & jnp.uint32(0xFFFF0000)
    return lo | hi


def _unpack_halves(u):
    lo = lax.bitcast_convert_type(u << 16, F32)
    hi = lax.bitcast_convert_type(u & jnp.uint32(0xFFFF0000), F32)
    return lo, hi


def _layernorm_rows(x, g, b):
    mu = jnp.mean(x, axis=-1, keepdims=True)
    xc = x - mu
    var = jnp.mean(xc * xc, axis=-1, keepdims=True)
    return xc * lax.rsqrt(var + LN_EPS) * g + b


def _in_proj_kernel(x_ref, w_ref, b_ref, o_ref):
    acc = jnp.dot(x_ref[...], w_ref[...], preferred_element_type=F32)
    o_ref[...] = acc + b_ref[...]


def in_proj(x_bf, w_bf, b, tm=IN_TM, tn=IN_TN):
    m, k = x_bf.shape
    n = w_bf.shape[1]
    tm = min(tm, m)
    assert m % tm == 0 and n % tn == 0
    return pl.pallas_call(
        _in_proj_kernel,
        grid=(m // tm, n // tn),
        in_specs=[pl.BlockSpec((tm, k), lambda i, j: (i, 0)),
                  pl.BlockSpec((k, tn), lambda i, j: (0, j)),
                  pl.BlockSpec((1, tn), lambda i, j: (0, j))],
        out_specs=pl.BlockSpec((tm, tn), lambda i, j: (i, j)),
        out_shape=jax.ShapeDtypeStruct((m, n), F32),
        compiler_params=_params("parallel", "parallel"),
        name="in_proj",
    )(x_bf, w_bf, b)


def _retention_kernel(cd_ref, q_ref, k_ref, v_ref, g_ref, cos_ref, sin_ref, decay_ref,
                      xi_ref, zeta_ref, gn_ref, o_ref, state_ref):
    @pl.when(pl.program_id(1) == 0)
    def _():
        state_ref[...] = jnp.zeros_like(state_ref)

    cos = cos_ref[...]
    sin = sin_ref[...]
    for h in range(RET_H):
        qs = slice(h * RET_DK, (h + 1) * RET_DK)
        vs = slice(h * RET_DV, (h + 1) * RET_DV)
        q = q_ref[0, :, qs]
        k = k_ref[0, :, qs]
        qr = q * cos + pltpu.roll(q, RET_DK // 2, 1) * sin
        kr = (k * cos + pltpu.roll(k, RET_DK // 2, 1) * sin) * (RET_DK ** -0.5)
        vb = v_ref[0, :, vs].astype(BF16)
        s = lax.dot_general(qr.astype(BF16), kr.astype(BF16), (((1,), (1,)), ((), ())),
                            preferred_element_type=F32)
        inner = (s * decay_ref[h]).astype(BF16)
        o = jnp.dot(inner, vb, preferred_element_type=F32)
        st = state_ref[h]
        o = o + jnp.dot((qr * xi_ref[h]).astype(BF16), st.astype(BF16),
                        preferred_element_type=F32)
        kz = (kr * zeta_ref[h]).astype(BF16)
        state_ref[h] = st * cd_ref[h] + lax.dot_general(
            kz, vb, (((0,), (0,)), ((), ())), preferred_element_type=F32)
        mu = jnp.mean(o, axis=-1, keepdims=True)
        oc = o - mu
        var = jnp.mean(oc * oc, axis=-1, keepdims=True)
        on = oc * lax.rsqrt(var + LN_EPS) * gn_ref[:, vs]
        g = g_ref[0, :, vs]
        o_ref[0, :, vs] = (g * _sigmoid(g) * on).astype(BF16)


def _retention_tables(seq, chunk):
    h = jnp.arange(RET_H, dtype=F32)
    log_g = jnp.log1p(-jnp.exp2(-5.0 - h))
    i = jnp.arange(chunk, dtype=F32)
    diff = i[:, None] - i[None, :]
    decay = jnp.where(diff >= 0, jnp.exp(log_g[:, None, None] * jnp.maximum(diff, 0.0)), 0.0)
    xi = jnp.exp(log_g[:, None] * (i + 1.0))
    zeta = jnp.exp(log_g[:, None] * (chunk - 1.0 - i))
    xi = jnp.broadcast_to(xi[:, :, None], (RET_H, chunk, RET_DK))
    zeta = jnp.broadcast_to(zeta[:, :, None], (RET_H, chunk, RET_DK))
    cd = jnp.exp(log_g * chunk)
    half = RET_DK // 2
    inv = 1.0 / (ROPE_BASE ** (jnp.arange(half, dtype=F32) / half))
    ang = jnp.arange(seq, dtype=F32)[:, None] * inv[None, :]
    cos = jnp.cos(ang)
    sin = jnp.sin(ang)
    cos2 = jnp.concatenate([cos, cos], axis=-1)
    sin2 = jnp.concatenate([-sin, sin], axis=-1)
    return cd, cos2, sin2, decay, xi, zeta


def retention(z, gn_g, tables):
    b, s, _ = z.shape
    cd, cos2, sin2, decay, xi, zeta = tables
    L = decay.shape[1]
    qw = RET_H * RET_DK
    vw = RET_H * RET_DV
    return pl.pallas_call(
        _retention_kernel,
        grid=(b, s // L),
        in_specs=[pl.BlockSpec(memory_space=pltpu.SMEM),
                  pl.BlockSpec((1, L, qw), lambda bi, n: (bi, n, 0)),
                  pl.BlockSpec((1, L, qw), lambda bi, n: (bi, n, 1)),
                  pl.BlockSpec((1, L, vw), lambda bi, n: (bi, n, 1)),
                  pl.BlockSpec((1, L, vw), lambda bi, n: (bi, n, 2)),
                  pl.BlockSpec((L, RET_DK), lambda bi, n: (n, 0)),
                  pl.BlockSpec((L, RET_DK), lambda bi, n: (n, 0)),
                  pl.BlockSpec((RET_H, L, L), lambda bi, n: (0, 0, 0)),
                  pl.BlockSpec((RET_H, L, RET_DK), lambda bi, n: (0, 0, 0)),
                  pl.BlockSpec((RET_H, L, RET_DK), lambda bi, n: (0, 0, 0)),
                  pl.BlockSpec((1, vw), lambda bi, n: (0, 0))],
        out_specs=pl.BlockSpec((1, L, vw), lambda bi, n: (bi, n, 0)),
        out_shape=jax.ShapeDtypeStruct((b, s, vw), BF16),
        scratch_shapes=[pltpu.VMEM((RET_H, RET_DK, RET_DV), F32)],
        compiler_params=_params("parallel", "arbitrary"),
        name="retention",
    )(cd, z, z, z, z, cos2, sin2, decay, xi, zeta, gn_g)


def _fox_cumsum_kernel(h_ref, w_ref, b_ref, tri_ref, o_ref):
    s = h_ref.shape[1]
    tri = tri_ref[...]
    carry = jnp.zeros((1, LANES), F32)
    for i in range(s // CUM_BLK):
        f = jnp.dot(h_ref[0, i * CUM_BLK:(i + 1) * CUM_BLK, :], w_ref[...],
                    preferred_element_type=F32) + b_ref[...]
        ls = jnp.minimum(f, 0.0) - jnp.log1p(jnp.exp(-jnp.abs(f)))
        hi = ls.astype(BF16)
        r1 = ls - hi.astype(F32)
        mid = r1.astype(BF16)
        lo = (r1 - mid.astype(F32)).astype(BF16)
        c = (jnp.dot(tri, hi, preferred_element_type=F32)
             + jnp.dot(tri, mid, preferred_element_type=F32)
             + jnp.dot(tri, lo, preferred_element_type=F32)) + carry
        carry = c[CUM_BLK - 1:CUM_BLK, :]
        o_ref[0, :, i * CUM_BLK:(i + 1) * CUM_BLK] = (-c).T[:FOX_H, :]


def fox_cumsum(h_bf, w_ff, b_ff):
    b, s, d = h_bf.shape
    r = jnp.arange(CUM_BLK)
    tri = (r[:, None] >= r[None, :]).astype(BF16)
    return pl.pallas_call(
        _fox_cumsum_kernel,
        grid=(b,),
        in_specs=[pl.BlockSpec((1, s, d), lambda bi: (bi, 0, 0)),
                  pl.BlockSpec((d, LANES), lambda bi: (0, 0)),
                  pl.BlockSpec((1, LANES), lambda bi: (0, 0)),
                  pl.BlockSpec((CUM_BLK, CUM_BLK), lambda bi: (0, 0))],
        out_specs=pl.BlockSpec((1, FOX_H, s), lambda bi: (bi, 0, 0)),
        out_shape=jax.ShapeDtypeStruct((b, FOX_H, s), F32),
        compiler_params=_params("parallel"),
        name="fox_cumsum",
    )(h_bf, w_ff, b_ff, tri)


def _fox_attn_kernel(q_ref, k_ref, v_ref, nc_ref, o_ref, kt_ref, vx_ref, qb_ref, m_ref, acc_ref):
    qi = pl.program_id(2)
    s_len = k_ref.shape[1]

    def head(g):
        return slice(g * FOX_DH, (g + 1) * FOX_DH)

    @pl.when(qi == 0)
    def _():
        for g in range(FOX_HG):
            for c in range(s_len // FOX_TK):
                rows = slice(c * FOX_TK, (c + 1) * FOX_TK)
                kt_ref[g, :, rows] = k_ref[0, rows, head(g)].T.astype(BF16)
                vx_ref[g, rows, :FOX_DH] = v_ref[0, rows, head(g)].astype(BF16)
            vx_ref[g, :, FOX_DH:] = jnp.ones((s_len, FOX_DH), BF16)

    for g in range(FOX_HG):
        qb_ref[g] = (q_ref[0, :, head(g)] * (FOX_DH ** -0.5 * LOG2E)).astype(BF16)
    m_ref[...] = jnp.full_like(m_ref, NEG_BIG)
    acc_ref[...] = jnp.zeros_like(acc_ref)

    def tile(j, diag):
        k0 = pl.multiple_of(j * FOX_TK, FOX_TK)
        rows = slice(0 if diag is None else diag * FOX_TK, FOX_TQ)
        for g in range(FOX_HG):
            s = jnp.dot(qb_ref[g, rows, :], kt_ref[g, :, pl.ds(k0, FOX_TK)],
                        preferred_element_type=F32)
            s = s + nc_ref[0, g, :, pl.ds(k0, FOX_TK)] * LOG2E
            if diag is not None:
                row = lax.broadcasted_iota(jnp.int32, s.shape, 0)
                col = lax.broadcasted_iota(jnp.int32, s.shape, 1)
                s = jnp.where(col <= row, s, NEG_BIG)
            m_prev = m_ref[g, rows, :]
            m_new = jnp.maximum(m_prev, jnp.max(s, axis=-1, keepdims=True))
            alpha = jnp.exp2(m_prev - m_new)
            p = jnp.concatenate(
                [jnp.exp2(s[:, c * LANES:(c + 1) * LANES] - m_new)
                 for c in range(FOX_TK // LANES)], axis=1).astype(BF16)
            pv = jnp.dot(p, vx_ref[g, pl.ds(k0, FOX_TK), :], preferred_element_type=F32)
            acc_ref[g, rows, :] = jnp.concatenate([alpha, alpha], axis=1) * acc_ref[g, rows, :] + pv
            m_ref[g, rows, :] = m_new

    def body(j, c):
        tile(j, None)
        return c

    per_q = FOX_TQ // FOX_TK
    n_full = qi * per_q
    lax.fori_loop(0, n_full, body, 0)
    for r in range(per_q):
        tile(n_full + r, r)
    for g in range(FOX_HG):
        o_ref[0, :, head(g)] = (acc_ref[g, :, :FOX_DH] / acc_ref[g, :, FOX_DH:]).astype(BF16)


def fox_attention(z, negc, q_blk, k_blk, v_blk):
    b, s, _ = z.shape
    assert FOX_TQ % FOX_TK == 0 and FOX_DH == LANES and FOX_H % FOX_HG == 0
    gw = FOX_HG * FOX_DH
    negc4 = negc.reshape(b, FOX_H, 1, s)
    qg, kg, vg = q_blk // FOX_HG, k_blk // FOX_HG, v_blk // FOX_HG
    return pl.pallas_call(
        _fox_attn_kernel,
        grid=(b, FOX_H // FOX_HG, s // FOX_TQ),
        in_specs=[pl.BlockSpec((1, FOX_TQ, gw), lambda bi, h, qi: (bi, qi, qg + h)),
                  pl.BlockSpec((1, s, gw), lambda bi, h, qi: (bi, 0, kg + h)),
                  pl.BlockSpec((1, s, gw), lambda bi, h, qi: (bi, 0, vg + h)),
                  pl.BlockSpec((1, FOX_HG, 1, s), lambda bi, h, qi: (bi, h, 0, 0))],
        out_specs=pl.BlockSpec((1, FOX_TQ, gw), lambda bi, h, qi: (bi, qi, h)),
        out_shape=jax.ShapeDtypeStruct((b, s, FOX_H * FOX_DH), BF16),
        scratch_shapes=[pltpu.VMEM((FOX_HG, FOX_DH, s), BF16),
                        pltpu.VMEM((FOX_HG, s, 2 * FOX_DH), BF16),
                        pltpu.VMEM((FOX_HG, FOX_TQ, FOX_DH), BF16),
                        pltpu.VMEM((FOX_HG, FOX_TQ, LANES), F32),
                        pltpu.VMEM((FOX_HG, FOX_TQ, 2 * FOX_DH), F32)],
        compiler_params=_params("parallel", "parallel", "arbitrary"),
        name="fox_attn",
    )(z, z, z, negc4)


def _conv_kernel(a_ref, g_ref, ap_ref, gp_ref, w_ref, cb_ref, lg_ref, lb_ref, o_ref, y_ref,
                 c_ref):
    ts = a_ref.shape[1]
    c_all = a_ref.shape[2]
    prev = ap_ref[0] * _sigmoid(gp_ref[0])
    y_ref[0:CONV_HALO, :] = jnp.where(pl.program_id(1) > 0, prev, 0.0)
    y_ref[CONV_HALO:CONV_HALO + ts, :] = a_ref[0] * _sigmoid(g_ref[0])
    lead = CONV_HALO - (CONV_W - 1)

    n_cb = c_all // CONV_CB
    win_rows = CONV_RB + CONV_HALO

    def block(idx, carry):
        r0 = pl.multiple_of((idx // n_cb) * CONV_RB, CONV_RB)
        c0 = pl.multiple_of((idx % n_cb) * CONV_CB, CONV_CB)
        win = y_ref[pl.ds(r0, win_rows), pl.ds(c0, CONV_CB)]
        acc = jnp.zeros((CONV_RB, CONV_CB), F32)
        for ph in range(SUBLANES):
            rot = win if ph == 0 else pltpu.roll(win, win_rows - ph, 0)
            for w in range(CONV_W):
                if (w + lead) % SUBLANES == ph:
                    a0 = (w + lead) - ph
                    acc = acc + w_ref[w:w + 1, pl.ds(c0, CONV_CB)] * rot[a0:a0 + CONV_RB]
        c_ref[pl.ds(r0, CONV_RB), pl.ds(c0, CONV_CB)] = acc + cb_ref[:, pl.ds(c0, CONV_CB)]
        return carry

    lax.fori_loop(0, (ts // CONV_RB) * n_cb, block, 0)
    y = _layernorm_rows(c_ref[...], lg_ref[...], lb_ref[...])
    o_ref[0] = (y * _sigmoid(y)).astype(BF16)


def conformer_conv(z, a_blk, g_blk, conv_w, conv_b, ln_g, ln_b):
    b, s, _ = z.shape
    c = conv_w.shape[1]
    ts = min(CONV_TS, s)
    hb = ts // CONV_HALO

    def halo(col):
        return lambda bi, i: (bi, jnp.maximum(i * hb - 1, 0), col)

    return pl.pallas_call(
        _conv_kernel,
        grid=(b, s // ts),
        in_specs=[pl.BlockSpec((1, ts, c), lambda bi, i: (bi, i, a_blk)),
                  pl.BlockSpec((1, ts, c), lambda bi, i: (bi, i, g_blk)),
                  pl.BlockSpec((1, CONV_HALO, c), halo(a_blk)),
                  pl.BlockSpec((1, CONV_HALO, c), halo(g_blk)),
                  pl.BlockSpec((CONV_W, c), lambda bi, i: (0, 0)),
                  pl.BlockSpec((1, c), lambda bi, i: (0, 0)),
                  pl.BlockSpec((1, c), lambda bi, i: (0, 0)),
                  pl.BlockSpec((1, c), lambda bi, i: (0, 0))],
        out_specs=pl.BlockSpec((1, ts, c), lambda bi, i: (bi, i, 0)),
        out_shape=jax.ShapeDtypeStruct((b, s, c), BF16),
        scratch_shapes=[pltpu.VMEM((CONV_HALO + ts, c), F32),
                        pltpu.VMEM((ts, c), F32)],
        compiler_params=_params("parallel", "parallel"),
        name="conv",
    )(z, z, z, z, conv_w, conv_b, ln_g, ln_b)


def _merge_kernel(a_ref, f_ref, c_ref, wa_ref, wf_ref, wc_ref, g0_ref, g1_ref, g2_ref, o_ref):
    ya = jnp.dot(a_ref[...], wa_ref[...], preferred_element_type=F32)
    m = _sigmoid(g0_ref[...]) * ya
    yf = jnp.dot(f_ref[...], wf_ref[...], preferred_element_type=F32)
    m = m + _sigmoid(g1_ref[...]) * yf
    yc = jnp.dot(c_ref[...], wc_ref[...], preferred_element_type=F32)
    m = m + _sigmoid(g2_ref[...]) * yc
    o_ref[...] = m.astype(BF16)


def merge(ret_o, fox_o, conv_o, w_ret, w_fox, w_conv, z2, gl_blk, tm=1024, tn=256):
    m, k = ret_o.shape
    n = w_ret.shape[1]
    tm = min(tm, m)
    nb = n // tn
    act = pl.BlockSpec((tm, k), lambda i, j: (i, 0))
    wsp = pl.BlockSpec((k, tn), lambda i, j: (0, j))

    def gate(br):
        return pl.BlockSpec((tm, tn), lambda i, j: (i, gl_blk * nb + br * nb + j))

    return pl.pallas_call(
        _merge_kernel,
        grid=(m // tm, nb),
        in_specs=[act, act, act, wsp, wsp, wsp, gate(0), gate(1), gate(2)],
        out_specs=pl.BlockSpec((tm, tn), lambda i, j: (i, j)),
        out_shape=jax.ShapeDtypeStruct((m, n), BF16),
        compiler_params=_params("parallel", "parallel"),
        name="merge",
    )(ret_o, fox_o, conv_o, w_ret, w_fox, w_conv, z2, z2, z2)


def _out_ln_kernel(x_ref, w_ref, h_ref, g_ref, b_ref, o_ref, op_ref):
    y = jnp.dot(x_ref[...], w_ref[...], preferred_element_type=F32)
    out = _layernorm_rows(ALPHA * h_ref[...] + y, g_ref[...], b_ref[...])
    o_ref[...] = out
    op_ref[...] = _pack_halves(out)


def out_ln(merged, w_out, h, g, b, tm=512):
    m, k = merged.shape
    n = w_out.shape[1]
    tm = min(tm, m)
    return pl.pallas_call(
        _out_ln_kernel,
        grid=(m // tm,),
        in_specs=[pl.BlockSpec((tm, k), lambda i: (i, 0)),
                  pl.BlockSpec((k, n), lambda i: (0, 0)),
                  pl.BlockSpec((tm, n), lambda i: (i, 0)),
                  pl.BlockSpec((1, n), lambda i: (0, 0)),
                  pl.BlockSpec((1, n), lambda i: (0, 0))],
        out_specs=[pl.BlockSpec((tm, n), lambda i: (i, 0)),
                   pl.BlockSpec((tm, n // 2), lambda i: (i, 0))],
        out_shape=[jax.ShapeDtypeStruct((m, n), F32),
                   jax.ShapeDtypeStruct((m, n // 2), jnp.uint32)],
        compiler_params=_params("parallel"),
        name="out_ln",
    )(merged, w_out, h, g, b)


ROUTE_COLS = 8


def _router_kernel(h_ref, w_ref, b_ref, tri_ref, idx_ref, wt_ref, rank_ref, cnt_ref, carry_ref):
    @pl.when(pl.program_id(0) == 0)
    def _():
        carry_ref[...] = jnp.zeros_like(carry_ref)

    h = h_ref[...]
    hh = h.astype(BF16)
    hl = (h - hh.astype(F32)).astype(BF16)
    w = w_ref[...]
    wh = w.astype(BF16)
    wl = (w - wh.astype(F32)).astype(BF16)
    logits = (jnp.dot(hh, wh, preferred_element_type=F32)
              + jnp.dot(hl, wh, preferred_element_type=F32)
              + jnp.dot(hh, wl, preferred_element_type=F32))
    scores = _sigmoid(logits)
    tm, ne = scores.shape
    sel = scores + b_ref[...]
    lane = lax.broadcasted_iota(jnp.int32, (tm, ne), 1).astype(F32)
    mask = jnp.zeros((tm, ne), F32)
    onehots, idxs, wts = [], [], []
    for _ in range(TOP_K):
        mx = jnp.max(sel, axis=-1, keepdims=True)
        ik = jnp.min(jnp.where(sel == mx, lane, float(ne)), axis=-1, keepdims=True)
        oh = lane == ik
        wts.append(jnp.sum(jnp.where(oh, scores, 0.0), axis=-1, keepdims=True))
        idxs.append(ik)
        onehots.append(oh)
        sel = jnp.where(oh, -jnp.inf, sel)
        mask = jnp.where(oh, 1.0, mask)
    wsum = wts[0]
    for t in wts[1:]:
        wsum = wsum + t
    cnt = jnp.dot(tri_ref[...], mask.astype(BF16), preferred_element_type=F32) + carry_ref[...]
    col = lax.broadcasted_iota(jnp.int32, (tm, LANES), 1)
    idx_o = jnp.zeros((tm, LANES), F32)
    wt_o = jnp.zeros((tm, LANES), F32)
    rank_o = jnp.zeros((tm, LANES), F32)
    for kk in range(TOP_K):
        rk = jnp.sum(jnp.where(onehots[kk], cnt, 0.0), axis=-1, keepdims=True)
        idx_o = jnp.where(col == kk, idxs[kk], idx_o)
        wt_o = jnp.where(col == kk, wts[kk] / wsum * ROUTED_SCALE, wt_o)
        rank_o = jnp.where(col == kk, rk, rank_o)
    idx_ref[...] = idx_o.T[:ROUTE_COLS].astype(jnp.int32)
    wt_ref[...] = wt_o[:, :ROUTE_COLS]
    rank_ref[...] = rank_o.T[:ROUTE_COLS].astype(jnp.int32)
    carry_ref[...] = carry_ref[...] + jnp.sum(mask, axis=0, keepdims=True)
    cnt_ref[...] = carry_ref[...]


def router(h, w_router, b_router, tm=1024):
    m, k = h.shape
    ne = w_router.shape[1]
    tm = min(tm, m)
    r = jnp.arange(tm)
    tri = (r[:, None] > r[None, :]).astype(BF16)
    small = pl.BlockSpec((tm, ROUTE_COLS), lambda i: (i, 0))
    small_t = pl.BlockSpec((ROUTE_COLS, tm), lambda i: (0, i))
    return pl.pallas_call(
        _router_kernel,
        grid=(m // tm,),
        in_specs=[pl.BlockSpec((tm, k), lambda i: (i, 0)),
                  pl.BlockSpec((k, ne), lambda i: (0, 0)),
                  pl.BlockSpec((1, ne), lambda i: (0, 0)),
                  pl.BlockSpec((tm, tm), lambda i: (0, 0))],
        out_specs=[small_t, small, small_t, pl.BlockSpec((1, ne), lambda i: (0, 0))],
        out_shape=[jax.ShapeDtypeStruct((ROUTE_COLS, m), jnp.int32),
                   jax.ShapeDtypeStruct((m, ROUTE_COLS), F32),
                   jax.ShapeDtypeStruct((ROUTE_COLS, m), jnp.int32),
                   jax.ShapeDtypeStruct((1, ne), F32)],
        scratch_shapes=[pltpu.VMEM((1, ne), F32)],
        compiler_params=_params("arbitrary"),
        name="router",
    )(h, w_router, b_router, tri)


def _experts_kernel(be_ref, bv_ref, li_ref, x_ref, wg_ref, wu_ref, wd_ref, o_ref, wgb_ref,
                    wub_ref, wdb_ref):
    i = pl.program_id(0)

    @pl.when(jnp.logical_or(i == 0, be_ref[i] != be_ref[jnp.maximum(i - 1, 0)]))
    def _():
        wgb_ref[...] = wg_ref[0, 0].astype(BF16)
        wub_ref[...] = wu_ref[0, 0].astype(BF16)
        wdb_ref[...] = wd_ref[0, 0].astype(BF16)

    @pl.when(bv_ref[i] > 0)
    def _():
        half = x_ref.shape[1]
        xa, xb = _unpack_halves(x_ref[...])
        xa = xa.astype(BF16)
        xb = xb.astype(BF16)
        g = (jnp.dot(xa, wgb_ref[:half, :], preferred_element_type=F32)
             + jnp.dot(xb, wgb_ref[half:, :], preferred_element_type=F32))
        u = (jnp.dot(xa, wub_ref[:half, :], preferred_element_type=F32)
             + jnp.dot(xb, wub_ref[half:, :], preferred_element_type=F32))
        a = (g * _sigmoid(g) * u).astype(BF16)
        o_ref[...] = _pack_halves(jnp.dot(a, wdb_ref[...], preferred_element_type=F32))

    @pl.when(bv_ref[i] == 0)
    def _():
        o_ref[...] = jnp.zeros_like(o_ref)


def experts(xs, block_e, block_valid, layer, w_gate, w_up, w_down):
    r, half = xs.shape
    d = 2 * half
    ff = w_gate.shape[3]
    nblk = r // MOE_BLOCK
    return pl.pallas_call(
        _experts_kernel,
        grid_spec=pltpu.PrefetchScalarGridSpec(
            num_scalar_prefetch=3,
            grid=(nblk,),
            in_specs=[pl.BlockSpec((MOE_BLOCK, half), lambda i, be, bv, li: (i, 0)),
                      pl.BlockSpec((1, 1, d, ff), lambda i, be, bv, li: (li[0], be[i], 0, 0)),
                      pl.BlockSpec((1, 1, d, ff), lambda i, be, bv, li: (li[0], be[i], 0, 0)),
                      pl.BlockSpec((1, 1, ff, d), lambda i, be, bv, li: (li[0], be[i], 0, 0))],
            out_specs=pl.BlockSpec((MOE_BLOCK, half), lambda i, be, bv, li: (i, 0)),
            scratch_shapes=[pltpu.VMEM((d, ff), BF16), pltpu.VMEM((d, ff), BF16),
                            pltpu.VMEM((ff, d), BF16)]),
        out_shape=jax.ShapeDtypeStruct((r, half), jnp.uint32),
        compiler_params=_params("arbitrary"),
        name="experts",
    )(block_e, block_valid, layer, xs, w_gate, w_up, w_down)


def _shared_ln_kernel(h_ref, yg_ref, wt_ref, wgu_ref, wd_ref, g_ref, b_ref, o_ref, ob_ref):
    ff = wd_ref.shape[0]
    gu = jnp.dot(h_ref[...].astype(BF16), wgu_ref[...], preferred_element_type=F32)
    g = gu[:, :ff]
    a = (g * _sigmoid(g) * gu[:, ff:]).astype(BF16)
    shared = jnp.dot(a, wd_ref[...], preferred_element_type=F32)
    wt = wt_ref[...]
    r_lo = r_hi = None
    for kk in range(TOP_K):
        lo, hi = _unpack_halves(yg_ref[kk])
        wk = wt[:, kk:kk + 1]
        r_lo = wk * lo if r_lo is None else r_lo + wk * lo
        r_hi = wk * hi if r_hi is None else r_hi + wk * hi
    routed = jnp.concatenate([r_lo, r_hi], axis=1)
    out = _layernorm_rows(ALPHA * h_ref[...] + (routed + shared), g_ref[...], b_ref[...])
    o_ref[...] = out
    ob_ref[...] = out.astype(BF16)


def shared_ln(h, yg, wt, w_gu, w_down, g, b, tm=256):
    m, d = h.shape
    ff2 = w_gu.shape[1]
    tm = min(tm, m)
    row = pl.BlockSpec((tm, d), lambda i: (i, 0))
    vec = pl.BlockSpec((1, d), lambda i: (0, 0))
    return pl.pallas_call(
        _shared_ln_kernel,
        grid=(m // tm,),
        in_specs=[row,
                  pl.BlockSpec((TOP_K, tm, d // 2), lambda i: (0, i, 0)),
                  pl.BlockSpec((tm, ROUTE_COLS), lambda i: (i, 0)),
                  pl.BlockSpec((d, ff2), lambda i: (0, 0)),
                  pl.BlockSpec((ff2 // 2, d), lambda i: (0, 0)),
                  vec, vec],
        out_specs=[row, row],
        out_shape=[jax.ShapeDtypeStruct((m, d), F32), jax.ShapeDtypeStruct((m, d), BF16)],
        compiler_params=_params("parallel"),
        name="shared_ln",
    )(h, yg, wt, w_gu, w_down, g, b)


def _mixer_layout(d):
    rq = RET_H * RET_DK
    rv = RET_H * RET_DV
    fq = FOX_H * FOX_DH
    main = 2 * rq + 2 * rv + 3 * fq
    ff0 = main
    cu0 = ff0 + FOX_H
    gl0 = cu0 + 2 * d
    end = gl0 + N_BRANCH * d
    return dict(rq=rq, rv=rv, fq=fq, main=main, ff0=ff0, cu0=cu0, gl0=gl0, end=end)


def _prep_in_weights(w_in, b_in, d):
    lay = _mixer_layout(d)
    pad = LANES - FOX_H
    w = jnp.concatenate([w_in[:, :lay["main"]], w_in[:, lay["cu0"]:lay["end"]]],
                        axis=1).astype(BF16)
    bb = jnp.concatenate([b_in[:lay["main"]], b_in[lay["cu0"]:lay["end"]]])
    w_ff = jnp.pad(w_in[:, lay["ff0"]:lay["cu0"]], ((0, 0), (0, pad))).astype(BF16)
    b_ff = jnp.pad(b_in[lay["ff0"]:lay["cu0"]], (0, pad))
    return w, bb.reshape(1, -1).astype(F32), w_ff, b_ff.reshape(1, -1).astype(F32)


def _moe_sublayer(h, h_pk, routing, layer, w_gate, w_up, w_down, w_sh_gu, w_sh_down, g, b):
    t, d = h.shape
    idx8, wt8, rank8, cnt = routing
    ne = cnt.shape[1]
    idx = idx8[:TOP_K]
    rank = rank8[:TOP_K]
    sizes = cnt[0].astype(jnp.int32)
    a = t * TOP_K
    n_blocks = -(-a // MOE_BLOCK) + ne
    r = n_blocks * MOE_BLOCK
    padded = (sizes + MOE_BLOCK - 1) // MOE_BLOCK * MOE_BLOCK
    pad_end = jnp.cumsum(padded)
    pad_start = pad_end - padded
    start_of = jnp.sum(jnp.where(idx[..., None] == jnp.arange(ne, dtype=jnp.int32), pad_start, 0),
                       axis=-1)
    dest = lax.optimization_barrier(start_of + rank)
    shift = (t + MOE_BLOCK).bit_length()
    tok = jnp.arange(t, dtype=jnp.int32)[None, :]
    real_keys = ((idx << shift) | tok).reshape(a)
    slot = jnp.arange(r - a, dtype=jnp.int32)
    slot_e = jnp.minimum(slot // MOE_BLOCK, ne - 1)
    slot_q = slot % MOE_BLOCK
    needed = (slot < ne * MOE_BLOCK) & (slot_q < (padded - sizes)[slot_e])
    pad_keys = jnp.where(needed, (slot_e << shift) | (t + slot_q), (ne << shift) + slot)
    order = jnp.sort(jnp.concatenate([real_keys, pad_keys]))
    row_tok = (order & ((1 << shift) - 1)) % t
    blk0 = jnp.arange(n_blocks, dtype=jnp.int32) * MOE_BLOCK
    block_e = jnp.minimum(jnp.sum((pad_end[None, :] <= blk0[:, None]).astype(jnp.int32), axis=1),
                          ne - 1)
    block_valid = (blk0 < pad_end[-1]).astype(jnp.int32)
    xs = h_pk[row_tok]
    ys = experts(xs, block_e, block_valid, layer, w_gate, w_up, w_down)
    yg = ys[dest.reshape(a)].reshape(TOP_K, t, d // 2)
    return shared_ln(h, yg, wt8, w_sh_gu, w_sh_down, g, b)


def kernel(x, w_in, b_in, ret_gn_g, conv_w, conv_b, conv_ln_g, conv_ln_b, w_ret_o, w_fox_o,
           w_conv_o, w_out, ln1_g, ln1_b, w_router, b_router, w_exp_gate, w_exp_up, w_exp_down,
           w_sh_gate, w_sh_up, w_sh_down, ln2_g, ln2_b):
    bsz, seq, d = x.shape
    t = bsz * seq
    lay = _mixer_layout(d)
    chunk = min(RET_CHUNK, seq)
    tables = _retention_tables(seq, chunk)
    fq0 = 2 * lay["rq"] + 2 * lay["rv"]

    def row(v):
        return v.reshape(1, -1)

    def layer(carry, p):
        h, h_bf = carry
        w_cat, b_cat, w_ff, b_ff = _prep_in_weights(p["w_in"], p["b_in"], d)
        z2 = in_proj(h_bf, w_cat, b_cat)
        z = z2.reshape(bsz, seq, -1)
        ret_o = retention(z, row(p["ret_gn_g"]), tables)
        negc = fox_cumsum(h_bf.reshape(bsz, seq, d), w_ff, b_ff)
        fox_o = fox_attention(z, negc, fq0 // FOX_DH, (fq0 + lay["fq"]) // FOX_DH,
                              (fq0 + 2 * lay["fq"]) // FOX_DH)
        conv_o = conformer_conv(z, lay["main"] // d, lay["main"] // d + 1, p["conv_w"],
                                row(p["conv_b"]), row(p["conv_ln_g"]), row(p["conv_ln_b"]))
        merged = merge(ret_o.reshape(t, -1), fox_o.reshape(t, -1), conv_o.reshape(t, -1),
                       p["w_ret_o"].astype(BF16), p["w_fox_o"].astype(BF16),
                       p["w_conv_o"].astype(BF16), z2, (lay["main"] + 2 * d) // d)
        h, h_pk = out_ln(merged, p["w_out"].astype(BF16), h, row(p["ln1_g"]), row(p["ln1_b"]))
        routing = router(h, p["w_router"], row(p["b_router"]))
        w_sh_gu = jnp.concatenate([p["w_sh_gate"], p["w_sh_up"]], axis=-1).astype(BF16)
        h, h_bf = _moe_sublayer(h, h_pk, routing, p["layer"],
                                w_exp_gate, w_exp_up, w_exp_down, w_sh_gu,
                                p["w_sh_down"].astype(BF16), row(p["ln2_g"]), row(p["ln2_b"]))
        return (h, h_bf), None

    depth = w_in.shape[0]
    params = dict(w_in=w_in, b_in=b_in, ret_gn_g=ret_gn_g, conv_w=conv_w, conv_b=conv_b,
                  conv_ln_g=conv_ln_g, conv_ln_b=conv_ln_b, w_ret_o=w_ret_o, w_fox_o=w_fox_o,
                  w_conv_o=w_conv_o, w_out=w_out, ln1_g=ln1_g, ln1_b=ln1_b, w_router=w_router,
                  b_router=b_router, w_sh_gate=w_sh_gate, w_sh_up=w_sh_up,
                  w_sh_down=w_sh_down, ln2_g=ln2_g, ln2_b=ln2_b,
                  layer=jnp.arange(depth, dtype=jnp.int32).reshape(depth, 1))
    h0 = x.reshape(t, d)
    (h, _), _ = lax.scan(layer, (h0, h0.astype(BF16)), params)
    return h.reshape(bsz, seq, d)
```

```python
import jax
import jax.numpy as jnp
from jax import lax
from jax.experimental import pallas as pl
from jax.experimental.pallas import tpu as pltpu

F32 = jnp.float32
BF16 = jnp.bfloat16

RET_H = 8
RET_DK = 128
RET_DV = 256
ROPE_BASE = 10000.0
FOX_H = 16
FOX_DH = 128
CONV_W = 31
N_BRANCH = 3
TOP_K = 6
ROUTED_SCALE = 2.5
DEPTH_FOR_NORM = 4
ALPHA = (2 * DEPTH_FOR_NORM) ** 0.25
LN_EPS = 1e-5

LANES = 128
SUBLANES = 8
VMEM_LIMIT = 56 * 1024 * 1024

IN_TM = 1024
IN_TN = 2048
RET_CHUNK = 256
FOX_TQ = 2048
FOX_TK = 512
FOX_HG = 1
CUM_BLK = 256
CONV_TS = 256
CONV_HALO = 32
CONV_RB = 256
CONV_CB = 128
MOE_BLOCK = 512
NEG_BIG = -1e30
LOG2E = 1.4426950408889634


def _params(*sem):
    return pltpu.CompilerParams(dimension_semantics=sem, vmem_limit_bytes=VMEM_LIMIT)


def _sigmoid(x):
    return 1.0 / (1.0 + jnp.exp(-x))


def _pack_halves(x):
    c = x.shape[1] // 2
    xb = x.astype(BF16).astype(F32)
    lo = lax.bitcast_convert_type(xb[:, :c], jnp.uint32) >> 16
    hi = lax.bitcast_convert_type(xb[:, c:], jnp.uint32) & jnp.uint32(0xFFFF0000)
    return lo | hi


def _unpack_halves(u):
    lo = lax.bitcast_convert_type(u << 16, F32)
    hi = lax.bitcast_convert_type(u & jnp.uint32(0xFFFF0000), F32)
    return lo, hi


def _layernorm_rows(x, g, b):
    mu = jnp.mean(x, axis=-1, keepdims=True)
    xc = x - mu
    var = jnp.mean(xc * xc, axis=-1, keepdims=True)
    return xc * lax.rsqrt(var + LN_EPS) * g + b


def _in_proj_kernel(x_ref, w_ref, b_ref, o_ref):
    acc = jnp.dot(x_ref[...], w_ref[...], preferred_element_type=F32)
    o_ref[...] = acc + b_ref[...]


def in_proj(x_bf, w_bf, b, tm=IN_TM, tn=IN_TN):
    m, k = x_bf.shape
    n = w_bf.shape[1]
    tm = min(tm, m)
    assert m % tm == 0 and n % tn == 0
    return pl.pallas_call(
        _in_proj_kernel,
        grid=(m // tm, n // tn),
        in_specs=[pl.BlockSpec((tm, k), lambda i, j: (i, 0)),
                  pl.BlockSpec((k, tn), lambda i, j: (0, j)),
                  pl.BlockSpec((1, tn), lambda i, j: (0, j))],
        out_specs=pl.BlockSpec((tm, tn), lambda i, j: (i, j)),
        out_shape=jax.ShapeDtypeStruct((m, n), F32),
        compiler_params=_params("parallel", "parallel"),
        name="in_proj",
    )(x_bf, w_bf, b)


def _retention_kernel(cd_ref, q_ref, k_ref, v_ref, g_ref, cos_ref, sin_ref, decay_ref,
                      xi_ref, zeta_ref, gn_ref, o_ref, state_ref):
    @pl.when(pl.program_id(1) == 0)
    def _():
        state_ref[...] = jnp.zeros_like(state_ref)

    cos = cos_ref[...]
    sin = sin_ref[...]
    for h in range(RET_H):
        qs = slice(h * RET_DK, (h + 1) * RET_DK)
        vs = slice(h * RET_DV, (h + 1) * RET_DV)
        q = q_ref[0, :, qs]
        k = k_ref[0, :, qs]
        qr = q * cos + pltpu.roll(q, RET_DK // 2, 1) * sin
        kr = (k * cos + pltpu.roll(k, RET_DK // 2, 1) * sin) * (RET_DK ** -0.5)
        vb = v_ref[0, :, vs].astype(BF16)
        s = lax.dot_general(qr.astype(BF16), kr.astype(BF16), (((1,), (1,)), ((), ())),
                            preferred_element_type=F32)
        inner = (s * decay_ref[h]).astype(BF16)
        o = jnp.dot(inner, vb, preferred_element_type=F32)
        st = state_ref[h]
        o = o + jnp.dot((qr * xi_ref[h]).astype(BF16), st.astype(BF16),
                        preferred_element_type=F32)
        kz = (kr * zeta_ref[h]).astype(BF16)
        state_ref[h] = st * cd_ref[h] + lax.dot_general(
            kz, vb, (((0,), (0,)), ((), ())), preferred_element_type=F32)
        mu = jnp.mean(o, axis=-1, keepdims=True)
        oc = o - mu
        var = jnp.mean(oc * oc, axis=-1, keepdims=True)
        on = oc * lax.rsqrt(var + LN_EPS) * gn_ref[:, vs]
        g = g_ref[0, :, vs]
        o_ref[0, :, vs] = (g * _sigmoid(g) * on).astype(BF16)


def _retention_tables(seq, chunk):
    h = jnp.arange(RET_H, dtype=F32)
    log_g = jnp.log1p(-jnp.exp2(-5.0 - h))
    i = jnp.arange(chunk, dtype=F32)
    diff = i[:, None] - i[None, :]
    decay = jnp.where(diff >= 0, jnp.exp(log_g[:, None, None] * jnp.maximum(diff, 0.0)), 0.0)
    xi = jnp.exp(log_g[:, None] * (i + 1.0))
    zeta = jnp.exp(log_g[:, None] * (chunk - 1.0 - i))
    xi = jnp.broadcast_to(xi[:, :, None], (RET_H, chunk, RET_DK))
    zeta = jnp.broadcast_to(zeta[:, :, None], (RET_H, chunk, RET_DK))
    cd = jnp.exp(log_g * chunk)
    half = RET_DK // 2
    inv = 1.0 / (ROPE_BASE ** (jnp.arange(half, dtype=F32) / half))
    ang = jnp.arange(seq, dtype=F32)[:, None] * inv[None, :]
    cos = jnp.cos(ang)
    sin = jnp.sin(ang)
    cos2 = jnp.concatenate([cos, cos], axis=-1)
    sin2 = jnp.concatenate([-sin, sin], axis=-1)
    return cd, cos2, sin2, decay, xi, zeta


def retention(z, gn_g, tables):
    b, s, _ = z.shape
    cd, cos2, sin2, decay, xi, zeta = tables
    L = decay.shape[1]
    qw = RET_H * RET_DK
    vw = RET_H * RET_DV
    return pl.pallas_call(
        _retention_kernel,
        grid=(b, s // L),
        in_specs=[pl.BlockSpec(memory_space=pltpu.SMEM),
                  pl.BlockSpec((1, L, qw), lambda bi, n: (bi, n, 0)),
                  pl.BlockSpec((1, L, qw), lambda bi, n: (bi, n, 1)),
                  pl.BlockSpec((1, L, vw), lambda bi, n: (bi, n, 1)),
                  pl.BlockSpec((1, L, vw), lambda bi, n: (bi, n, 2)),
                  pl.BlockSpec((L, RET_DK), lambda bi, n: (n, 0)),
                  pl.BlockSpec((L, RET_DK), lambda bi, n: (n, 0)),
                  pl.BlockSpec((RET_H, L, L), lambda bi, n: (0, 0, 0)),
                  pl.BlockSpec((RET_H, L, RET_DK), lambda bi, n: (0, 0, 0)),
                  pl.BlockSpec((RET_H, L, RET_DK), lambda bi, n: (0, 0, 0)),
                  pl.BlockSpec((1, vw), lambda bi, n: (0, 0))],
        out_specs=pl.BlockSpec((1, L, vw), lambda bi, n: (bi, n, 0)),
        out_shape=jax.ShapeDtypeStruct((b, s, vw), BF16),
        scratch_shapes=[pltpu.VMEM((RET_H, RET_DK, RET_DV), F32)],
        compiler_params=_params("parallel", "arbitrary"),
        name="retention",
    )(cd, z, z, z, z, cos2, sin2, decay, xi, zeta, gn_g)


def _fox_cumsum_kernel(h_ref, w_ref, b_ref, tri_ref, o_ref):
    s = h_ref.shape[1]
    tri = tri_ref[...]
    carry = jnp.zeros((1, LANES), F32)
    for i in range(s // CUM_BLK):
        f = jnp.dot(h_ref[0, i * CUM_BLK:(i + 1) * CUM_BLK, :], w_ref[...],
                    preferred_element_type=F32) + b_ref[...]
        ls = jnp.minimum(f, 0.0) - jnp.log1p(jnp.exp(-jnp.abs(f)))
        hi = ls.astype(BF16)
        r1 = ls - hi.astype(F32)
        mid = r1.astype(BF16)
        lo = (r1 - mid.astype(F32)).astype(BF16)
        c = (jnp.dot(tri, hi, preferred_element_type=F32)
             + jnp.dot(tri, mid, preferred_element_type=F32)
             + jnp.dot(tri, lo, preferred_element_type=F32)) + carry
        carry = c[CUM_BLK - 1:CUM_BLK, :]
        o_ref[0, :, i * CUM_BLK:(i + 1) * CUM_BLK] = (-c).T[:FOX_H, :]


def fox_cumsum(h_bf, w_ff, b_ff):
    b, s, d = h_bf.shape
    r = jnp.arange(CUM_BLK)
    tri = (r[:, None] >= r[None, :]).astype(BF16)
    return pl.pallas_call(
        _fox_cumsum_kernel,
        grid=(b,),
        in_specs=[pl.BlockSpec((1, s, d), lambda bi: (bi, 0, 0)),
                  pl.BlockSpec((d, LANES), lambda bi: (0, 0)),
                  pl.BlockSpec((1, LANES), lambda bi: (0, 0)),
                  pl.BlockSpec((CUM_BLK, CUM_BLK), lambda bi: (0, 0))],
        out_specs=pl.BlockSpec((1, FOX_H, s), lambda bi: (bi, 0, 0)),
        out_shape=jax.ShapeDtypeStruct((b, FOX_H, s), F32),
        compiler_params=_params("parallel"),
        name="fox_cumsum",
    )(h_bf, w_ff, b_ff, tri)


def _fox_attn_kernel(q_ref, k_ref, v_ref, nc_ref, o_ref, kt_ref, vx_ref, qb_ref, m_ref, acc_ref):
    qi = pl.program_id(2)
    s_len = k_ref.shape[1]

    def head(g):
        return slice(g * FOX_DH, (g + 1) * FOX_DH)

    @pl.when(qi == 0)
    def _():
        for g in range(FOX_HG):
            for c in range(s_len // FOX_TK):
                rows = slice(c * FOX_TK, (c + 1) * FOX_TK)
                kt_ref[g, :, rows] = k_ref[0, rows, head(g)].T.astype(BF16)
                vx_ref[g, rows, :FOX_DH] = v_ref[0, rows, head(g)].astype(BF16)
            vx_ref[g, :, FOX_DH:] = jnp.ones((s_len, FOX_DH), BF16)

    for g in range(FOX_HG):
        qb_ref[g] = (q_ref[0, :, head(g)] * (FOX_DH ** -0.5 * LOG2E)).astype(BF16)
    m_ref[...] = jnp.full_like(m_ref, NEG_BIG)
    acc_ref[...] = jnp.zeros_like(acc_ref)

    def tile(j, diag):
        k0 = pl.multiple_of(j * FOX_TK, FOX_TK)
        rows = slice(0 if diag is None else diag * FOX_TK, FOX_TQ)
        for g in range(FOX_HG):
            s = jnp.dot(qb_ref[g, rows, :], kt_ref[g, :, pl.ds(k0, FOX_TK)],
                        preferred_element_type=F32)
            s = s + nc_ref[0, g, :, pl.ds(k0, FOX_TK)] * LOG2E
            if diag is not None:
                row = lax.broadcasted_iota(jnp.int32, s.shape, 0)
                col = lax.broadcasted_iota(jnp.int32, s.shape, 1)
                s = jnp.where(col <= row, s, NEG_BIG)
            m_prev = m_ref[g, rows, :]
            m_new = jnp.maximum(m_prev, jnp.max(s, axis=-1, keepdims=True))
            alpha = jnp.exp2(m_prev - m_new)
            p = jnp.concatenate(
                [jnp.exp2(s[:, c * LANES:(c + 1) * LANES] - m_new)
                 for c in range(FOX_TK // LANES)], axis=1).astype(BF16)
            pv = jnp.dot(p, vx_ref[g, pl.ds(k0, FOX_TK), :], preferred_element_type=F32)
            acc_ref[g, rows, :] = jnp.concatenate([alpha, alpha], axis=1) * acc_ref[g, rows, :] + pv
            m_ref[g, rows, :] = m_new

    def body(j, c):
        tile(j, None)
        return c

    per_q = FOX_TQ // FOX_TK
    n_full = qi * per_q
    lax.fori_loop(0, n_full, body, 0)
    for r in range(per_q):
        tile(n_full + r, r)
    for g in range(FOX_HG):
        o_ref[0, :, head(g)] = (acc_ref[g, :, :FOX_DH] / acc_ref[g, :, FOX_DH:]).astype(BF16)


def fox_attention(z, negc, q_blk, k_blk, v_blk):
    b, s, _ = z.shape
    assert FOX_TQ % FOX_TK == 0 and FOX_DH == LANES and FOX_H % FOX_HG == 0
    gw = FOX_HG * FOX_DH
    negc4 = negc.reshape(b, FOX_H, 1, s)
    qg, kg, vg = q_blk // FOX_HG, k_blk // FOX_HG, v_blk // FOX_HG
    return pl.pallas_call(
        _fox_attn_kernel,
        grid=(b, FOX_H // FOX_HG, s // FOX_TQ),
        in_specs=[pl.BlockSpec((1, FOX_TQ, gw), lambda bi, h, qi: (bi, qi, qg + h)),
                  pl.BlockSpec((1, s, gw), lambda bi, h, qi: (bi, 0, kg + h)),
                  pl.BlockSpec((1, s, gw), lambda bi, h, qi: (bi, 0, vg + h)),
                  pl.BlockSpec((1, FOX_HG, 1, s), lambda bi, h, qi: (bi, h, 0, 0))],
        out_specs=pl.BlockSpec((1, FOX_TQ, gw), lambda bi, h, qi: (bi, qi, h)),
        out_shape=jax.ShapeDtypeStruct((b, s, FOX_H * FOX_DH), BF16),
        scratch_shapes=[pltpu.VMEM((FOX_HG, FOX_DH, s), BF16),
                        pltpu.VMEM((FOX_HG, s, 2 * FOX_DH), BF16),
                        pltpu.VMEM((FOX_HG, FOX_TQ, FOX_DH), BF16),
                        pltpu.VMEM((FOX_HG, FOX_TQ, LANES), F32),
                        pltpu.VMEM((FOX_HG, FOX_TQ, 2 * FOX_DH), F32)],
        compiler_params=_params("parallel", "parallel", "arbitrary"),
        name="fox_attn",
    )(z, z, z, negc4)


def _conv_kernel(a_ref, g_ref, ap_ref, gp_ref, w_ref, cb_ref, lg_ref, lb_ref, o_ref, y_ref,
                 c_ref):
    ts = a_ref.shape[1]
    c_all = a_ref.shape[2]
    prev = ap_ref[0] * _sigmoid(gp_ref[0])
    y_ref[0:CONV_HALO, :] = jnp.where(pl.program_id(1) > 0, prev, 0.0)
    y_ref[CONV_HALO:CONV_HALO + ts, :] = a_ref[0] * _sigmoid(g_ref[0])
    lead = CONV_HALO - (CONV_W - 1)

    n_cb = c_all // CONV_CB
    win_rows = CONV_RB + CONV_HALO

    def block(idx, carry):
        r0 = pl.multiple_of((idx // n_cb) * CONV_RB, CONV_RB)
        c0 = pl.multiple_of((idx % n_cb) * CONV_CB, CONV_CB)
        win = y_ref[pl.ds(r0, win_rows), pl.ds(c0, CONV_CB)]
        acc = jnp.zeros((CONV_RB, CONV_CB), F32)
        for ph in range(SUBLANES):
            rot = win if ph == 0 else pltpu.roll(win, win_rows - ph, 0)
            for w in range(CONV_W):
                if (w + lead) % SUBLANES == ph:
                    a0 = (w + lead) - ph
                    acc = acc + w_ref[w:w + 1, pl.ds(c0, CONV_CB)] * rot[a0:a0 + CONV_RB]
        c_ref[pl.ds(r0, CONV_RB), pl.ds(c0, CONV_CB)] = acc + cb_ref[:, pl.ds(c0, CONV_CB)]
        return carry

    lax.fori_loop(0, (ts // CONV_RB) * n_cb, block, 0)
    y = _layernorm_rows(c_ref[...], lg_ref[...], lb_ref[...])
    o_ref[0] = (y * _sigmoid(y)).astype(BF16)


def conformer_conv(z, a_blk, g_blk, conv_w, conv_b, ln_g, ln_b):
    b, s, _ = z.shape
    c = conv_w.shape[1]
    ts = min(CONV_TS, s)
    hb = ts // CONV_HALO

    def halo(col):
        return lambda bi, i: (bi, jnp.maximum(i * hb - 1, 0), col)

    return pl.pallas_call(
        _conv_kernel,
        grid=(b, s // ts),
        in_specs=[pl.BlockSpec((1, ts, c), lambda bi, i: (bi, i, a_blk)),
                  pl.BlockSpec((1, ts, c), lambda bi, i: (bi, i, g_blk)),
                  pl.BlockSpec((1, CONV_HALO, c), halo(a_blk)),
                  pl.BlockSpec((1, CONV_HALO, c), halo(g_blk)),
                  pl.BlockSpec((CONV_W, c), lambda bi, i: (0, 0)),
                  pl.BlockSpec((1, c), lambda bi, i: (0, 0)),
                  pl.BlockSpec((1, c), lambda bi, i: (0, 0)),
                  pl.BlockSpec((1, c), lambda bi, i: (0, 0))],
        out_specs=pl.BlockSpec((1, ts, c), lambda bi, i: (bi, i, 0)),
        out_shape=jax.ShapeDtypeStruct((b, s, c), BF16),
        scratch_shapes=[pltpu.VMEM((CONV_HALO + ts, c), F32),
                        pltpu.VMEM((ts, c), F32)],
        compiler_params=_params("parallel", "parallel"),
        name="conv",
    )(z, z, z, z, conv_w, conv_b, ln_g, ln_b)


def _merge_kernel(a_ref, f_ref, c_ref, wa_ref, wf_ref, wc_ref, g0_ref, g1_ref, g2_ref, o_ref):
    ya = jnp.dot(a_ref[...], wa_ref[...], preferred_element_type=F32)
    m = _sigmoid(g0_ref[...]) * ya
    yf = jnp.dot(f_ref[...], wf_ref[...], preferred_element_type=F32)
    m = m + _sigmoid(g1_ref[...]) * yf
    yc = jnp.dot(c_ref[...], wc_ref[...], preferred_element_type=F32)
    m = m + _sigmoid(g2_ref[...]) * yc
    o_ref[...] = m.astype(BF16)


def merge(ret_o, fox_o, conv_o, w_ret, w_fox, w_conv, z2, gl_blk, tm=1024, tn=256):
    m, k = ret_o.shape
    n = w_ret.shape[1]
    tm = min(tm, m)
    nb = n // tn
    act = pl.BlockSpec((tm, k), lambda i, j: (i, 0))
    wsp = pl.BlockSpec((k, tn), lambda i, j: (0, j))

    def gate(br):
        return pl.BlockSpec((tm, tn), lambda i, j: (i, gl_blk * nb + br * nb + j))

    return pl.pallas_call(
        _merge_kernel,
        grid=(m // tm, nb),
        in_specs=[act, act, act, wsp, wsp, wsp, gate(0), gate(1), gate(2)],
        out_specs=pl.BlockSpec((tm, tn), lambda i, j: (i, j)),
        out_shape=jax.ShapeDtypeStruct((m, n), BF16),
        compiler_params=_params("parallel", "parallel"),
        name="merge",
    )(ret_o, fox_o, conv_o, w_ret, w_fox, w_conv, z2, z2, z2)


def _out_ln_kernel(x_ref, w_ref, h_ref, g_ref, b_ref, o_ref, op_ref):
    y = jnp.dot(x_ref[...], w_ref[...], preferred_element_type=F32)
    out = _layernorm_rows(ALPHA * h_ref[...] + y, g_ref[...], b_ref[...])
    o_ref[...] = out
    op_ref[...] = _pack_halves(out)


def out_ln(merged, w_out, h, g, b, tm=512):
    m, k = merged.shape
    n = w_out.shape[1]
    tm = min(tm, m)
    return pl.pallas_call(
        _out_ln_kernel,
        grid=(m // tm,),
        in_specs=[pl.BlockSpec((tm, k), lambda i: (i, 0)),
                  pl.BlockSpec((k, n), lambda i: (0, 0)),
                  pl.BlockSpec((tm, n), lambda i: (i, 0)),
                  pl.BlockSpec((1, n), lambda i: (0, 0)),
                  pl.BlockSpec((1, n), lambda i: (0, 0))],
        out_specs=[pl.BlockSpec((tm, n), lambda i: (i, 0)),
                   pl.BlockSpec((tm, n // 2), lambda i: (i, 0))],
        out_shape=[jax.ShapeDtypeStruct((m, n), F32),
                   jax.ShapeDtypeStruct((m, n // 2), jnp.uint32)],
        compiler_params=_params("parallel"),
        name="out_ln",
    )(merged, w_out, h, g, b)


ROUTE_COLS = 8


def _router_kernel(h_ref, w_ref, b_ref, tri_ref, idx_ref, wt_ref, rank_ref, cnt_ref, carry_ref):
    @pl.when(pl.program_id(0) == 0)
    def _():
        carry_ref[...] = jnp.zeros_like(carry_ref)

    h = h_ref[...]
    hh = h.astype(BF16)
    hl = (h - hh.astype(F32)).astype(BF16)
    w = w_ref[...]
    wh = w.astype(BF16)
    wl = (w - wh.astype(F32)).astype(BF16)
    logits = (jnp.dot(hh, wh, preferred_element_type=F32)
              + jnp.dot(hl, wh, preferred_element_type=F32)
              + jnp.dot(hh, wl, preferred_element_type=F32))
    scores = _sigmoid(logits)
    tm, ne = scores.shape
    sel = scores + b_ref[...]
    lane = lax.broadcasted_iota(jnp.int32, (tm, ne), 1).astype(F32)
    mask = jnp.zeros((tm, ne), F32)
    onehots, idxs, wts = [], [], []
    for _ in range(TOP_K):
        mx = jnp.max(sel, axis=-1, keepdims=True)
        ik = jnp.min(jnp.where(sel == mx, lane, float(ne)), axis=-1, keepdims=True)
        oh = lane == ik
        wts.append(jnp.sum(jnp.where(oh, scores, 0.0), axis=-1, keepdims=True))
        idxs.append(ik)
        onehots.append(oh)
        sel = jnp.where(oh, -jnp.inf, sel)
        mask = jnp.where(oh, 1.0, mask)
    wsum = wts[0]
    for t in wts[1:]:
        wsum = wsum + t
    cnt = jnp.dot(tri_ref[...], mask.astype(BF16), preferred_element_type=F32) + carry_ref[...]
    col = lax.broadcasted_iota(jnp.int32, (tm, LANES), 1)
    idx_o = jnp.zeros((tm, LANES), F32)
    wt_o = jnp.zeros((tm, LANES), F32)
    rank_o = jnp.zeros((tm, LANES), F32)
    for kk in range(TOP_K):
        rk = jnp.sum(jnp.where(onehots[kk], cnt, 0.0), axis=-1, keepdims=True)
        idx_o = jnp.where(col == kk, idxs[kk], idx_o)
        wt_o = jnp.where(col == kk, wts[kk] / wsum * ROUTED_SCALE, wt_o)
        rank_o = jnp.where(col == kk, rk, rank_o)
    idx_ref[...] = idx_o.T[:ROUTE_COLS].astype(jnp.int32)
    wt_ref[...] = wt_o[:, :ROUTE_COLS]
    rank_ref[...] = rank_o.T[:ROUTE_COLS].astype(jnp.int32)
    carry_ref[...] = carry_ref[...] + jnp.sum(mask, axis=0, keepdims=True)
    cnt_ref[...] = carry_ref[...]


def router(h, w_router, b_router, tm=1024):
    m, k = h.shape
    ne = w_router.shape[1]
    tm = min(tm, m)
    r = jnp.arange(tm)
    tri = (r[:, None] > r[None, :]).astype(BF16)
    small = pl.BlockSpec((tm, ROUTE_COLS), lambda i: (i, 0))
    small_t = pl.BlockSpec((ROUTE_COLS, tm), lambda i: (0, i))
    return pl.pallas_call(
        _router_kernel,
        grid=(m // tm,),
        in_specs=[pl.BlockSpec((tm, k), lambda i: (i, 0)),
                  pl.BlockSpec((k, ne), lambda i: (0, 0)),
                  pl.BlockSpec((1, ne), lambda i: (0, 0)),
                  pl.BlockSpec((tm, tm), lambda i: (0, 0))],
        out_specs=[small_t, small, small_t, pl.BlockSpec((1, ne), lambda i: (0, 0))],
        out_shape=[jax.ShapeDtypeStruct((ROUTE_COLS, m), jnp.int32),
                   jax.ShapeDtypeStruct((m, ROUTE_COLS), F32),
                   jax.ShapeDtypeStruct((ROUTE_COLS, m), jnp.int32),
                   jax.ShapeDtypeStruct((1, ne), F32)],
        scratch_shapes=[pltpu.VMEM((1, ne), F32)],
        compiler_params=_params("arbitrary"),
        name="router",
    )(h, w_router, b_router, tri)


def _experts_kernel(be_ref, bv_ref, li_ref, x_ref, wg_ref, wu_ref, wd_ref, o_ref, wgb_ref,
                    wub_ref, wdb_ref):
    i = pl.program_id(0)

    @pl.when(jnp.logical_or(i == 0, be_ref[i] != be_ref[jnp.maximum(i - 1, 0)]))
    def _():
        wgb_ref[...] = wg_ref[0, 0].astype(BF16)
        wub_ref[...] = wu_ref[0, 0].astype(BF16)
        wdb_ref[...] = wd_ref[0, 0].astype(BF16)

    @pl.when(bv_ref[i] > 0)
    def _():
        half = x_ref.shape[1]
        xa, xb = _unpack_halves(x_ref[...])
        xa = xa.astype(BF16)
        xb = xb.astype(BF16)
        g = (jnp.dot(xa, wgb_ref[:half, :], preferred_element_type=F32)
             + jnp.dot(xb, wgb_ref[half:, :], preferred_element_type=F32))
        u = (jnp.dot(xa, wub_ref[:half, :], preferred_element_type=F32)
             + jnp.dot(xb, wub_ref[half:, :], preferred_element_type=F32))
        a = (g * _sigmoid(g) * u).astype(BF16)
        o_ref[...] = _pack_halves(jnp.dot(a, wdb_ref[...], preferred_element_type=F32))

    @pl.when(bv_ref[i] == 0)
    def _():
        o_ref[...] = jnp.zeros_like(o_ref)


def experts(xs, block_e, block_valid, layer, w_gate, w_up, w_down):
    r, half = xs.shape
    d = 2 * half
    ff = w_gate.shape[3]
    nblk = r // MOE_BLOCK
    return pl.pallas_call(
        _experts_kernel,
        grid_spec=pltpu.PrefetchScalarGridSpec(
            num_scalar_prefetch=3,
            grid=(nblk,),
            in_specs=[pl.BlockSpec((MOE_BLOCK, half), lambda i, be, bv, li: (i, 0)),
                      pl.BlockSpec((1, 1, d, ff), lambda i, be, bv, li: (li[0], be[i], 0, 0)),
                      pl.BlockSpec((1, 1, d, ff), lambda i, be, bv, li: (li[0], be[i], 0, 0)),
                      pl.BlockSpec((1, 1, ff, d), lambda i, be, bv, li: (li[0], be[i], 0, 0))],
            out_specs=pl.BlockSpec((MOE_BLOCK, half), lambda i, be, bv, li: (i, 0)),
            scratch_shapes=[pltpu.VMEM((d, ff), BF16), pltpu.VMEM((d, ff), BF16),
                            pltpu.VMEM((ff, d), BF16)]),
        out_shape=jax.ShapeDtypeStruct((r, half), jnp.uint32),
        compiler_params=_params("arbitrary"),
        name="experts",
    )(block_e, block_valid, layer, xs, w_gate, w_up, w_down)


def _shared_ln_kernel(h_ref, yg_ref, wt_ref, wgu_ref, wd_ref, g_ref, b_ref, o_ref, ob_ref):
    ff = wd_ref.shape[0]
    gu = jnp.dot(h_ref[...].astype(BF16), wgu_ref[...], preferred_element_type=F32)
    g = gu[:, :ff]
    a = (g * _sigmoid(g) * gu[:, ff:]).astype(BF16)
    shared = jnp.dot(a, wd_ref[...], preferred_element_type=F32)
    wt = wt_ref[...]
    r_lo = r_hi = None
    for kk in range(TOP_K):
        lo, hi = _unpack_halves(yg_ref[kk])
        wk = wt[:, kk:kk + 1]
        r_lo = wk * lo if r_lo is None else r_lo + wk * lo
        r_hi = wk * hi if r_hi is None else r_hi + wk * hi
    routed = jnp.concatenate([r_lo, r_hi], axis=1)
    out = _layernorm_rows(ALPHA * h_ref[...] + (routed + shared), g_ref[...], b_ref[...])
    o_ref[...] = out
    ob_ref[...] = out.astype(BF16)


def shared_ln(h, yg, wt, w_gu, w_down, g, b, tm=256):
    m, d = h.shape
    ff2 = w_gu.shape[1]
    tm = min(tm, m)
    row = pl.BlockSpec((tm, d), lambda i: (i, 0))
    vec = pl.BlockSpec((1, d), lambda i: (0, 0))
    return pl.pallas_call(
        _shared_ln_kernel,
        grid=(m // tm,),
        in_specs=[row,
                  pl.BlockSpec((TOP_K, tm, d // 2), lambda i: (0, i, 0)),
                  pl.BlockSpec((tm, ROUTE_COLS), lambda i: (i, 0)),
                  pl.BlockSpec((d, ff2), lambda i: (0, 0)),
                  pl.BlockSpec((ff2 // 2, d), lambda i: (0, 0)),
                  vec, vec],
        out_specs=[row, row],
        out_shape=[jax.ShapeDtypeStruct((m, d), F32), jax.ShapeDtypeStruct((m, d), BF16)],
        compiler_params=_params("parallel"),
        name="shared_ln",
    )(h, yg, wt, w_gu, w_down, g, b)


def _mixer_layout(d):
    rq = RET_H * RET_DK
    rv = RET_H * RET_DV
    fq = FOX_H * FOX_DH
    main = 2 * rq + 2 * rv + 3 * fq
    ff0 = main
    cu0 = ff0 + FOX_H
    gl0 = cu0 + 2 * d
    end = gl0 + N_BRANCH * d
    return dict(rq=rq, rv=rv, fq=fq, main=main, ff0=ff0, cu0=cu0, gl0=gl0, end=end)


def _prep_in_weights(w_in, b_in, d):
    lay = _mixer_layout(d)
    pad = LANES - FOX_H
    w = jnp.concatenate([w_in[:, :lay["main"]], w_in[:, lay["cu0"]:lay["end"]]],
                        axis=1).astype(BF16)
    bb = jnp.concatenate([b_in[:lay["main"]], b_in[lay["cu0"]:lay["end"]]])
    w_ff = jnp.pad(w_in[:, lay["ff0"]:lay["cu0"]], ((0, 0), (0, pad))).astype(BF16)
    b_ff = jnp.pad(b_in[lay["ff0"]:lay["cu0"]], (0, pad))
    return w, bb.reshape(1, -1).astype(F32), w_ff, b_ff.reshape(1, -1).astype(F32)


def _moe_sublayer(h, h_pk, routing, layer, w_gate, w_up, w_down, w_sh_gu, w_sh_down, g, b):
    t, d = h.shape
    idx8, wt8, rank8, cnt = routing
    ne = cnt.shape[1]
    idx = idx8[:TOP_K]
    rank = rank8[:TOP_K]
    sizes = cnt[0].astype(jnp.int32)
    a = t * TOP_K
    n_blocks = -(-a // MOE_BLOCK) + ne
    r = n_blocks * MOE_BLOCK
    padded = (sizes + MOE_BLOCK - 1) // MOE_BLOCK * MOE_BLOCK
    pad_end = jnp.cumsum(padded)
    pad_start = pad_end - padded
    start_of = jnp.sum(jnp.where(idx[..., None] == jnp.arange(ne, dtype=jnp.int32), pad_start, 0),
                       axis=-1)
    dest = lax.optimization_barrier(start_of + rank)
    shift = (t + MOE_BLOCK).bit_length()
    tok = jnp.arange(t, dtype=jnp.int32)[None, :]
    real_keys = ((idx << shift) | tok).reshape(a)
    slot = jnp.arange(r - a, dtype=jnp.int32)
    slot_e = jnp.minimum(slot // MOE_BLOCK, ne - 1)
    slot_q = slot % MOE_BLOCK
    needed = (slot < ne * MOE_BLOCK) & (slot_q < (padded - sizes)[slot_e])
    pad_keys = jnp.where(needed, (slot_e << shift) | (t + slot_q), (ne << shift) + slot)
    order = jnp.sort(jnp.concatenate([real_keys, pad_keys]))
    row_tok = (order & ((1 << shift) - 1)) % t
    blk0 = jnp.arange(n_blocks, dtype=jnp.int32) * MOE_BLOCK
    block_e = jnp.minimum(jnp.sum((pad_end[None, :] <= blk0[:, None]).astype(jnp.int32), axis=1),
                          ne - 1)
    block_valid = (blk0 < pad_end[-1]).astype(jnp.int32)
    xs = h_pk[row_tok]
    ys = experts(xs, block_e, block_valid, layer, w_gate, w_up, w_down)
    yg = ys[dest.reshape(a)].reshape(TOP_K, t, d // 2)
    return shared_ln(h, yg, wt8, w_sh_gu, w_sh_down, g, b)


def kernel(x, w_in, b_in, ret_gn_g, conv_w, conv_b, conv_ln_g, conv_ln_b, w_ret_o, w_fox_o,
           w_conv_o, w_out, ln1_g, ln1_b, w_router, b_router, w_exp_gate, w_exp_up, w_exp_down,
           w_sh_gate, w_sh_up, w_sh_down, ln2_g, ln2_b):
    bsz, seq, d = x.shape
    t = bsz * seq
    lay = _mixer_layout(d)
    chunk = min(RET_CHUNK, seq)
    tables = _retention_tables(seq, chunk)
    fq0 = 2 * lay["rq"] + 2 * lay["rv"]

    def row(v):
        return v.reshape(1, -1)

    def layer(carry, p):
        h, h_bf = carry
        w_cat, b_cat, w_ff, b_ff = _prep_in_weights(p["w_in"], p["b_in"], d)
        z2 = in_proj(h_bf, w_cat, b_cat)
        z = z2.reshape(bsz, seq, -1)
        ret_o = retention(z, row(p["ret_gn_g"]), tables)
        negc = fox_cumsum(h_bf.reshape(bsz, seq, d), w_ff, b_ff)
        fox_o = fox_attention(z, negc, fq0 // FOX_DH, (fq0 + lay["fq"]) // FOX_DH,
                              (fq0 + 2 * lay["fq"]) // FOX_DH)
        conv_o = conformer_conv(z, lay["main"] // d, lay["main"] // d + 1, p["conv_w"],
                                row(p["conv_b"]), row(p["conv_ln_g"]), row(p["conv_ln_b"]))
        merged = merge(ret_o.reshape(t, -1), fox_o.reshape(t, -1), conv_o.reshape(t, -1),
                       p["w_ret_o"].astype(BF16), p["w_fox_o"].astype(BF16),
                       p["w_conv_o"].astype(BF16), z2, (lay["main"] + 2 * d) // d)
        h, h_pk = out_ln(merged, p["w_out"].astype(BF16), h, row(p["ln1_g"]), row(p["ln1_b"]))
        routing = router(h, p["w_router"], row(p["b_router"]))
        w_sh_gu = jnp.concatenate([p["w_sh_gate"], p["w_sh_up"]], axis=-1).astype(BF16)
        h, h_bf = _moe_sublayer(h, h_pk, routing, p["layer"],
                                w_exp_gate, w_exp_up, w_exp_down, w_sh_gu,
                                p["w_sh_down"].astype(BF16), row(p["ln2_g"]), row(p["ln2_b"]))
        return (h, h_bf), None

    depth = w_in.shape[0]
    params = dict(w_in=w_in, b_in=b_in, ret_gn_g=ret_gn_g, conv_w=conv_w, conv_b=conv_b,
                  conv_ln_g=conv_ln_g, conv_ln_b=conv_ln_b, w_ret_o=w_ret_o, w_fox_o=w_fox_o,
                  w_conv_o=w_conv_o, w_out=w_out, ln1_g=ln1_g, ln1_b=ln1_b, w_router=w_router,
                  b_router=b_router, w_sh_gate=w_sh_gate, w_sh_up=w_sh_up,
                  w_sh_down=w_sh_down, ln2_g=ln2_g, ln2_b=ln2_b,
                  layer=jnp.arange(depth, dtype=jnp.int32).reshape(depth, 1))
    h0 = x.reshape(t, d)
    carry = (h0, h0.astype(BF16))
    for l in range(depth):
        carry, _ = layer(carry, {name: v[l] for name, v in params.items()})
    return carry[0].reshape(bsz, seq, d)
```
